```python
import jax, jax.numpy as jnp
from jax import lax
import numpy as np

D_MODEL = 4096
BATCH = 2
SEQ = 8192
DEPTH = 2

D_MIX = D_MODEL
HEAD_DIM = 128
W_SB = D_MIX // 4
W_SWA = D_MIX // 4
W_CONV = D_MIX // 4
W_GMLP = D_MIX - W_SB - W_SWA - W_CONV
N_SB_HEADS = W_SB // HEAD_DIM
N_SWA_HEADS = W_SWA // HEAD_DIM
N_SWA_KV = max(1, N_SWA_HEADS // 4)
W_SWA_KV = N_SWA_KV * HEAD_DIM
GMLP_GROUP_DIM = 128
N_GMLP_GROUPS = W_GMLP // GMLP_GROUP_DIM
SB_BLOCK = 128
SWA_BLOCK = 128
WINDOW = 128
GMLP_CHUNK = 128
CONV_WIDTH = 3
EPS = 1e-6

SPLITS = (W_SB, W_SB, W_SB, W_SB,
          W_SWA, W_SWA_KV, W_SWA_KV, W_SWA,
          W_CONV, W_CONV, W_CONV, W_CONV,
          W_GMLP, W_GMLP, W_GMLP)
D_IN_PROJ = sum(SPLITS)
SPLIT_IDX = tuple(int(c) for c in np.cumsum(SPLITS)[:-1])

kernel_name = "hybrid_sb_swa_conv_gmlp_block"


def rmsnorm(x, g):
    xf = x.astype(jnp.float32)
    y = xf * lax.rsqrt(jnp.mean(xf * xf, axis=-1, keepdims=True) + EPS)
    return (y * g.astype(jnp.float32)).astype(x.dtype)


def stick_breaking_attention(q, k, v):
    B, S, H, Dh = q.shape
    nblk = S // SB_BLOCK
    scale = Dh ** -0.5
    qb = q.reshape(B, nblk, SB_BLOCK, H, Dh).transpose(1, 0, 3, 2, 4)
    key_pos = jnp.arange(S)

    def one_block(args):
        qi, i = args
        z = jnp.einsum('bhqd,bshd->bhqs', qi, k).astype(jnp.float32) * scale
        q_pos = i * SB_BLOCK + jnp.arange(SB_BLOCK)
        mask = key_pos[None, :] < q_pos[:, None]
        log_beta = jax.nn.log_sigmoid(z)
        log_1m_beta = jnp.where(mask, jax.nn.log_sigmoid(-z), 0.0)
        rev = lax.cumsum(log_1m_beta, axis=3, reverse=True)
        excl = jnp.concatenate([rev[..., 1:], jnp.zeros_like(rev[..., :1])], axis=-1)
        w = jnp.where(mask, jnp.exp(log_beta + excl), 0.0)
        return jnp.einsum('bhqs,bshd->bqhd', w.astype(v.dtype), v)

    out = lax.map(one_block, (qb, jnp.arange(nblk)))
    return out.transpose(1, 0, 2, 3, 4).reshape(B, S, H, Dh)


def sliding_window_sink_attention(q, k, v, sinks):
    B, S, H, Dh = q.shape
    Hkv = k.shape[2]
    G = H // Hkv
    nblk = S // SWA_BLOCK
    scale = Dh ** -0.5
    qb = q.reshape(B, nblk, SWA_BLOCK, Hkv, G, Dh)

    def banded(t):
        tb = t.reshape(B, nblk, SWA_BLOCK, Hkv, Dh)
        prev = jnp.pad(tb, ((0, 0), (1, 0), (0, 0), (0, 0), (0, 0)))[:, :-1]
        return jnp.concatenate([prev, tb], axis=2)

    kb, vb = banded(k), banded(v)
    s = jnp.einsum('bnqkgd,bnskd->bnkgqs', qb, kb).astype(jnp.float32) * scale
    q_rel = jnp.arange(SWA_BLOCK) + SWA_BLOCK
    k_rel = jnp.arange(2 * SWA_BLOCK)
    diff = q_rel[:, None] - k_rel[None, :]
    in_window = (diff >= 0) & (diff < WINDOW)
    k_abs = jnp.arange(nblk)[:, None] * SWA_BLOCK - SWA_BLOCK + k_rel[None, :]
    mask = in_window[None] & (k_abs >= 0)[:, None, :]
    s = jnp.where(mask[None, :, None, None], s, -jnp.inf)
    sink = sinks.astype(jnp.float32).reshape(Hkv, G)[None, None, :, :, None, None]
    m = jnp.maximum(jnp.max(s, axis=-1, keepdims=True), sink)
    p = jnp.exp(s - m)
    p = p / (jnp.sum(p, axis=-1, keepdims=True) + jnp.exp(sink - m))
    out = jnp.einsum('bnkgqs,bnskd->bnqkgd', p.astype(v.dtype), vb)
    return out.reshape(B, S, H, Dh)


def short_conv_mixer(x_in, b_gate, c_gate, conv_w):
    S = x_in.shape[1]
    h = c_gate * x_in
    hp = jnp.pad(h, ((0, 0), (CONV_WIDTH - 1, 0), (0, 0)))
    y = conv_w[0] * hp[:, 0:S]
    for j in range(1, CONV_WIDTH):
        y = y + conv_w[j] * hp[:, j:j + S]
    return b_gate * y


def chunked_spatial_gating(u, v, v_gain, w_s, b_s):
    B, S, W = v.shape
    G = w_s.shape[0]
    nch = S // GMLP_CHUNK
    v = rmsnorm(v, v_gain)
    vc = v.reshape(B, nch, GMLP_CHUNK, G, W // G)
    causal = jnp.tril(jnp.ones((GMLP_CHUNK, GMLP_CHUNK), dtype=bool))
    ws = jnp.where(causal[None], w_s, 0.0).astype(v.dtype)
    mixed = jnp.einsum('gts,bnsgc->bntgc', ws, vc) + b_s.T[None, None, :, :, None]
    return u * mixed.reshape(B, S, W)


def hybrid_layer(x, norm_g, w_in, conv_w, v_gain, w_s, b_s, sinks, w_out):
    B, S, _ = x.shape
    h = rmsnorm(x, norm_g)
    proj = h @ w_in
    (q_a, k_a, v_a, g_a,
     q_b, k_b, v_b, g_b,
     x_c, b_c, c_c, g_c,
     u_d, v_d, g_d) = jnp.split(proj, SPLIT_IDX, axis=-1)

    def heads(t, n):
        return t.reshape(B, S, n, HEAD_DIM)

    y_a = stick_breaking_attention(heads(q_a, N_SB_HEADS), heads(k_a, N_SB_HEADS),
                                   heads(v_a, N_SB_HEADS)).reshape(B, S, W_SB)
    y_b = sliding_window_sink_attention(heads(q_b, N_SWA_HEADS), heads(k_b, N_SWA_KV),
                                        heads(v_b, N_SWA_KV), sinks).reshape(B, S, W_SWA)
    y_c = short_conv_mixer(x_c, b_c, c_c, conv_w)
    y_d = chunked_spatial_gating(u_d, v_d, v_gain, w_s, b_s)
    y = jnp.concatenate([y_a * jax.nn.silu(g_a), y_b * jax.nn.silu(g_b),
                         y_c * jax.nn.silu(g_c), y_d * jax.nn.silu(g_d)], axis=-1)
    return x + y @ w_out


def setup_inputs(seed: int = 0) -> dict:
    key = jax.random.key(seed)
    ks = jax.random.split(key, 11)
    f32 = jnp.float32
    x = jax.random.normal(ks[0], (BATCH, SEQ, D_MODEL), f32)
    norm_gain = 1.0 + 0.02 * jax.random.normal(ks[1], (DEPTH, D_MODEL), f32)
    w_in = jax.random.normal(ks[2], (DEPTH, D_MODEL, D_IN_PROJ), f32) * D_MODEL ** -0.5
    conv_w = jax.random.normal(ks[3], (DEPTH, CONV_WIDTH, W_CONV), f32) * CONV_WIDTH ** -0.5
    gmlp_v_gain = 1.0 + 0.02 * jax.random.normal(ks[4], (DEPTH, W_GMLP), f32)
    gmlp_w_s = jax.random.normal(ks[5], (DEPTH, N_GMLP_GROUPS, GMLP_CHUNK, GMLP_CHUNK), f32) * GMLP_CHUNK ** -0.5
    gmlp_b_s = 1.0 + 0.01 * jax.random.normal(ks[6], (DEPTH, N_GMLP_GROUPS, GMLP_CHUNK), f32)
    swa_sinks = jax.random.normal(ks[7], (DEPTH, N_SWA_HEADS), f32)
    w_out = jax.random.normal(ks[8], (DEPTH, D_MIX, D_MODEL), f32) * D_MIX ** -0.5
    final_gain = 1.0 + 0.02 * jax.random.normal(ks[9], (D_MODEL,), f32)
    return {"x": x, "norm_gain": norm_gain, "w_in": w_in, "conv_w": conv_w,
            "gmlp_v_gain": gmlp_v_gain, "gmlp_w_s": gmlp_w_s, "gmlp_b_s": gmlp_b_s,
            "swa_sinks": swa_sinks, "w_out": w_out, "final_gain": final_gain}


def reference(x, norm_gain, w_in, conv_w, gmlp_v_gain, gmlp_w_s, gmlp_b_s, swa_sinks, w_out, final_gain):
    h = x
    for l in range(DEPTH):
        h = hybrid_layer(h, norm_gain[l], w_in[l], conv_w[l], gmlp_v_gain[l],
                         gmlp_w_s[l], gmlp_b_s[l], swa_sinks[l], w_out[l])
    return rmsnorm(h, final_gain)
```

```python
import functools

import jax
import jax.numpy as jnp
from jax import lax
from jax.experimental import pallas as pl
from jax.experimental.pallas import tpu as pltpu

HEAD_DIM = 128
GMLP_GROUP_DIM = 128
GMLP_CHUNK = 128
WINDOW = 128
CONV_WIDTH = 3
EPS = 1e-6

LANES = 128
SUBLANES = 8
V7X_VMEM_BYTES = 64 * 1024 * 1024
VMEM_SPILL_ALLOWANCE = 6 * 1024 * 1024
MASKED_SCORE = -1e30

F32 = jnp.float32
BF16 = jnp.bfloat16


def _compiler_params(semantics, block_bytes):
    limit = min(2 * block_bytes + VMEM_SPILL_ALLOWANCE, V7X_VMEM_BYTES - 4 * 1024 * 1024)
    return pltpu.CompilerParams(dimension_semantics=semantics, vmem_limit_bytes=int(limit))


def _col_block(offset, width):
    assert offset % width == 0, (offset, width)
    return offset // width


def _silu(g):
    return g * jax.nn.sigmoid(g)


def _rmsnorm_kernel(x_ref, g_ref, o_ref):
    x = x_ref[...]
    ms = jnp.mean(x * x, axis=-1, keepdims=True)
    o_ref[...] = (x * lax.rsqrt(ms + EPS) * g_ref[...]).astype(o_ref.dtype)


def _rmsnorm(x, gain, out_dtype):
    m, d = x.shape
    tm = min(256, m)
    return pl.pallas_call(
        _rmsnorm_kernel,
        grid=(m // tm,),
        in_specs=[pl.BlockSpec((tm, d), lambda i: (i, 0)),
                  pl.BlockSpec((1, d), lambda i: (0, 0))],
        out_specs=pl.BlockSpec((tm, d), lambda i: (i, 0)),
        out_shape=jax.ShapeDtypeStruct((m, d), out_dtype),
        compiler_params=_compiler_params(("parallel",), tm * d * 12),
        name="rmsnorm",
    )(x, gain.reshape(1, d))


def _inproj_kernel(h_ref, w_ref, s_ref, o_ref):
    acc = jnp.dot(h_ref[...], w_ref[...], preferred_element_type=F32)
    o_ref[...] = (acc * s_ref[...]).astype(o_ref.dtype)


def _inproj(hn, w, col_scale):
    m, d = hn.shape
    n = w.shape[1]
    tm = min(1024, m)
    tn = 512
    assert m % tm == 0 and n % tn == 0
    blocks = tm * d * 2 + d * tn * 2 + tm * tn * (2 + 4)
    return pl.pallas_call(
        _inproj_kernel,
        grid=(m // tm, n // tn),
        in_specs=[pl.BlockSpec((tm, d), lambda i, j: (i, 0)),
                  pl.BlockSpec((d, tn), lambda i, j: (0, j)),
                  pl.BlockSpec((1, tn), lambda i, j: (0, j))],
        out_specs=pl.BlockSpec((tm, tn), lambda i, j: (i, j)),
        out_shape=jax.ShapeDtypeStruct((m, n), BF16),
        compiler_params=_compiler_params(("parallel", "arbitrary"), blocks),
        name="inproj",
    )(hn, w, col_scale)


def _sb_attn_kernel(q_ref, k_ref, v_ref, g_ref, o_ref, *, tq):
    i = pl.program_id(2)
    q = q_ref[...]
    row = lax.broadcasted_iota(jnp.int32, (tq, tq), 0)
    col = lax.broadcasted_iota(jnp.int32, (tq, tq), 1)
    suffix_ones = (row >= col).astype(BF16)
    strictly_causal = col < row
    last_lane = lax.broadcasted_iota(jnp.int32, (tq, LANES), 1) == LANES - 1

    def tile(kb, carry, acc, diagonal):
        start = pl.multiple_of(kb * tq, tq)
        k = k_ref[pl.ds(start, tq), :]
        v = v_ref[pl.ds(start, tq), :]
        z = lax.dot_general(q, k, (((1,), (1,)), ((), ())), preferred_element_type=F32)
        p = jnp.maximum(z, 0.0) + jnp.log1p(jnp.exp(-jnp.abs(z)))
        if diagonal:
            p = jnp.where(strictly_causal, p, 0.0)
        p = jnp.concatenate([p[:, :tq - LANES], p[:, tq - LANES:] + carry], axis=1)
        p_hi = p.astype(BF16)
        p_lo = (p - p_hi.astype(F32)).astype(BF16)
        cs = (jnp.dot(p_hi, suffix_ones, preferred_element_type=F32)
              + jnp.dot(p_lo, suffix_ones, preferred_element_type=F32))
        w = jnp.exp(z - cs)
        if diagonal:
            w = jnp.where(strictly_causal, w, 0.0)
        acc = acc + jnp.dot(w.astype(BF16), v, preferred_element_type=F32)
        carry = jnp.where(last_lane, pltpu.roll(cs[:, :LANES], LANES - 1, 1), 0.0)
        return carry, acc

    carry = jnp.zeros((tq, LANES), F32)
    acc = jnp.zeros((tq, HEAD_DIM), F32)
    carry, acc = tile(i, carry, acc, True)

    def body(n, c):
        return tile(i - 1 - n, c[0], c[1], False)

    carry, acc = lax.fori_loop(0, i, body, (carry, acc))
    o_ref[...] = (acc * _silu(g_ref[...].astype(F32))).astype(o_ref.dtype)


def _sb_attention(proj, q_off, k_off, v_off, g_off, n_heads):
    b, s, _ = proj.shape
    tq = min(256, s)
    assert s % tq == 0 and tq % LANES == 0
    qb, kb, vb, gb = (_col_block(o, HEAD_DIM) for o in (q_off, k_off, v_off, g_off))
    blocks = 3 * tq * HEAD_DIM * 2 + 2 * s * HEAD_DIM * 2 + 12 * tq * tq * 4
    return pl.pallas_call(
        functools.partial(_sb_attn_kernel, tq=tq),
        grid=(b, n_heads, s // tq),
        in_specs=[pl.BlockSpec((None, tq, HEAD_DIM), lambda bi, h, i: (bi, i, qb + h)),
                  pl.BlockSpec((None, s, HEAD_DIM), lambda bi, h, i: (bi, 0, kb + h)),
                  pl.BlockSpec((None, s, HEAD_DIM), lambda bi, h, i: (bi, 0, vb + h)),
                  pl.BlockSpec((None, tq, HEAD_DIM), lambda bi, h, i: (bi, i, gb + h))],
        out_specs=pl.BlockSpec((None, tq, HEAD_DIM), lambda bi, h, i: (bi, i, h)),
        out_shape=jax.ShapeDtypeStruct((b, s, n_heads * HEAD_DIM), BF16),
        compiler_params=_compiler_params(("parallel", "parallel", "arbitrary"), blocks),
        name="sb_attention",
    )(proj, proj, proj, proj)


def _swa_kernel(sink_ref, q_ref, kp_ref, kc_ref, vp_ref, vc_ref, g_ref, o_ref, *, group):
    kv = pl.program_id(1)
    n = pl.program_id(2)
    k = jnp.concatenate([kp_ref[...], kc_ref[...]], axis=0)
    v = jnp.concatenate([vp_ref[...], vc_ref[...]], axis=0)
    r = lax.broadcasted_iota(jnp.int32, (WINDOW, 2 * WINDOW), 0)
    c = lax.broadcasted_iota(jnp.int32, (WINDOW, 2 * WINDOW), 1)
    valid = (c > r) & (c <= r + WINDOW) & ((c >= WINDOW) | (n > 0))
    outs = []
    for g in range(group):
        qg = q_ref[:, g * HEAD_DIM:(g + 1) * HEAD_DIM]
        s = lax.dot_general(qg, k, (((1,), (1,)), ((), ())), preferred_element_type=F32)
        s = jnp.where(valid, s, MASKED_SCORE)
        sink = sink_ref[kv * group + g]
        m = jnp.maximum(jnp.max(s, axis=-1, keepdims=True), sink)
        p = jnp.exp(s - m)
        denom = jnp.sum(p, axis=-1, keepdims=True) + jnp.exp(sink - m)
        outs.append(jnp.dot(p.astype(BF16), v, preferred_element_type=F32) / denom)
    o = jnp.concatenate(outs, axis=1)
    o_ref[...] = (o * _silu(g_ref[...].astype(F32))).astype(o_ref.dtype)


def _swa_attention(proj, sinks, q_off, k_off, v_off, g_off, n_heads, n_kv):
    b, s, _ = proj.shape
    group = n_heads // n_kv
    gw = group * HEAD_DIM
    nblk = s // WINDOW
    qb, gb = _col_block(q_off, gw), _col_block(g_off, gw)
    kb, vb = _col_block(k_off, HEAD_DIM), _col_block(v_off, HEAD_DIM)
    blocks = 3 * WINDOW * gw * 2 + 4 * WINDOW * HEAD_DIM * 2 + 8 * group * WINDOW * 2 * WINDOW * 4
    return pl.pallas_call(
        functools.partial(_swa_kernel, group=group),
        grid=(b, n_kv, nblk),
        in_specs=[pl.BlockSpec(memory_space=pltpu.SMEM),
                  pl.BlockSpec((None, WINDOW, gw), lambda bi, kv, n: (bi, n, qb + kv)),
                  pl.BlockSpec((None, WINDOW, HEAD_DIM), lambda bi, kv, n: (bi, jnp.maximum(n - 1, 0), kb + kv)),
                  pl.BlockSpec((None, WINDOW, HEAD_DIM), lambda bi, kv, n: (bi, n, kb + kv)),
                  pl.BlockSpec((None, WINDOW, HEAD_DIM), lambda bi, kv, n: (bi, jnp.maximum(n - 1, 0), vb + kv)),
                  pl.BlockSpec((None, WINDOW, HEAD_DIM), lambda bi, kv, n: (bi, n, vb + kv)),
                  pl.BlockSpec((None, WINDOW, gw), lambda bi, kv, n: (bi, n, gb + kv))],
        out_specs=pl.BlockSpec((None, WINDOW, gw), lambda bi, kv, n: (bi, n, kv)),
        out_shape=jax.ShapeDtypeStruct((b, s, n_heads * HEAD_DIM), BF16),
        compiler_params=_compiler_params(("parallel", "parallel", "arbitrary"), blocks),
        name="swa_attention",
    )(sinks, proj, proj, proj, proj, proj, proj)


def _conv_kernel(x_ref, xp_ref, c_ref, cp_ref, b_ref, g_ref, w_ref, o_ref):
    i = pl.program_id(1)
    h = c_ref[...].astype(F32) * x_ref[...].astype(F32)
    hp = cp_ref[...].astype(F32) * xp_ref[...].astype(F32)
    hp = jnp.where(i > 0, hp, 0.0)
    ext = jnp.concatenate([hp, h], axis=0)
    h1 = pltpu.roll(ext, 1, 0)[SUBLANES:]
    h2 = pltpu.roll(ext, 2, 0)[SUBLANES:]
    w = w_ref[...]
    y = w[0:1] * h2 + w[1:2] * h1 + w[2:3] * h
    y = b_ref[...].astype(F32) * y
    o_ref[...] = (y * _silu(g_ref[...].astype(F32))).astype(o_ref.dtype)


def _short_conv(proj, conv_w, x_off, b_off, c_off, g_off, width):
    b, s, _ = proj.shape
    tm = min(512, s)
    assert s % tm == 0 and tm % SUBLANES == 0
    xb, bb, cb, gb = (_col_block(o, width) for o in (x_off, b_off, c_off, g_off))
    rows_per_blk = tm // SUBLANES

    def cur(col):
        return pl.BlockSpec((None, tm, width), lambda bi, i: (bi, i, col))

    def prev(col):
        return pl.BlockSpec((None, SUBLANES, width),
                            lambda bi, i: (bi, jnp.maximum(i * rows_per_blk - 1, 0), col))

    blocks = 5 * tm * width * 2 + 6 * tm * width * 4
    return pl.pallas_call(
        _conv_kernel,
        grid=(b, s // tm),
        in_specs=[cur(xb), prev(xb), cur(cb), prev(cb), cur(bb), cur(gb),
                  pl.BlockSpec((CONV_WIDTH, width), lambda bi, i: (0, 0))],
        out_specs=pl.BlockSpec((None, tm, width), lambda bi, i: (bi, i, 0)),
        out_shape=jax.ShapeDtypeStruct((b, s, width), BF16),
        compiler_params=_compiler_params(("parallel", "arbitrary"), blocks),
        name="short_conv",
    )(proj, proj, proj, proj, proj, proj, conv_w)


def _gmlp_kernel(u_ref, v_ref, g_ref, vg_ref, ws_ref, bias_ref, o_ref, *, n_chunks, n_groups):
    v = v_ref[...].astype(F32)
    ms = jnp.mean(v * v, axis=-1, keepdims=True)
    vn = (v * lax.rsqrt(ms + EPS) * vg_ref[...]).astype(BF16)
    r = lax.broadcasted_iota(jnp.int32, (GMLP_CHUNK, GMLP_CHUNK), 0)
    c = lax.broadcasted_iota(jnp.int32, (GMLP_CHUNK, GMLP_CHUNK), 1)
    causal = c <= r
    gd = GMLP_GROUP_DIM
    mixed = []
    for g in range(n_groups):
        wsg = jnp.where(causal, ws_ref[g], 0.0).astype(BF16)
        rhs = jnp.concatenate(
            [vn[ch * GMLP_CHUNK:(ch + 1) * GMLP_CHUNK, g * gd:(g + 1) * gd] for ch in range(n_chunks)], axis=1)
        mixed.append(jnp.dot(wsg, rhs, preferred_element_type=F32))
    bias = bias_ref[...]
    rows = []
    for ch in range(n_chunks):
        rows.append(jnp.concatenate([mixed[g][:, ch * gd:(ch + 1) * gd] for g in range(n_groups)], axis=1) + bias)
    mix = jnp.concatenate(rows, axis=0)
    y = u_ref[...].astype(F32) * mix
    o_ref[...] = (y * _silu(g_ref[...].astype(F32))).astype(o_ref.dtype)


def _spatial_gating(proj, v_gain, w_s, bias_tw, u_off, v_off, g_off, width):
    b, s, _ = proj.shape
    n_chunks = 2
    tc = n_chunks * GMLP_CHUNK
    assert s % tc == 0
    n_groups = w_s.shape[0]
    ub, vb, gb = (_col_block(o, width) for o in (u_off, v_off, g_off))

    def cur(col):
        return pl.BlockSpec((None, tc, width), lambda bi, i: (bi, i, col))

    blocks = 4 * tc * width * 2 + n_groups * GMLP_CHUNK * GMLP_CHUNK * 4 + 8 * tc * width * 4
    return pl.pallas_call(
        functools.partial(_gmlp_kernel, n_chunks=n_chunks, n_groups=n_groups),
        grid=(b, s // tc),
        in_specs=[cur(ub), cur(vb), cur(gb),
                  pl.BlockSpec((1, width), lambda bi, i: (0, 0)),
                  pl.BlockSpec((n_groups, GMLP_CHUNK, GMLP_CHUNK), lambda bi, i: (0, 0, 0)),
                  pl.BlockSpec((GMLP_CHUNK, width), lambda bi, i: (0, 0))],
        out_specs=pl.BlockSpec((None, tc, width), lambda bi, i: (bi, i, 0)),
        out_shape=jax.ShapeDtypeStruct((b, s, width), BF16),
        compiler_params=_compiler_params(("parallel", "arbitrary"), blocks),
        name="spatial_gating",
    )(proj, proj, proj, v_gain.reshape(1, width), w_s, bias_tw)


def _outproj_kernel(ya_ref, yb_ref, yc_ref, yd_ref, w_ref, x_ref, o_ref, *, width):
    acc = x_ref[...]
    for idx, y_ref in enumerate((ya_ref, yb_ref, yc_ref, yd_ref)):
        acc = acc + jnp.dot(y_ref[...], w_ref[idx * width:(idx + 1) * width, :], preferred_element_type=F32)
    o_ref[...] = acc


def _outproj(ys, w, x):
    m, d = x.shape
    width = ys[0].shape[1]
    tm = min(1024, m)
    tn = min(512, d)
    assert m % tm == 0 and d % tn == 0
    blocks = 4 * tm * width * 2 + 4 * width * tn * 2 + 3 * tm * tn * 4
    y_spec = pl.BlockSpec((tm, width), lambda i, j: (i, 0))
    return pl.pallas_call(
        functools.partial(_outproj_kernel, width=width),
        grid=(m // tm, d // tn),
        in_specs=[y_spec, y_spec, y_spec, y_spec,
                  pl.BlockSpec((4 * width, tn), lambda i, j: (0, j)),
                  pl.BlockSpec((tm, tn), lambda i, j: (i, j))],
        out_specs=pl.BlockSpec((tm, tn), lambda i, j: (i, j)),
        out_shape=jax.ShapeDtypeStruct((m, d), F32),
        compiler_params=_compiler_params(("parallel", "arbitrary"), blocks),
        name="outproj",
    )(*ys, w, x)


def kernel(x, norm_gain, w_in, conv_w, gmlp_v_gain, gmlp_w_s, gmlp_b_s, swa_sinks, w_out, final_gain):
    batch, seq, d_model = x.shape
    depth = w_in.shape[0]
    m = batch * seq
    w_a = w_b = w_c = w_d = d_model // 4
    n_sb_heads = w_a // HEAD_DIM
    n_swa_heads = w_b // HEAD_DIM
    n_swa_kv = max(1, n_swa_heads // 4)
    w_kv = n_swa_kv * HEAD_DIM
    n_groups = gmlp_w_s.shape[1]
    assert n_swa_heads == swa_sinks.shape[1] and w_d == n_groups * GMLP_GROUP_DIM

    names = ("qa", "ka", "va", "ga", "qb", "kb", "vb", "gb", "xc", "bc", "cc", "gc", "ud", "vd", "gd")
    widths = (w_a, w_a, w_a, w_a, w_b, w_kv, w_kv, w_b, w_c, w_c, w_c, w_c, w_d, w_d, w_d)
    src, start = {}, 0
    for name, wd in zip(names, widths):
        src[name] = (start, wd)
        start += wd
    n_proj = start
    assert n_proj == w_in.shape[2]
    order = [nm for nm in names if nm not in ("kb", "vb")] + ["kb", "vb"]
    dst, start = {}, 0
    for name in order:
        dst[name] = start
        start += src[name][1]
    (qa, ka, va, ga, qb, kb, vb, gb, xc, bc, cc, gc, ud, vd, gd) = (dst[nm] for nm in names)

    def permuted_bf16(w):
        return jnp.concatenate([w[:, src[nm][0]:src[nm][0] + src[nm][1]] for nm in order], axis=1).astype(BF16)

    scale = HEAD_DIM ** -0.5
    col = jnp.arange(n_proj)
    is_q = ((col >= qa) & (col < qa + w_a)) | ((col >= qb) & (col < qb + w_b))
    col_scale = jnp.where(is_q, scale, 1.0).astype(F32).reshape(1, n_proj)

    h = x.reshape(m, d_model)
    for l in range(depth):
        hn = _rmsnorm(h, norm_gain[l], BF16)
        proj = _inproj(hn, permuted_bf16(w_in[l]), col_scale).reshape(batch, seq, n_proj)
        y_a = _sb_attention(proj, qa, ka, va, ga, n_sb_heads)
        y_b = _swa_attention(proj, swa_sinks[l], qb, kb, vb, gb, n_swa_heads, n_swa_kv)
        y_c = _short_conv(proj, conv_w[l], xc, bc, cc, gc, w_c)
        bias_tw = jnp.repeat(gmlp_b_s[l].T, GMLP_GROUP_DIM, axis=1)
        y_d = _spatial_gating(proj, gmlp_v_gain[l], gmlp_w_s[l], bias_tw, ud, vd, gd, w_d)
        ys = [y.reshape(m, -1) for y in (y_a, y_b, y_c, y_d)]
        h = _outproj(ys, w_out[l].astype(BF16), h)
    out = _rmsnorm(h, final_gain, F32)
    return out.reshape(batch, seq, d_model)
```

```python
import functools

import jax
import jax.numpy as jnp
from jax import lax
from jax.experimental import pallas as pl
from jax.experimental.pallas import tpu as pltpu

HEAD_DIM = 128
GMLP_GROUP_DIM = 128
GMLP_CHUNK = 128
WINDOW = 128
CONV_WIDTH = 3
EPS = 1e-6
LOG2_E = 1.4426950408889634

LANES = 128
SUBLANES = 8
V7X_VMEM_BYTES = 64 * 1024 * 1024
VMEM_SPILL_ALLOWANCE = 6 * 1024 * 1024
MASKED_SCORE = -1e30
SB_HEADS_PER_STEP = 4

F32 = jnp.float32
BF16 = jnp.bfloat16


def _compiler_params(semantics, block_bytes):
    limit = min(2 * block_bytes + VMEM_SPILL_ALLOWANCE, V7X_VMEM_BYTES - 4 * 1024 * 1024)
    return pltpu.CompilerParams(dimension_semantics=semantics, vmem_limit_bytes=int(limit))


def _col_block(offset, width):
    assert offset % width == 0, (offset, width)
    return offset // width


def _silu(g):
    return g * jax.nn.sigmoid(g)


def _rmsnorm_kernel(x_ref, g_ref, o_ref):
    x = x_ref[...]
    ms = jnp.mean(x * x, axis=-1, keepdims=True)
    o_ref[...] = (x * lax.rsqrt(ms + EPS) * g_ref[...]).astype(o_ref.dtype)


def _rmsnorm(x, gain, out_dtype):
    m, d = x.shape
    tm = min(256, m)
    return pl.pallas_call(
        _rmsnorm_kernel,
        grid=(m // tm,),
        in_specs=[pl.BlockSpec((tm, d), lambda i: (i, 0)),
                  pl.BlockSpec((1, d), lambda i: (0, 0))],
        out_specs=pl.BlockSpec((tm, d), lambda i: (i, 0)),
        out_shape=jax.ShapeDtypeStruct((m, d), out_dtype),
        compiler_params=_compiler_params(("parallel",), tm * d * 12),
        name="rmsnorm",
    )(x, gain.reshape(1, d))


def _inproj_kernel(h_ref, w_ref, s_ref, o_ref):
    acc = jnp.dot(h_ref[...], w_ref[...], preferred_element_type=F32)
    o_ref[...] = (acc * s_ref[...]).astype(o_ref.dtype)


def _inproj(hn, w, col_scale):
    m, d = hn.shape
    n = w.shape[1]
    tm = min(1024, m)
    tn = 512
    assert m % tm == 0 and n % tn == 0
    blocks = tm * d * 2 + d * tn * 2 + tm * tn * (2 + 4)
    return pl.pallas_call(
        _inproj_kernel,
        grid=(m // tm, n // tn),
        in_specs=[pl.BlockSpec((tm, d), lambda i, j: (i, 0)),
                  pl.BlockSpec((d, tn), lambda i, j: (0, j)),
                  pl.BlockSpec((1, tn), lambda i, j: (0, j))],
        out_specs=pl.BlockSpec((tm, tn), lambda i, j: (i, j)),
        out_shape=jax.ShapeDtypeStruct((m, n), BF16),
        compiler_params=_compiler_params(("parallel", "arbitrary"), blocks),
        name="inproj",
    )(hn, w, col_scale)


def _sb_attn_kernel(q_ref, k_ref, v_ref, g_ref, o_ref, *, tq, heads):
    i = pl.program_id(2)
    dh = HEAD_DIM
    row = lax.broadcasted_iota(jnp.int32, (tq, tq), 0)
    col = lax.broadcasted_iota(jnp.int32, (tq, tq), 1)
    suffix_ones = (row >= col).astype(BF16)
    strictly_causal = col < row
    last_lane = lax.broadcasted_iota(jnp.int32, (tq, LANES), 1) == LANES - 1

    def head_cols(ref, rows, hd):
        return ref[rows, hd * dh:(hd + 1) * dh]

    def sweep(kb, state, diagonal):
        rows = pl.ds(pl.multiple_of(kb * tq, tq), tq)
        hds = range(heads)
        zs = [lax.dot_general(head_cols(q_ref, slice(None), hd), head_cols(k_ref, rows, hd),
                              (((1,), (1,)), ((), ())), preferred_element_type=F32) for hd in hds]
        splits = []
        for hd in hds:
            z = zs[hd]
            p = jnp.maximum(z, 0.0) + jnp.log2(1.0 + jnp.exp2(-jnp.abs(z)))
            if diagonal:
                p = jnp.where(strictly_causal, p, 0.0)
            p = jnp.concatenate([p[:, :tq - LANES], p[:, tq - LANES:] + state[hd][0]], axis=1)
            p_hi = p.astype(BF16)
            splits.append((p_hi, (p - p_hi.astype(F32)).astype(BF16)))
        css = [jnp.dot(p_hi, suffix_ones, preferred_element_type=F32)
               + jnp.dot(p_lo, suffix_ones, preferred_element_type=F32) for p_hi, p_lo in splits]
        ws = []
        for hd in hds:
            w = jnp.exp2(zs[hd] - css[hd])
            if diagonal:
                w = jnp.where(strictly_causal, w, 0.0)
            ws.append(w.astype(BF16))
        out = []
        for hd in hds:
            acc = state[hd][1] + jnp.dot(ws[hd], head_cols(v_ref, rows, hd), preferred_element_type=F32)
            carry = jnp.where(last_lane, pltpu.roll(css[hd][:, :LANES], LANES - 1, 1), 0.0)
            out.append((carry, acc))
        return tuple(out)

    zero = (jnp.zeros((tq, LANES), F32), jnp.zeros((tq, dh), F32))
    state = sweep(i, (zero,) * heads, True)
    state = lax.fori_loop(0, i, lambda n, st: sweep(i - 1 - n, st, False), state)
    acc = jnp.concatenate([st[1] for st in state], axis=1)
    o_ref[...] = (acc * _silu(g_ref[...].astype(F32))).astype(o_ref.dtype)


def _sb_attention(proj, q_off, k_off, v_off, g_off, n_heads):
    b, s, _ = proj.shape
    tq = min(256, s)
    heads = min(SB_HEADS_PER_STEP, n_heads)
    assert s % tq == 0 and tq % LANES == 0 and n_heads % heads == 0
    gw = heads * HEAD_DIM
    qb, kb, vb, gb = (_col_block(o, gw) for o in (q_off, k_off, v_off, g_off))
    blocks = 3 * tq * gw * 2 + 2 * s * gw * 2 + heads * 8 * tq * tq * 4
    return pl.pallas_call(
        functools.partial(_sb_attn_kernel, tq=tq, heads=heads),
        grid=(b, n_heads // heads, s // tq),
        in_specs=[pl.BlockSpec((None, tq, gw), lambda bi, h, i: (bi, i, qb + h)),
                  pl.BlockSpec((None, s, gw), lambda bi, h, i: (bi, 0, kb + h)),
                  pl.BlockSpec((None, s, gw), lambda bi, h, i: (bi, 0, vb + h)),
                  pl.BlockSpec((None, tq, gw), lambda bi, h, i: (bi, i, gb + h))],
        out_specs=pl.BlockSpec((None, tq, gw), lambda bi, h, i: (bi, i, h)),
        out_shape=jax.ShapeDtypeStruct((b, s, n_heads * HEAD_DIM), BF16),
        compiler_params=_compiler_params(("parallel", "parallel", "arbitrary"), blocks),
        name="sb_attention",
    )(proj, proj, proj, proj)


def _swa_kernel(sink_ref, q_ref, kp_ref, kc_ref, vp_ref, vc_ref, g_ref, o_ref, *, group):
    kv = pl.program_id(1)
    n = pl.program_id(2)
    k = jnp.concatenate([kp_ref[...], kc_ref[...]], axis=0)
    v = jnp.concatenate([vp_ref[...], vc_ref[...]], axis=0)
    r = lax.broadcasted_iota(jnp.int32, (WINDOW, 2 * WINDOW), 0)
    c = lax.broadcasted_iota(jnp.int32, (WINDOW, 2 * WINDOW), 1)
    valid = (c > r) & (c <= r + WINDOW) & ((c >= WINDOW) | (n > 0))
    outs = []
    for g in range(group):
        qg = q_ref[:, g * HEAD_DIM:(g + 1) * HEAD_DIM]
        s = lax.dot_general(qg, k, (((1,), (1,)), ((), ())), preferred_element_type=F32)
        s = jnp.where(valid, s, MASKED_SCORE)
        sink = sink_ref[kv * group + g]
        m = jnp.maximum(jnp.max(s, axis=-1, keepdims=True), sink)
        p = jnp.exp(s - m)
        denom = jnp.sum(p, axis=-1, keepdims=True) + jnp.exp(sink - m)
        outs.append(jnp.dot(p.astype(BF16), v, preferred_element_type=F32) / denom)
    o = jnp.concatenate(outs, axis=1)
    o_ref[...] = (o * _silu(g_ref[...].astype(F32))).astype(o_ref.dtype)


def _swa_attention(proj, sinks, q_off, k_off, v_off, g_off, n_heads, n_kv):
    b, s, _ = proj.shape
    group = n_heads // n_kv
    gw = group * HEAD_DIM
    nblk = s // WINDOW
    qb, gb = _col_block(q_off, gw), _col_block(g_off, gw)
    kb, vb = _col_block(k_off, HEAD_DIM), _col_block(v_off, HEAD_DIM)
    blocks = 3 * WINDOW * gw * 2 + 4 * WINDOW * HEAD_DIM * 2 + 8 * group * WINDOW * 2 * WINDOW * 4
    return pl.pallas_call(
        functools.partial(_swa_kernel, group=group),
        grid=(b, n_kv, nblk),
        in_specs=[pl.BlockSpec(memory_space=pltpu.SMEM),
                  pl.BlockSpec((None, WINDOW, gw), lambda bi, kv, n: (bi, n, qb + kv)),
                  pl.BlockSpec((None, WINDOW, HEAD_DIM), lambda bi, kv, n: (bi, jnp.maximum(n - 1, 0), kb + kv)),
                  pl.BlockSpec((None, WINDOW, HEAD_DIM), lambda bi, kv, n: (bi, n, kb + kv)),
                  pl.BlockSpec((None, WINDOW, HEAD_DIM), lambda bi, kv, n: (bi, jnp.maximum(n - 1, 0), vb + kv)),
                  pl.BlockSpec((None, WINDOW, HEAD_DIM), lambda bi, kv, n: (bi, n, vb + kv)),
                  pl.BlockSpec((None, WINDOW, gw), lambda bi, kv, n: (bi, n, gb + kv))],
        out_specs=pl.BlockSpec((None, WINDOW, gw), lambda bi, kv, n: (bi, n, kv)),
        out_shape=jax.ShapeDtypeStruct((b, s, n_heads * HEAD_DIM), BF16),
        compiler_params=_compiler_params(("parallel", "parallel", "arbitrary"), blocks),
        name="swa_attention",
    )(sinks, proj, proj, proj, proj, proj, proj)


def _conv_kernel(x_ref, xp_ref, c_ref, cp_ref, b_ref, g_ref, w_ref, o_ref):
    i = pl.program_id(1)
    h = c_ref[...].astype(F32) * x_ref[...].astype(F32)
    hp = cp_ref[...].astype(F32) * xp_ref[...].astype(F32)
    hp = jnp.where(i > 0, hp, 0.0)
    ext = jnp.concatenate([hp, h], axis=0)
    h1 = pltpu.roll(ext, 1, 0)[SUBLANES:]
    h2 = pltpu.roll(ext, 2, 0)[SUBLANES:]
    w = w_ref[...]
    y = w[0:1] * h2 + w[1:2] * h1 + w[2:3] * h
    y = b_ref[...].astype(F32) * y
    o_ref[...] = (y * _silu(g_ref[...].astype(F32))).astype(o_ref.dtype)


def _short_conv(proj, conv_w, x_off, b_off, c_off, g_off, width):
    b, s, _ = proj.shape
    tm = min(512, s)
    assert s % tm == 0 and tm % SUBLANES == 0
    xb, bb, cb, gb = (_col_block(o, width) for o in (x_off, b_off, c_off, g_off))
    rows_per_blk = tm // SUBLANES

    def cur(col):
        return pl.BlockSpec((None, tm, width), lambda bi, i: (bi, i, col))

    def prev(col):
        return pl.BlockSpec((None, SUBLANES, width),
                            lambda bi, i: (bi, jnp.maximum(i * rows_per_blk - 1, 0), col))

    blocks = 5 * tm * width * 2 + 6 * tm * width * 4
    return pl.pallas_call(
        _conv_kernel,
        grid=(b, s // tm),
        in_specs=[cur(xb), prev(xb), cur(cb), prev(cb), cur(bb), cur(gb),
                  pl.BlockSpec((CONV_WIDTH, width), lambda bi, i: (0, 0))],
        out_specs=pl.BlockSpec((None, tm, width), lambda bi, i: (bi, i, 0)),
        out_shape=jax.ShapeDtypeStruct((b, s, width), BF16),
        compiler_params=_compiler_params(("parallel", "arbitrary"), blocks),
        name="short_conv",
    )(proj, proj, proj, proj, proj, proj, conv_w)


def _gmlp_kernel(u_ref, v_ref, g_ref, vg_ref, ws_ref, bias_ref, o_ref, *, n_chunks, n_groups):
    v = v_ref[...].astype(F32)
    ms = jnp.mean(v * v, axis=-1, keepdims=True)
    vn = (v * lax.rsqrt(ms + EPS) * vg_ref[...]).astype(BF16)
    r = lax.broadcasted_iota(jnp.int32, (GMLP_CHUNK, GMLP_CHUNK), 0)
    c = lax.broadcasted_iota(jnp.int32, (GMLP_CHUNK, GMLP_CHUNK), 1)
    causal = c <= r
    gd = GMLP_GROUP_DIM
    mixed = []
    for g in range(n_groups):
        wsg = jnp.where(causal, ws_ref[g], 0.0).astype(BF16)
        rhs = jnp.concatenate(
            [vn[ch * GMLP_CHUNK:(ch + 1) * GMLP_CHUNK, g * gd:(g + 1) * gd] for ch in range(n_chunks)], axis=1)
        mixed.append(jnp.dot(wsg, rhs, preferred_element_type=F32))
    bias = bias_ref[...]
    rows = []
    for ch in range(n_chunks):
        rows.append(jnp.concatenate([mixed[g][:, ch * gd:(ch + 1) * gd] for g in range(n_groups)], axis=1) + bias)
    mix = jnp.concatenate(rows, axis=0)
    y = u_ref[...].astype(F32) * mix
    o_ref[...] = (y * _silu(g_ref[...].astype(F32))).astype(o_ref.dtype)


def _spatial_gating(proj, v_gain, w_s, bias_tw, u_off, v_off, g_off, width):
    b, s, _ = proj.shape
    n_chunks = 2
    tc = n_chunks * GMLP_CHUNK
    assert s % tc == 0
    n_groups = w_s.shape[0]
    ub, vb, gb = (_col_block(o, width) for o in (u_off, v_off, g_off))

    def cur(col):
        return pl.BlockSpec((None, tc, width), lambda bi, i: (bi, i, col))

    blocks = 4 * tc * width * 2 + n_groups * GMLP_CHUNK * GMLP_CHUNK * 4 + 8 * tc * width * 4
    return pl.pallas_call(
        functools.partial(_gmlp_kernel, n_chunks=n_chunks, n_groups=n_groups),
        grid=(b, s // tc),
        in_specs=[cur(ub), cur(vb), cur(gb),
                  pl.BlockSpec((1, width), lambda bi, i: (0, 0)),
                  pl.BlockSpec((n_groups, GMLP_CHUNK, GMLP_CHUNK), lambda bi, i: (0, 0, 0)),
                  pl.BlockSpec((GMLP_CHUNK, width), lambda bi, i: (0, 0))],
        out_specs=pl.BlockSpec((None, tc, width), lambda bi, i: (bi, i, 0)),
        out_shape=jax.ShapeDtypeStruct((b, s, width), BF16),
        compiler_params=_compiler_params(("parallel", "arbitrary"), blocks),
        name="spatial_gating",
    )(proj, proj, proj, v_gain.reshape(1, width), w_s, bias_tw)


def _outproj_kernel(ya_ref, yb_ref, yc_ref, yd_ref, w_ref, x_ref, o_ref, *, width):
    acc = x_ref[...]
    for idx, y_ref in enumerate((ya_ref, yb_ref, yc_ref, yd_ref)):
        acc = acc + jnp.dot(y_ref[...], w_ref[idx * width:(idx + 1) * width, :], preferred_element_type=F32)
    o_ref[...] = acc


def _outproj(ys, w, x):
    m, d = x.shape
    width = ys[0].shape[1]
    tm = min(1024, m)
    tn = min(512, d)
    assert m % tm == 0 and d % tn == 0
    blocks = 4 * tm * width * 2 + 4 * width * tn * 2 + 3 * tm * tn * 4
    y_spec = pl.BlockSpec((tm, width), lambda i, j: (i, 0))
    return pl.pallas_call(
        functools.partial(_outproj_kernel, width=width),
        grid=(m // tm, d // tn),
        in_specs=[y_spec, y_spec, y_spec, y_spec,
                  pl.BlockSpec((4 * width, tn), lambda i, j: (0, j)),
                  pl.BlockSpec((tm, tn), lambda i, j: (i, j))],
        out_specs=pl.BlockSpec((tm, tn), lambda i, j: (i, j)),
        out_shape=jax.ShapeDtypeStruct((m, d), F32),
        compiler_params=_compiler_params(("parallel", "arbitrary"), blocks),
        name="outproj",
    )(*ys, w, x)


def kernel(x, norm_gain, w_in, conv_w, gmlp_v_gain, gmlp_w_s, gmlp_b_s, swa_sinks, w_out, final_gain):
    batch, seq, d_model = x.shape
    depth = w_in.shape[0]
    m = batch * seq
    w_a = w_b = w_c = w_d = d_model // 4
    n_sb_heads = w_a // HEAD_DIM
    n_swa_heads = w_b // HEAD_DIM
    n_swa_kv = max(1, n_swa_heads // 4)
    w_kv = n_swa_kv * HEAD_DIM
    n_groups = gmlp_w_s.shape[1]
    assert n_swa_heads == swa_sinks.shape[1] and w_d == n_groups * GMLP_GROUP_DIM

    names = ("qa", "ka", "va", "ga", "qb", "kb", "vb", "gb", "xc", "bc", "cc", "gc", "ud", "vd", "gd")
    widths = (w_a, w_a, w_a, w_a, w_b, w_kv, w_kv, w_b, w_c, w_c, w_c, w_c, w_d, w_d, w_d)
    src, start = {}, 0
    for name, wd in zip(names, widths):
        src[name] = (start, wd)
        start += wd
    n_proj = start
    assert n_proj == w_in.shape[2]
    order = [nm for nm in names if nm not in ("kb", "vb")] + ["kb", "vb"]
    dst, start = {}, 0
    for name in order:
        dst[name] = start
        start += src[name][1]
    (qa, ka, va, ga, qb, kb, vb, gb, xc, bc, cc, gc, ud, vd, gd) = (dst[nm] for nm in names)

    def permuted_bf16(w):
        return jnp.concatenate([w[:, src[nm][0]:src[nm][0] + src[nm][1]] for nm in order], axis=1).astype(BF16)

    scale = HEAD_DIM ** -0.5
    col = jnp.arange(n_proj)
    col_scale = jnp.where((col >= qa) & (col < qa + w_a), scale * LOG2_E,
                          jnp.where((col >= qb) & (col < qb + w_b), scale, 1.0))
    col_scale = col_scale.astype(F32).reshape(1, n_proj)

    h = x.reshape(m, d_model)
    for l in range(depth):
        hn = _rmsnorm(h, norm_gain[l], BF16)
        proj = _inproj(hn, permuted_bf16(w_in[l]), col_scale).reshape(batch, seq, n_proj)
        y_a = _sb_attention(proj, qa, ka, va, ga, n_sb_heads)
        y_b = _swa_attention(proj, swa_sinks[l], qb, kb, vb, gb, n_swa_heads, n_swa_kv)
        y_c = _short_conv(proj, conv_w[l], xc, bc, cc, gc, w_c)
        bias_tw = jnp.repeat(gmlp_b_s[l].T, GMLP_GROUP_DIM, axis=1)
        y_d = _spatial_gating(proj, gmlp_v_gain[l], gmlp_w_s[l], bias_tw, ud, vd, gd, w_d)
        ys = [y.reshape(m, -1) for y in (y_a, y_b, y_c, y_d)]
        h = _outproj(ys, w_out[l].astype(BF16), h)
    out = _rmsnorm(h, final_gain, F32)
    return out.reshape(batch, seq, d_model)
```

```python
import functools

import jax
import jax.numpy as jnp
from jax import lax
from jax.experimental import pallas as pl
from jax.experimental.pallas import tpu as pltpu

HEAD_DIM = 128
GMLP_GROUP_DIM = 128
GMLP_CHUNK = 128
WINDOW = 128
CONV_WIDTH = 3
EPS = 1e-6
LOG2_E = 1.4426950408889634

LANES = 128
SUBLANES = 8
V7X_VMEM_BYTES = 64 * 1024 * 1024
VMEM_SPILL_ALLOWANCE = 6 * 1024 * 1024
MASKED_SCORE = -1e30
SB_HEADS_PER_STEP = 4

F32 = jnp.float32
BF16 = jnp.bfloat16


def _compiler_params(semantics, block_bytes):
    limit = min(2 * block_bytes + VMEM_SPILL_ALLOWANCE, V7X_VMEM_BYTES - 4 * 1024 * 1024)
    return pltpu.CompilerParams(dimension_semantics=semantics, vmem_limit_bytes=int(limit))


def _col_block(offset, width):
    assert offset % width == 0, (offset, width)
    return offset // width


def _silu(g):
    return g * jax.nn.sigmoid(g)


def _rmsnorm_kernel(x_ref, g_ref, o_ref):
    x = x_ref[...]
    ms = jnp.mean(x * x, axis=-1, keepdims=True)
    o_ref[...] = (x * lax.rsqrt(ms + EPS) * g_ref[...]).astype(o_ref.dtype)


def _rmsnorm(x, gain, out_dtype):
    m, d = x.shape
    tm = min(256, m)
    return pl.pallas_call(
        _rmsnorm_kernel,
        grid=(m // tm,),
        in_specs=[pl.BlockSpec((tm, d), lambda i: (i, 0)),
                  pl.BlockSpec((1, d), lambda i: (0, 0))],
        out_specs=pl.BlockSpec((tm, d), lambda i: (i, 0)),
        out_shape=jax.ShapeDtypeStruct((m, d), out_dtype),
        compiler_params=_compiler_params(("parallel",), tm * d * 12),
        name="rmsnorm",
    )(x, gain.reshape(1, d))


def _inproj_kernel(h_ref, w_ref, s_ref, o_ref):
    acc = jnp.dot(h_ref[...], w_ref[...], preferred_element_type=F32)
    o_ref[...] = (acc * s_ref[...]).astype(o_ref.dtype)


def _inproj(hn, w, col_scale):
    m, d = hn.shape
    n = w.shape[1]
    tm = min(1024, m)
    tn = 512
    assert m % tm == 0 and n % tn == 0
    blocks = tm * d * 2 + d * tn * 2 + tm * tn * (2 + 4)
    return pl.pallas_call(
        _inproj_kernel,
        grid=(m // tm, n // tn),
        in_specs=[pl.BlockSpec((tm, d), lambda i, j: (i, 0)),
                  pl.BlockSpec((d, tn), lambda i, j: (0, j)),
                  pl.BlockSpec((1, tn), lambda i, j: (0, j))],
        out_specs=pl.BlockSpec((tm, tn), lambda i, j: (i, j)),
        out_shape=jax.ShapeDtypeStruct((m, n), BF16),
        compiler_params=_compiler_params(("parallel", "arbitrary"), blocks),
        name="inproj",
    )(hn, w, col_scale)


def _sb_attn_kernel(q_ref, k_ref, v_ref, g_ref, o_ref, z_scr, w_scr, carry_scr, acc_scr, *, tq, heads):
    i = pl.program_id(2)
    dh = HEAD_DIM
    hds = range(heads)
    row = lax.broadcasted_iota(jnp.int32, (tq, tq), 0)
    col = lax.broadcasted_iota(jnp.int32, (tq, tq), 1)
    ones = (row >= col).astype(BF16)
    suffix_ones = jnp.concatenate([ones, ones], axis=0)
    strictly_causal = col < row
    last_lane = lax.broadcasted_iota(jnp.int32, (tq, LANES), 1) == LANES - 1

    def key_rows(kb):
        return pl.ds(pl.multiple_of(kb * tq, tq), tq)

    def head_cols(hd):
        return slice(hd * dh, (hd + 1) * dh)

    def scores(kb, hd):
        return lax.dot_general(q_ref[:, head_cols(hd)], k_ref[key_rows(kb), head_cols(hd)],
                               (((1,), (1,)), ((), ())), preferred_element_type=F32)

    def split_softplus(z, carry, diagonal):
        p = jnp.maximum(z, 0.0) + jnp.log2(1.0 + jnp.exp2(-jnp.abs(z)))
        if diagonal:
            p = jnp.where(strictly_causal, p, 0.0)
        p = jnp.concatenate([p[:, :tq - LANES], p[:, tq - LANES:] + carry], axis=1)
        p_hi = p.astype(BF16)
        p_lo = (p - p_hi.astype(F32)).astype(BF16)
        return jnp.concatenate([p_hi, p_lo], axis=1)

    def finish(z, cs, hd, diagonal):
        w = jnp.exp2(z - cs)
        if diagonal:
            w = jnp.where(strictly_causal, w, 0.0)
        w_scr[hd] = w.astype(BF16)
        carry_scr[hd] = jnp.where(last_lane, pltpu.roll(cs[:, :LANES], LANES - 1, 1), 0.0)

    def apply_weights(kb):
        for hd in hds:
            acc_scr[:, head_cols(hd)] += jnp.dot(w_scr[hd], v_ref[key_rows(kb), head_cols(hd)],
                                                 preferred_element_type=F32)

    acc_scr[...] = jnp.zeros_like(acc_scr)
    zs = [scores(i, hd) for hd in hds]
    zero_carry = jnp.zeros((tq, LANES), F32)
    splits = [split_softplus(zs[hd], zero_carry, True) for hd in hds]
    nxt = jnp.maximum(i - 1, 0)
    for hd in hds:
        cs = jnp.dot(splits[hd], suffix_ones, preferred_element_type=F32)
        finish(zs[hd], cs, hd, True)
        z_scr[0, hd] = scores(nxt, hd)

    def step(kb, z_in, z_out):
        apply_weights(kb + 1)
        nxt = jnp.maximum(kb - 1, 0)
        for hd in hds:
            z_scr[z_out, hd] = scores(nxt, hd)
        splits = [split_softplus(z_scr[z_in, hd], carry_scr[hd], False) for hd in hds]
        for hd in hds:
            cs = jnp.dot(splits[hd], suffix_ones, preferred_element_type=F32)
            finish(z_scr[z_in, hd], cs, hd, False)

    def two_steps(m, _):
        kb = i - 1 - 2 * m
        step(kb, 0, 1)
        step(kb - 1, 1, 0)
        return 0

    lax.fori_loop(0, i // 2, two_steps, 0)

    @pl.when(i % 2 == 1)
    def _():
        step(0, 0, 1)

    apply_weights(0)
    o_ref[...] = (acc_scr[...] * _silu(g_ref[...].astype(F32))).astype(o_ref.dtype)


def _sb_attention(proj, q_off, k_off, v_off, g_off, n_heads):
    b, s, _ = proj.shape
    tq = min(256, s)
    heads = min(SB_HEADS_PER_STEP, n_heads)
    assert s % tq == 0 and tq % LANES == 0 and n_heads % heads == 0
    gw = heads * HEAD_DIM
    qb, kb, vb, gb = (_col_block(o, gw) for o in (q_off, k_off, v_off, g_off))
    blocks = 3 * tq * gw * 2 + 2 * s * gw * 2 + heads * 8 * tq * tq * 4
    return pl.pallas_call(
        functools.partial(_sb_attn_kernel, tq=tq, heads=heads),
        grid=(b, n_heads // heads, s // tq),
        in_specs=[pl.BlockSpec((None, tq, gw), lambda bi, h, i: (bi, i, qb + h)),
                  pl.BlockSpec((None, s, gw), lambda bi, h, i: (bi, 0, kb + h)),
                  pl.BlockSpec((None, s, gw), lambda bi, h, i: (bi, 0, vb + h)),
                  pl.BlockSpec((None, tq, gw), lambda bi, h, i: (bi, i, gb + h))],
        out_specs=pl.BlockSpec((None, tq, gw), lambda bi, h, i: (bi, i, h)),
        out_shape=jax.ShapeDtypeStruct((b, s, n_heads * HEAD_DIM), BF16),
        scratch_shapes=[pltpu.VMEM((2, heads, tq, tq), F32),
                        pltpu.VMEM((heads, tq, tq), BF16),
                        pltpu.VMEM((heads, tq, LANES), F32),
                        pltpu.VMEM((tq, gw), F32)],
        compiler_params=_compiler_params(("parallel", "parallel", "arbitrary"), blocks),
        name="sb_attention",
    )(proj, proj, proj, proj)


def _swa_kernel(sink_ref, q_ref, kp_ref, kc_ref, vp_ref, vc_ref, g_ref, o_ref, *, group):
    kv = pl.program_id(1)
    n = pl.program_id(2)
    k = jnp.concatenate([kp_ref[...], kc_ref[...]], axis=0)
    v = jnp.concatenate([vp_ref[...], vc_ref[...]], axis=0)
    r = lax.broadcasted_iota(jnp.int32, (WINDOW, 2 * WINDOW), 0)
    c = lax.broadcasted_iota(jnp.int32, (WINDOW, 2 * WINDOW), 1)
    valid = (c > r) & (c <= r + WINDOW) & ((c >= WINDOW) | (n > 0))
    outs = []
    for g in range(group):
        qg = q_ref[:, g * HEAD_DIM:(g + 1) * HEAD_DIM]
        s = lax.dot_general(qg, k, (((1,), (1,)), ((), ())), preferred_element_type=F32)
        s = jnp.where(valid, s, MASKED_SCORE)
        sink = sink_ref[kv * group + g]
        m = jnp.maximum(jnp.max(s, axis=-1, keepdims=True), sink)
        p = jnp.exp(s - m)
        denom = jnp.sum(p, axis=-1, keepdims=True) + jnp.exp(sink - m)
        outs.append(jnp.dot(p.astype(BF16), v, preferred_element_type=F32) / denom)
    o = jnp.concatenate(outs, axis=1)
    o_ref[...] = (o * _silu(g_ref[...].astype(F32))).astype(o_ref.dtype)


def _swa_attention(proj, sinks, q_off, k_off, v_off, g_off, n_heads, n_kv):
    b, s, _ = proj.shape
    group = n_heads // n_kv
    gw = group * HEAD_DIM
    nblk = s // WINDOW
    qb, gb = _col_block(q_off, gw), _col_block(g_off, gw)
    kb, vb = _col_block(k_off, HEAD_DIM), _col_block(v_off, HEAD_DIM)
    blocks = 3 * WINDOW * gw * 2 + 4 * WINDOW * HEAD_DIM * 2 + 8 * group * WINDOW * 2 * WINDOW * 4
    return pl.pallas_call(
        functools.partial(_swa_kernel, group=group),
        grid=(b, n_kv, nblk),
        in_specs=[pl.BlockSpec(memory_space=pltpu.SMEM),
                  pl.BlockSpec((None, WINDOW, gw), lambda bi, kv, n: (bi, n, qb + kv)),
                  pl.BlockSpec((None, WINDOW, HEAD_DIM), lambda bi, kv, n: (bi, jnp.maximum(n - 1, 0), kb + kv)),
                  pl.BlockSpec((None, WINDOW, HEAD_DIM), lambda bi, kv, n: (bi, n, kb + kv)),
                  pl.BlockSpec((None, WINDOW, HEAD_DIM), lambda bi, kv, n: (bi, jnp.maximum(n - 1, 0), vb + kv)),
                  pl.BlockSpec((None, WINDOW, HEAD_DIM), lambda bi, kv, n: (bi, n, vb + kv)),
                  pl.BlockSpec((None, WINDOW, gw), lambda bi, kv, n: (bi, n, gb + kv))],
        out_specs=pl.BlockSpec((None, WINDOW, gw), lambda bi, kv, n: (bi, n, kv)),
        out_shape=jax.ShapeDtypeStruct((b, s, n_heads * HEAD_DIM), BF16),
        compiler_params=_compiler_params(("parallel", "parallel", "arbitrary"), blocks),
        name="swa_attention",
    )(sinks, proj, proj, proj, proj, proj, proj)


def _conv_kernel(x_ref, xp_ref, c_ref, cp_ref, b_ref, g_ref, w_ref, o_ref):
    i = pl.program_id(1)
    h = c_ref[...].astype(F32) * x_ref[...].astype(F32)
    hp = cp_ref[...].astype(F32) * xp_ref[...].astype(F32)
    hp = jnp.where(i > 0, hp, 0.0)
    ext = jnp.concatenate([hp, h], axis=0)
    h1 = pltpu.roll(ext, 1, 0)[SUBLANES:]
    h2 = pltpu.roll(ext, 2, 0)[SUBLANES:]
    w = w_ref[...]
    y = w[0:1] * h2 + w[1:2] * h1 + w[2:3] * h
    y = b_ref[...].astype(F32) * y
    o_ref[...] = (y * _silu(g_ref[...].astype(F32))).astype(o_ref.dtype)


def _short_conv(proj, conv_w, x_off, b_off, c_off, g_off, width):
    b, s, _ = proj.shape
    tm = min(512, s)
    assert s % tm == 0 and tm % SUBLANES == 0
    xb, bb, cb, gb = (_col_block(o, width) for o in (x_off, b_off, c_off, g_off))
    rows_per_blk = tm // SUBLANES

    def cur(col):
        return pl.BlockSpec((None, tm, width), lambda bi, i: (bi, i, col))

    def prev(col):
        return pl.BlockSpec((None, SUBLANES, width),
                            lambda bi, i: (bi, jnp.maximum(i * rows_per_blk - 1, 0), col))

    blocks = 5 * tm * width * 2 + 6 * tm * width * 4
    return pl.pallas_call(
        _conv_kernel,
        grid=(b, s // tm),
        in_specs=[cur(xb), prev(xb), cur(cb), prev(cb), cur(bb), cur(gb),
                  pl.BlockSpec((CONV_WIDTH, width), lambda bi, i: (0, 0))],
        out_specs=pl.BlockSpec((None, tm, width), lambda bi, i: (bi, i, 0)),
        out_shape=jax.ShapeDtypeStruct((b, s, width), BF16),
        compiler_params=_compiler_params(("parallel", "arbitrary"), blocks),
        name="short_conv",
    )(proj, proj, proj, proj, proj, proj, conv_w)


def _gmlp_kernel(u_ref, v_ref, g_ref, vg_ref, ws_ref, bias_ref, o_ref, *, n_chunks, n_groups):
    v = v_ref[...].astype(F32)
    ms = jnp.mean(v * v, axis=-1, keepdims=True)
    vn = (v * lax.rsqrt(ms + EPS) * vg_ref[...]).astype(BF16)
    r = lax.broadcasted_iota(jnp.int32, (GMLP_CHUNK, GMLP_CHUNK), 0)
    c = lax.broadcasted_iota(jnp.int32, (GMLP_CHUNK, GMLP_CHUNK), 1)
    causal = c <= r
    gd = GMLP_GROUP_DIM
    mixed = []
    for g in range(n_groups):
        wsg = jnp.where(causal, ws_ref[g], 0.0).astype(BF16)
        rhs = jnp.concatenate(
            [vn[ch * GMLP_CHUNK:(ch + 1) * GMLP_CHUNK, g * gd:(g + 1) * gd] for ch in range(n_chunks)], axis=1)
        mixed.append(jnp.dot(wsg, rhs, preferred_element_type=F32))
    bias = bias_ref[...]
    rows = []
    for ch in range(n_chunks):
        rows.append(jnp.concatenate([mixed[g][:, ch * gd:(ch + 1) * gd] for g in range(n_groups)], axis=1) + bias)
    mix = jnp.concatenate(rows, axis=0)
    y = u_ref[...].astype(F32) * mix
    o_ref[...] = (y * _silu(g_ref[...].astype(F32))).astype(o_ref.dtype)


def _spatial_gating(proj, v_gain, w_s, bias_tw, u_off, v_off, g_off, width):
    b, s, _ = proj.shape
    n_chunks = 2
    tc = n_chunks * GMLP_CHUNK
    assert s % tc == 0
    n_groups = w_s.shape[0]
    ub, vb, gb = (_col_block(o, width) for o in (u_off, v_off, g_off))

    def cur(col):
        return pl.BlockSpec((None, tc, width), lambda bi, i: (bi, i, col))

    blocks = 4 * tc * width * 2 + n_groups * GMLP_CHUNK * GMLP_CHUNK * 4 + 8 * tc * width * 4
    return pl.pallas_call(
        functools.partial(_gmlp_kernel, n_chunks=n_chunks, n_groups=n_groups),
        grid=(b, s // tc),
        in_specs=[cur(ub), cur(vb), cur(gb),
                  pl.BlockSpec((1, width), lambda bi, i: (0, 0)),
                  pl.BlockSpec((n_groups, GMLP_CHUNK, GMLP_CHUNK), lambda bi, i: (0, 0, 0)),
                  pl.BlockSpec((GMLP_CHUNK, width), lambda bi, i: (0, 0))],
        out_specs=pl.BlockSpec((None, tc, width), lambda bi, i: (bi, i, 0)),
        out_shape=jax.ShapeDtypeStruct((b, s, width), BF16),
        compiler_params=_compiler_params(("parallel", "arbitrary"), blocks),
        name="spatial_gating",
    )(proj, proj, proj, v_gain.reshape(1, width), w_s, bias_tw)


def _outproj_kernel(ya_ref, yb_ref, yc_ref, yd_ref, w_ref, x_ref, o_ref, *, width):
    acc = x_ref[...]
    for idx, y_ref in enumerate((ya_ref, yb_ref, yc_ref, yd_ref)):
        acc = acc + jnp.dot(y_ref[...], w_ref[idx * width:(idx + 1) * width, :], preferred_element_type=F32)
    o_ref[...] = acc


def _outproj(ys, w, x):
    m, d = x.shape
    width = ys[0].shape[1]
    tm = min(1024, m)
    tn = min(512, d)
    assert m % tm == 0 and d % tn == 0
    blocks = 4 * tm * width * 2 + 4 * width * tn * 2 + 3 * tm * tn * 4
    y_spec = pl.BlockSpec((tm, width), lambda i, j: (i, 0))
    return pl.pallas_call(
        functools.partial(_outproj_kernel, width=width),
        grid=(m // tm, d // tn),
        in_specs=[y_spec, y_spec, y_spec, y_spec,
                  pl.BlockSpec((4 * width, tn), lambda i, j: (0, j)),
                  pl.BlockSpec((tm, tn), lambda i, j: (i, j))],
        out_specs=pl.BlockSpec((tm, tn), lambda i, j: (i, j)),
        out_shape=jax.ShapeDtypeStruct((m, d), F32),
        compiler_params=_compiler_params(("parallel", "arbitrary"), blocks),
        name="outproj",
    )(*ys, w, x)


def kernel(x, norm_gain, w_in, conv_w, gmlp_v_gain, gmlp_w_s, gmlp_b_s, swa_sinks, w_out, final_gain):
    batch, seq, d_model = x.shape
    depth = w_in.shape[0]
    m = batch * seq
    w_a = w_b = w_c = w_d = d_model // 4
    n_sb_heads = w_a // HEAD_DIM
    n_swa_heads = w_b // HEAD_DIM
    n_swa_kv = max(1, n_swa_heads // 4)
    w_kv = n_swa_kv * HEAD_DIM
    n_groups = gmlp_w_s.shape[1]
    assert n_swa_heads == swa_sinks.shape[1] and w_d == n_groups * GMLP_GROUP_DIM

    names = ("qa", "ka", "va", "ga", "qb", "kb", "vb", "gb", "xc", "bc", "cc", "gc", "ud", "vd", "gd")
    widths = (w_a, w_a, w_a, w_a, w_b, w_kv, w_kv, w_b, w_c, w_c, w_c, w_c, w_d, w_d, w_d)
    src, start = {}, 0
    for name, wd in zip(names, widths):
        src[name] = (start, wd)
        start += wd
    n_proj = start
    assert n_proj == w_in.shape[2]
    order = [nm for nm in names if nm not in ("kb", "vb")] + ["kb", "vb"]
    dst, start = {}, 0
    for name in order:
        dst[name] = start
        start += src[name][1]
    (qa, ka, va, ga, qb, kb, vb, gb, xc, bc, cc, gc, ud, vd, gd) = (dst[nm] for nm in names)

    def permuted_bf16(w):
        return jnp.concatenate([w[:, src[nm][0]:src[nm][0] + src[nm][1]] for nm in order], axis=1).astype(BF16)

    scale = HEAD_DIM ** -0.5
    col = jnp.arange(n_proj)
    col_scale = jnp.where((col >= qa) & (col < qa + w_a), scale * LOG2_E,
                          jnp.where((col >= qb) & (col < qb + w_b), scale, 1.0))
    col_scale = col_scale.astype(F32).reshape(1, n_proj)

    h = x.reshape(m, d_model)
    for l in range(depth):
        hn = _rmsnorm(h, norm_gain[l], BF16)
        proj = _inproj(hn, permuted_bf16(w_in[l]), col_scale).reshape(batch, seq, n_proj)
        y_a = _sb_attention(proj, qa, ka, va, ga, n_sb_heads)
        y_b = _swa_attention(proj, swa_sinks[l], qb, kb, vb, gb, n_swa_heads, n_swa_kv)
        y_c = _short_conv(proj, conv_w[l], xc, bc, cc, gc, w_c)
        bias_tw = jnp.repeat(gmlp_b_s[l].T, GMLP_GROUP_DIM, axis=1)
        y_d = _spatial_gating(proj, gmlp_v_gain[l], gmlp_w_s[l], bias_tw, ud, vd, gd, w_d)
        ys = [y.reshape(m, -1) for y in (y_a, y_b, y_c, y_d)]
        h = _outproj(ys, w_out[l].astype(BF16), h)
    out = _rmsnorm(h, final_gain, F32)
    return out.reshape(batch, seq, d_model)
```

```python
import functools

import jax
import jax.numpy as jnp
from jax import lax
from jax.experimental import pallas as pl
from jax.experimental.pallas import tpu as pltpu

HEAD_DIM = 128
GMLP_GROUP_DIM = 128
GMLP_CHUNK = 128
WINDOW = 128
CONV_WIDTH = 3
EPS = 1e-6
LOG2_E = 1.4426950408889634
EXP2_CLAMP = 64.0

LANES = 128
SUBLANES = 8
V7X_VMEM_BYTES = 64 * 1024 * 1024
VMEM_SPILL_ALLOWANCE = 6 * 1024 * 1024
MASKED_SCORE = -1e30
SB_HEADS_PER_STEP = 4

F32 = jnp.float32
BF16 = jnp.bfloat16


def _compiler_params(semantics, block_bytes):
    limit = min(2 * block_bytes + VMEM_SPILL_ALLOWANCE, V7X_VMEM_BYTES - 4 * 1024 * 1024)
    return pltpu.CompilerParams(dimension_semantics=semantics, vmem_limit_bytes=int(limit))


def _col_block(offset, width):
    assert offset % width == 0, (offset, width)
    return offset // width


def _silu(g):
    return g * jax.nn.sigmoid(g)


def _rmsnorm_kernel(x_ref, g_ref, o_ref):
    x = x_ref[...]
    ms = jnp.mean(x * x, axis=-1, keepdims=True)
    o_ref[...] = (x * lax.rsqrt(ms + EPS) * g_ref[...]).astype(o_ref.dtype)


def _rmsnorm(x, gain, out_dtype):
    m, d = x.shape
    tm = min(256, m)
    return pl.pallas_call(
        _rmsnorm_kernel,
        grid=(m // tm,),
        in_specs=[pl.BlockSpec((tm, d), lambda i: (i, 0)),
                  pl.BlockSpec((1, d), lambda i: (0, 0))],
        out_specs=pl.BlockSpec((tm, d), lambda i: (i, 0)),
        out_shape=jax.ShapeDtypeStruct((m, d), out_dtype),
        compiler_params=_compiler_params(("parallel",), tm * d * 12),
        name="rmsnorm",
    )(x, gain.reshape(1, d))


def _inproj_kernel(h_ref, w_ref, s_ref, o_ref):
    acc = jnp.dot(h_ref[...], w_ref[...], preferred_element_type=F32)
    o_ref[...] = (acc * s_ref[...]).astype(o_ref.dtype)


def _inproj(hn, w, col_scale):
    m, d = hn.shape
    n = w.shape[1]
    tm = min(1024, m)
    tn = 512
    assert m % tm == 0 and n % tn == 0
    blocks = tm * d * 2 + d * tn * 2 + tm * tn * (2 + 4)
    return pl.pallas_call(
        _inproj_kernel,
        grid=(m // tm, n // tn),
        in_specs=[pl.BlockSpec((tm, d), lambda i, j: (i, 0)),
                  pl.BlockSpec((d, tn), lambda i, j: (0, j)),
                  pl.BlockSpec((1, tn), lambda i, j: (0, j))],
        out_specs=pl.BlockSpec((tm, tn), lambda i, j: (i, j)),
        out_shape=jax.ShapeDtypeStruct((m, n), BF16),
        compiler_params=_compiler_params(("parallel", "arbitrary"), blocks),
        name="inproj",
    )(hn, w, col_scale)


def _sb_attn_kernel(q_ref, k_ref, v_ref, g_ref, o_ref, z_scr, w_scr, carry_scr, acc_scr, *, tq, heads):
    i = pl.program_id(2)
    dh = HEAD_DIM
    hds = range(heads)
    row = lax.broadcasted_iota(jnp.int32, (tq, tq), 0)
    col = lax.broadcasted_iota(jnp.int32, (tq, tq), 1)
    strictly_causal = col < row
    block_total = lax.broadcasted_iota(jnp.int32, (tq, LANES), 1) == LANES - 1
    later_keys = jnp.concatenate([row > col, block_total], axis=1).astype(BF16)

    def key_rows(kb):
        return pl.ds(pl.multiple_of(kb * tq, tq), tq)

    def head_cols(hd):
        return slice(hd * dh, (hd + 1) * dh)

    def scores(kb, hd):
        return lax.dot_general(q_ref[:, head_cols(hd)], k_ref[key_rows(kb), head_cols(hd)],
                               (((1,), (1,)), ((), ())), preferred_element_type=F32)

    def softplus_terms(z, carry, diagonal):
        p = jnp.maximum(jnp.log2(1.0 + jnp.exp2(jnp.minimum(z, EXP2_CLAMP))), z)
        if diagonal:
            p = jnp.where(strictly_causal, p, 0.0)
        p = jnp.concatenate([p[:, :tq - LANES], p[:, tq - LANES:] + carry], axis=1)
        return p.astype(BF16), z - p

    def finish(u, cs, hd, diagonal):
        w = jnp.exp2(u - cs[:, :tq])
        if diagonal:
            w = jnp.where(strictly_causal, w, 0.0)
        w_scr[hd] = w.astype(BF16)
        carry_scr[hd] = cs[:, tq:]

    def apply_weights(kb):
        for hd in hds:
            acc_scr[:, head_cols(hd)] += jnp.dot(w_scr[hd], v_ref[key_rows(kb), head_cols(hd)],
                                                 preferred_element_type=F32)

    acc_scr[...] = jnp.zeros_like(acc_scr)
    zs = [scores(i, hd) for hd in hds]
    zero_carry = jnp.zeros((tq, LANES), F32)
    terms = [softplus_terms(zs[hd], zero_carry, True) for hd in hds]
    nxt = jnp.maximum(i - 1, 0)
    for hd in hds:
        p_bf16, u = terms[hd]
        cs = jnp.dot(p_bf16, later_keys, preferred_element_type=F32)
        finish(u, cs, hd, True)
        z_scr[0, hd] = scores(nxt, hd)

    def step(kb, z_in, z_out):
        apply_weights(kb + 1)
        nxt = jnp.maximum(kb - 1, 0)
        for hd in hds:
            z_scr[z_out, hd] = scores(nxt, hd)
        terms = []
        for hd in hds:
            p_bf16, u = softplus_terms(z_scr[z_in, hd], carry_scr[hd], False)
            z_scr[z_in, hd] = u
            terms.append(p_bf16)
        for hd in hds:
            cs = jnp.dot(terms[hd], later_keys, preferred_element_type=F32)
            finish(z_scr[z_in, hd], cs, hd, False)

    def two_steps(m, _):
        kb = i - 1 - 2 * m
        step(kb, 0, 1)
        step(kb - 1, 1, 0)
        return 0

    lax.fori_loop(0, i // 2, two_steps, 0)

    @pl.when(i % 2 == 1)
    def _():
        step(0, 0, 1)

    apply_weights(0)
    o_ref[...] = (acc_scr[...] * _silu(g_ref[...].astype(F32))).astype(o_ref.dtype)


def _sb_attention(proj, q_off, k_off, v_off, g_off, n_heads):
    b, s, _ = proj.shape
    tq = min(256, s)
    heads = min(SB_HEADS_PER_STEP, n_heads)
    assert s % tq == 0 and tq % LANES == 0 and n_heads % heads == 0
    gw = heads * HEAD_DIM
    qb, kb, vb, gb = (_col_block(o, gw) for o in (q_off, k_off, v_off, g_off))
    blocks = 3 * tq * gw * 2 + 2 * s * gw * 2 + heads * 8 * tq * tq * 4
    return pl.pallas_call(
        functools.partial(_sb_attn_kernel, tq=tq, heads=heads),
        grid=(b, n_heads // heads, s // tq),
        in_specs=[pl.BlockSpec((None, tq, gw), lambda bi, h, i: (bi, i, qb + h)),
                  pl.BlockSpec((None, s, gw), lambda bi, h, i: (bi, 0, kb + h)),
                  pl.BlockSpec((None, s, gw), lambda bi, h, i: (bi, 0, vb + h)),
                  pl.BlockSpec((None, tq, gw), lambda bi, h, i: (bi, i, gb + h))],
        out_specs=pl.BlockSpec((None, tq, gw), lambda bi, h, i: (bi, i, h)),
        out_shape=jax.ShapeDtypeStruct((b, s, n_heads * HEAD_DIM), BF16),
        scratch_shapes=[pltpu.VMEM((2, heads, tq, tq), F32),
                        pltpu.VMEM((heads, tq, tq), BF16),
                        pltpu.VMEM((heads, tq, LANES), F32),
                        pltpu.VMEM((tq, gw), F32)],
        compiler_params=_compiler_params(("parallel", "parallel", "arbitrary"), blocks),
        name="sb_attention",
    )(proj, proj, proj, proj)


def _swa_kernel(sink_ref, q_ref, kp_ref, kc_ref, vp_ref, vc_ref, g_ref, o_ref, *, group):
    kv = pl.program_id(1)
    n = pl.program_id(2)
    k = jnp.concatenate([kp_ref[...], kc_ref[...]], axis=0)
    v = jnp.concatenate([vp_ref[...], vc_ref[...]], axis=0)
    r = lax.broadcasted_iota(jnp.int32, (WINDOW, 2 * WINDOW), 0)
    c = lax.broadcasted_iota(jnp.int32, (WINDOW, 2 * WINDOW), 1)
    valid = (c > r) & (c <= r + WINDOW) & ((c >= WINDOW) | (n > 0))
    outs = []
    for g in range(group):
        qg = q_ref[:, g * HEAD_DIM:(g + 1) * HEAD_DIM]
        s = lax.dot_general(qg, k, (((1,), (1,)), ((), ())), preferred_element_type=F32)
        s = jnp.where(valid, s, MASKED_SCORE)
        sink = sink_ref[kv * group + g]
        m = jnp.maximum(jnp.max(s, axis=-1, keepdims=True), sink)
        p = jnp.exp(s - m)
        denom = jnp.sum(p, axis=-1, keepdims=True) + jnp.exp(sink - m)
        outs.append(jnp.dot(p.astype(BF16), v, preferred_element_type=F32) / denom)
    o = jnp.concatenate(outs, axis=1)
    o_ref[...] = (o * _silu(g_ref[...].astype(F32))).astype(o_ref.dtype)


def _swa_attention(proj, sinks, q_off, k_off, v_off, g_off, n_heads, n_kv):
    b, s, _ = proj.shape
    group = n_heads // n_kv
    gw = group * HEAD_DIM
    nblk = s // WINDOW
    qb, gb = _col_block(q_off, gw), _col_block(g_off, gw)
    kb, vb = _col_block(k_off, HEAD_DIM), _col_block(v_off, HEAD_DIM)
    blocks = 3 * WINDOW * gw * 2 + 4 * WINDOW * HEAD_DIM * 2 + 8 * group * WINDOW * 2 * WINDOW * 4
    return pl.pallas_call(
        functools.partial(_swa_kernel, group=group),
        grid=(b, n_kv, nblk),
        in_specs=[pl.BlockSpec(memory_space=pltpu.SMEM),
                  pl.BlockSpec((None, WINDOW, gw), lambda bi, kv, n: (bi, n, qb + kv)),
                  pl.BlockSpec((None, WINDOW, HEAD_DIM), lambda bi, kv, n: (bi, jnp.maximum(n - 1, 0), kb + kv)),
                  pl.BlockSpec((None, WINDOW, HEAD_DIM), lambda bi, kv, n: (bi, n, kb + kv)),
                  pl.BlockSpec((None, WINDOW, HEAD_DIM), lambda bi, kv, n: (bi, jnp.maximum(n - 1, 0), vb + kv)),
                  pl.BlockSpec((None, WINDOW, HEAD_DIM), lambda bi, kv, n: (bi, n, vb + kv)),
                  pl.BlockSpec((None, WINDOW, gw), lambda bi, kv, n: (bi, n, gb + kv))],
        out_specs=pl.BlockSpec((None, WINDOW, gw), lambda bi, kv, n: (bi, n, kv)),
        out_shape=jax.ShapeDtypeStruct((b, s, n_heads * HEAD_DIM), BF16),
        compiler_params=_compiler_params(("parallel", "parallel", "arbitrary"), blocks),
        name="swa_attention",
    )(sinks, proj, proj, proj, proj, proj, proj)


def _conv_kernel(x_ref, xp_ref, c_ref, cp_ref, b_ref, g_ref, w_ref, o_ref):
    i = pl.program_id(1)
    h = c_ref[...].astype(F32) * x_ref[...].astype(F32)
    hp = cp_ref[...].astype(F32) * xp_ref[...].astype(F32)
    hp = jnp.where(i > 0, hp, 0.0)
    ext = jnp.concatenate([hp, h], axis=0)
    h1 = pltpu.roll(ext, 1, 0)[SUBLANES:]
    h2 = pltpu.roll(ext, 2, 0)[SUBLANES:]
    w = w_ref[...]
    y = w[0:1] * h2 + w[1:2] * h1 + w[2:3] * h
    y = b_ref[...].astype(F32) * y
    o_ref[...] = (y * _silu(g_ref[...].astype(F32))).astype(o_ref.dtype)


def _short_conv(proj, conv_w, x_off, b_off, c_off, g_off, width):
    b, s, _ = proj.shape
    tm = min(512, s)
    assert s % tm == 0 and tm % SUBLANES == 0
    xb, bb, cb, gb = (_col_block(o, width) for o in (x_off, b_off, c_off, g_off))
    rows_per_blk = tm // SUBLANES

    def cur(col):
        return pl.BlockSpec((None, tm, width), lambda bi, i: (bi, i, col))

    def prev(col):
        return pl.BlockSpec((None, SUBLANES, width),
                            lambda bi, i: (bi, jnp.maximum(i * rows_per_blk - 1, 0), col))

    blocks = 5 * tm * width * 2 + 6 * tm * width * 4
    return pl.pallas_call(
        _conv_kernel,
        grid=(b, s // tm),
        in_specs=[cur(xb), prev(xb), cur(cb), prev(cb), cur(bb), cur(gb),
                  pl.BlockSpec((CONV_WIDTH, width), lambda bi, i: (0, 0))],
        out_specs=pl.BlockSpec((None, tm, width), lambda bi, i: (bi, i, 0)),
        out_shape=jax.ShapeDtypeStruct((b, s, width), BF16),
        compiler_params=_compiler_params(("parallel", "arbitrary"), blocks),
        name="short_conv",
    )(proj, proj, proj, proj, proj, proj, conv_w)


def _gmlp_kernel(u_ref, v_ref, g_ref, vg_ref, ws_ref, bias_ref, o_ref, *, n_chunks, n_groups):
    v = v_ref[...].astype(F32)
    ms = jnp.mean(v * v, axis=-1, keepdims=True)
    vn = (v * lax.rsqrt(ms + EPS) * vg_ref[...]).astype(BF16)
    r = lax.broadcasted_iota(jnp.int32, (GMLP_CHUNK, GMLP_CHUNK), 0)
    c = lax.broadcasted_iota(jnp.int32, (GMLP_CHUNK, GMLP_CHUNK), 1)
    causal = c <= r
    gd = GMLP_GROUP_DIM
    mixed = []
    for g in range(n_groups):
        wsg = jnp.where(causal, ws_ref[g], 0.0).astype(BF16)
        rhs = jnp.concatenate(
            [vn[ch * GMLP_CHUNK:(ch + 1) * GMLP_CHUNK, g * gd:(g + 1) * gd] for ch in range(n_chunks)], axis=1)
        mixed.append(jnp.dot(wsg, rhs, preferred_element_type=F32))
    bias = bias_ref[...]
    rows = []
    for ch in range(n_chunks):
        rows.append(jnp.concatenate([mixed[g][:, ch * gd:(ch + 1) * gd] for g in range(n_groups)], axis=1) + bias)
    mix = jnp.concatenate(rows, axis=0)
    y = u_ref[...].astype(F32) * mix
    o_ref[...] = (y * _silu(g_ref[...].astype(F32))).astype(o_ref.dtype)


def _spatial_gating(proj, v_gain, w_s, bias_tw, u_off, v_off, g_off, width):
    b, s, _ = proj.shape
    n_chunks = 2
    tc = n_chunks * GMLP_CHUNK
    assert s % tc == 0
    n_groups = w_s.shape[0]
    ub, vb, gb = (_col_block(o, width) for o in (u_off, v_off, g_off))

    def cur(col):
        return pl.BlockSpec((None, tc, width), lambda bi, i: (bi, i, col))

    blocks = 4 * tc * width * 2 + n_groups * GMLP_CHUNK * GMLP_CHUNK * 4 + 8 * tc * width * 4
    return pl.pallas_call(
        functools.partial(_gmlp_kernel, n_chunks=n_chunks, n_groups=n_groups),
        grid=(b, s // tc),
        in_specs=[cur(ub), cur(vb), cur(gb),
                  pl.BlockSpec((1, width), lambda bi, i: (0, 0)),
                  pl.BlockSpec((n_groups, GMLP_CHUNK, GMLP_CHUNK), lambda bi, i: (0, 0, 0)),
                  pl.BlockSpec((GMLP_CHUNK, width), lambda bi, i: (0, 0))],
        out_specs=pl.BlockSpec((None, tc, width), lambda bi, i: (bi, i, 0)),
        out_shape=jax.ShapeDtypeStruct((b, s, width), BF16),
        compiler_params=_compiler_params(("parallel", "arbitrary"), blocks),
        name="spatial_gating",
    )(proj, proj, proj, v_gain.reshape(1, width), w_s, bias_tw)


def _outproj_kernel(ya_ref, yb_ref, yc_ref, yd_ref, w_ref, x_ref, o_ref, *, width):
    acc = x_ref[...]
    for idx, y_ref in enumerate((ya_ref, yb_ref, yc_ref, yd_ref)):
        acc = acc + jnp.dot(y_ref[...], w_ref[idx * width:(idx + 1) * width, :], preferred_element_type=F32)
    o_ref[...] = acc


def _outproj(ys, w, x):
    m, d = x.shape
    width = ys[0].shape[1]
    tm = min(1024, m)
    tn = min(512, d)
    assert m % tm == 0 and d % tn == 0
    blocks = 4 * tm * width * 2 + 4 * width * tn * 2 + 3 * tm * tn * 4
    y_spec = pl.BlockSpec((tm, width), lambda i, j: (i, 0))
    return pl.pallas_call(
        functools.partial(_outproj_kernel, width=width),
        grid=(m // tm, d // tn),
        in_specs=[y_spec, y_spec, y_spec, y_spec,
                  pl.BlockSpec((4 * width, tn), lambda i, j: (0, j)),
                  pl.BlockSpec((tm, tn), lambda i, j: (i, j))],
        out_specs=pl.BlockSpec((tm, tn), lambda i, j: (i, j)),
        out_shape=jax.ShapeDtypeStruct((m, d), F32),
        compiler_params=_compiler_params(("parallel", "arbitrary"), blocks),
        name="outproj",
    )(*ys, w, x)


def kernel(x, norm_gain, w_in, conv_w, gmlp_v_gain, gmlp_w_s, gmlp_b_s, swa_sinks, w_out, final_gain):
    batch, seq, d_model = x.shape
    depth = w_in.shape[0]
    m = batch * seq
    w_a = w_b = w_c = w_d = d_model // 4
    n_sb_heads = w_a // HEAD_DIM
    n_swa_heads = w_b // HEAD_DIM
    n_swa_kv = max(1, n_swa_heads // 4)
    w_kv = n_swa_kv * HEAD_DIM
    n_groups = gmlp_w_s.shape[1]
    assert n_swa_heads == swa_sinks.shape[1] and w_d == n_groups * GMLP_GROUP_DIM

    names = ("qa", "ka", "va", "ga", "qb", "kb", "vb", "gb", "xc", "bc", "cc", "gc", "ud", "vd", "gd")
    widths = (w_a, w_a, w_a, w_a, w_b, w_kv, w_kv, w_b, w_c, w_c, w_c, w_c, w_d, w_d, w_d)
    src, start = {}, 0
    for name, wd in zip(names, widths):
        src[name] = (start, wd)
        start += wd
    n_proj = start
    assert n_proj == w_in.shape[2]
    order = [nm for nm in names if nm not in ("kb", "vb")] + ["kb", "vb"]
    dst, start = {}, 0
    for name in order:
        dst[name] = start
        start += src[name][1]
    (qa, ka, va, ga, qb, kb, vb, gb, xc, bc, cc, gc, ud, vd, gd) = (dst[nm] for nm in names)

    def permuted_bf16(w):
        return jnp.concatenate([w[:, src[nm][0]:src[nm][0] + src[nm][1]] for nm in order], axis=1).astype(BF16)

    scale = HEAD_DIM ** -0.5
    col = jnp.arange(n_proj)
    col_scale = jnp.where((col >= qa) & (col < qa + w_a), scale * LOG2_E,
                          jnp.where((col >= qb) & (col < qb + w_b), scale, 1.0))
    col_scale = col_scale.astype(F32).reshape(1, n_proj)

    h = x.reshape(m, d_model)
    for l in range(depth):
        hn = _rmsnorm(h, norm_gain[l], BF16)
        proj = _inproj(hn, permuted_bf16(w_in[l]), col_scale).reshape(batch, seq, n_proj)
        y_a = _sb_attention(proj, qa, ka, va, ga, n_sb_heads)
        y_b = _swa_attention(proj, swa_sinks[l], qb, kb, vb, gb, n_swa_heads, n_swa_kv)
        y_c = _short_conv(proj, conv_w[l], xc, bc, cc, gc, w_c)
        bias_tw = jnp.repeat(gmlp_b_s[l].T, GMLP_GROUP_DIM, axis=1)
        y_d = _spatial_gating(proj, gmlp_v_gain[l], gmlp_w_s[l], bias_tw, ud, vd, gd, w_d)
        ys = [y.reshape(m, -1) for y in (y_a, y_b, y_c, y_d)]
        h = _outproj(ys, w_out[l].astype(BF16), h)
    out = _rmsnorm(h, final_gain, F32)
    return out.reshape(batch, seq, d_model)
```

```python
import functools

import jax
import jax.numpy as jnp
from jax import lax
from jax.experimental import pallas as pl
from jax.experimental.pallas import tpu as pltpu

HEAD_DIM = 128
GMLP_GROUP_DIM = 128
GMLP_CHUNK = 128
WINDOW = 128
CONV_WIDTH = 3
EPS = 1e-6
LOG2_E = 1.4426950408889634
EXP2_CLAMP = 64.0

LANES = 128
SUBLANES = 8
V7X_VMEM_BYTES = 64 * 1024 * 1024
VMEM_SPILL_ALLOWANCE = 6 * 1024 * 1024
MASKED_SCORE = -1e30
SB_HEADS_PER_STEP = 4
SWA_BLOCKS_PER_STEP = 4
GMLP_CHUNKS_PER_STEP = 4

F32 = jnp.float32
BF16 = jnp.bfloat16


def _compiler_params(semantics, block_bytes):
    limit = min(2 * block_bytes + VMEM_SPILL_ALLOWANCE, V7X_VMEM_BYTES - 4 * 1024 * 1024)
    return pltpu.CompilerParams(dimension_semantics=semantics, vmem_limit_bytes=int(limit))


def _col_block(offset, width):
    assert offset % width == 0, (offset, width)
    return offset // width


def _silu(g):
    return g * jax.nn.sigmoid(g)


def _rmsnorm_kernel(x_ref, g_ref, o_ref):
    x = x_ref[...]
    ms = jnp.mean(x * x, axis=-1, keepdims=True)
    o_ref[...] = (x * lax.rsqrt(ms + EPS) * g_ref[...]).astype(o_ref.dtype)


def _rmsnorm(x, gain, out_dtype):
    m, d = x.shape
    tm = min(256, m)
    return pl.pallas_call(
        _rmsnorm_kernel,
        grid=(m // tm,),
        in_specs=[pl.BlockSpec((tm, d), lambda i: (i, 0)),
                  pl.BlockSpec((1, d), lambda i: (0, 0))],
        out_specs=pl.BlockSpec((tm, d), lambda i: (i, 0)),
        out_shape=jax.ShapeDtypeStruct((m, d), out_dtype),
        compiler_params=_compiler_params(("parallel",), tm * d * 12),
        name="rmsnorm",
    )(x, gain.reshape(1, d))


def _inproj_kernel(h_ref, w_ref, s_ref, o_ref):
    acc = jnp.dot(h_ref[...], w_ref[...].astype(BF16), preferred_element_type=F32)
    o_ref[...] = (acc * s_ref[...]).astype(o_ref.dtype)


def _inproj(hn, w_all, layer, col_scale):
    m, d = hn.shape
    n = w_all.shape[2]
    tm = min(1024, m)
    tn = 512
    assert m % tm == 0 and n % tn == 0
    blocks = tm * d * 2 + d * tn * (4 + 2) + tm * tn * (2 + 4)
    return pl.pallas_call(
        _inproj_kernel,
        grid=(m // tm, n // tn),
        in_specs=[pl.BlockSpec((tm, d), lambda i, j: (i, 0)),
                  pl.BlockSpec((None, d, tn), lambda i, j: (layer, 0, j)),
                  pl.BlockSpec((1, tn), lambda i, j: (0, j))],
        out_specs=pl.BlockSpec((tm, tn), lambda i, j: (i, j)),
        out_shape=jax.ShapeDtypeStruct((m, n), BF16),
        compiler_params=_compiler_params(("parallel", "arbitrary"), blocks),
        name="inproj",
    )(hn, w_all, col_scale)


def _sb_attn_kernel(q_ref, k_ref, v_ref, g_ref, o_ref, z_scr, w_scr, carry_scr, acc_scr, *, tq, heads):
    i = pl.program_id(2)
    dh = HEAD_DIM
    hds = range(heads)
    row = lax.broadcasted_iota(jnp.int32, (tq, tq), 0)
    col = lax.broadcasted_iota(jnp.int32, (tq, tq), 1)
    strictly_causal = col < row
    block_total = lax.broadcasted_iota(jnp.int32, (tq, LANES), 1) == LANES - 1
    later_keys = jnp.concatenate([row > col, block_total], axis=1).astype(BF16)

    def key_rows(kb):
        return pl.ds(pl.multiple_of(kb * tq, tq), tq)

    def head_cols(hd):
        return slice(hd * dh, (hd + 1) * dh)

    def scores(kb, hd):
        return lax.dot_general(q_ref[:, head_cols(hd)], k_ref[key_rows(kb), head_cols(hd)],
                               (((1,), (1,)), ((), ())), preferred_element_type=F32)

    def softplus_terms(z, carry, diagonal):
        p = jnp.maximum(jnp.log2(1.0 + jnp.exp2(jnp.minimum(z, EXP2_CLAMP))), z)
        if diagonal:
            p = jnp.where(strictly_causal, p, 0.0)
        p = jnp.concatenate([p[:, :tq - LANES], p[:, tq - LANES:] + carry], axis=1)
        return p.astype(BF16), z - p

    def finish(u, cs, hd, diagonal):
        w = jnp.exp2(u - cs[:, :tq])
        if diagonal:
            w = jnp.where(strictly_causal, w, 0.0)
        w_scr[hd] = w.astype(BF16)
        carry_scr[hd] = cs[:, tq:]

    def apply_weights(kb):
        for hd in hds:
            acc_scr[:, head_cols(hd)] += jnp.dot(w_scr[hd], v_ref[key_rows(kb), head_cols(hd)],
                                                 preferred_element_type=F32)

    acc_scr[...] = jnp.zeros_like(acc_scr)
    zs = [scores(i, hd) for hd in hds]
    zero_carry = jnp.zeros((tq, LANES), F32)
    terms = [softplus_terms(zs[hd], zero_carry, True) for hd in hds]
    nxt = jnp.maximum(i - 1, 0)
    for hd in hds:
        p_bf16, u = terms[hd]
        cs = jnp.dot(p_bf16, later_keys, preferred_element_type=F32)
        finish(u, cs, hd, True)
        z_scr[0, hd] = scores(nxt, hd)

    def step(kb, z_in, z_out):
        apply_weights(kb + 1)
        nxt = jnp.maximum(kb - 1, 0)
        for hd in hds:
            z_scr[z_out, hd] = scores(nxt, hd)
        terms = []
        for hd in hds:
            p_bf16, u = softplus_terms(z_scr[z_in, hd], carry_scr[hd], False)
            z_scr[z_in, hd] = u
            terms.append(p_bf16)
        for hd in hds:
            cs = jnp.dot(terms[hd], later_keys, preferred_element_type=F32)
            finish(z_scr[z_in, hd], cs, hd, False)

    def two_steps(m, _):
        kb = i - 1 - 2 * m
        step(kb, 0, 1)
        step(kb - 1, 1, 0)
        return 0

    lax.fori_loop(0, i // 2, two_steps, 0)

    @pl.when(i % 2 == 1)
    def _():
        step(0, 0, 1)

    apply_weights(0)
    o_ref[...] = (acc_scr[...] * _silu(g_ref[...].astype(F32))).astype(o_ref.dtype)


def _sb_attention(proj, q_off, k_off, v_off, g_off, n_heads):
    b, s, _ = proj.shape
    tq = min(256, s)
    heads = min(SB_HEADS_PER_STEP, n_heads)
    assert s % tq == 0 and tq % LANES == 0 and n_heads % heads == 0
    gw = heads * HEAD_DIM
    qb, kb, vb, gb = (_col_block(o, gw) for o in (q_off, k_off, v_off, g_off))
    blocks = 3 * tq * gw * 2 + 2 * s * gw * 2 + heads * 8 * tq * tq * 4
    return pl.pallas_call(
        functools.partial(_sb_attn_kernel, tq=tq, heads=heads),
        grid=(b, n_heads // heads, s // tq),
        in_specs=[pl.BlockSpec((None, tq, gw), lambda bi, h, i: (bi, i, qb + h)),
                  pl.BlockSpec((None, s, gw), lambda bi, h, i: (bi, 0, kb + h)),
                  pl.BlockSpec((None, s, gw), lambda bi, h, i: (bi, 0, vb + h)),
                  pl.BlockSpec((None, tq, gw), lambda bi, h, i: (bi, i, gb + h))],
        out_specs=pl.BlockSpec((None, tq, gw), lambda bi, h, i: (bi, i, h)),
        out_shape=jax.ShapeDtypeStruct((b, s, n_heads * HEAD_DIM), BF16),
        scratch_shapes=[pltpu.VMEM((2, heads, tq, tq), F32),
                        pltpu.VMEM((heads, tq, tq), BF16),
                        pltpu.VMEM((heads, tq, LANES), F32),
                        pltpu.VMEM((tq, gw), F32)],
        compiler_params=_compiler_params(("parallel", "parallel", "arbitrary"), blocks),
        name="sb_attention",
    )(proj, proj, proj, proj)


def _swa_kernel(sink_ref, q_ref, kp_ref, kc_ref, vp_ref, vc_ref, g_ref, o_ref, *, group, n_sub):
    kv = pl.program_id(1)
    n = pl.program_id(2)
    k_all = jnp.concatenate([kp_ref[...], kc_ref[...]], axis=0)
    v_all = jnp.concatenate([vp_ref[...], vc_ref[...]], axis=0)
    r = lax.broadcasted_iota(jnp.int32, (WINDOW, 2 * WINDOW), 0)
    c = lax.broadcasted_iota(jnp.int32, (WINDOW, 2 * WINDOW), 1)
    in_window = (c > r) & (c <= r + WINDOW)
    for sub in range(n_sub):
        rows = slice(sub * WINDOW, (sub + 1) * WINDOW)
        k = k_all[sub * WINDOW:(sub + 2) * WINDOW]
        v = v_all[sub * WINDOW:(sub + 2) * WINDOW]
        valid = in_window & ((c >= WINDOW) | (n > 0)) if sub == 0 else in_window
        outs = []
        for g in range(group):
            qg = q_ref[rows, g * HEAD_DIM:(g + 1) * HEAD_DIM]
            s = lax.dot_general(qg, k, (((1,), (1,)), ((), ())), preferred_element_type=F32)
            s = jnp.where(valid, s, MASKED_SCORE)
            sink = sink_ref[kv * group + g]
            m = jnp.maximum(jnp.max(s, axis=-1, keepdims=True), sink)
            p = jnp.exp(s - m)
            denom = jnp.sum(p, axis=-1, keepdims=True) + jnp.exp(sink - m)
            outs.append(jnp.dot(p.astype(BF16), v, preferred_element_type=F32) / denom)
        o = jnp.concatenate(outs, axis=1)
        o_ref[rows, :] = (o * _silu(g_ref[rows, :].astype(F32))).astype(o_ref.dtype)


def _swa_attention(proj, sinks, q_off, k_off, v_off, g_off, n_heads, n_kv):
    b, s, _ = proj.shape
    group = n_heads // n_kv
    gw = group * HEAD_DIM
    n_sub = min(SWA_BLOCKS_PER_STEP, s // WINDOW)
    tq = n_sub * WINDOW
    assert s % tq == 0
    qb, gb = _col_block(q_off, gw), _col_block(g_off, gw)
    kb, vb = _col_block(k_off, HEAD_DIM), _col_block(v_off, HEAD_DIM)
    blocks = 3 * tq * gw * 2 + 2 * (tq + WINDOW) * HEAD_DIM * 2 + 8 * group * n_sub * WINDOW * 2 * WINDOW * 4

    def cur(col):
        return pl.BlockSpec((None, tq, HEAD_DIM), lambda bi, kv, n: (bi, n, col + kv))

    def prev(col):
        return pl.BlockSpec((None, WINDOW, HEAD_DIM), lambda bi, kv, n: (bi, jnp.maximum(n * n_sub - 1, 0), col + kv))

    return pl.pallas_call(
        functools.partial(_swa_kernel, group=group, n_sub=n_sub),
        grid=(b, n_kv, s // tq),
        in_specs=[pl.BlockSpec(memory_space=pltpu.SMEM),
                  pl.BlockSpec((None, tq, gw), lambda bi, kv, n: (bi, n, qb + kv)),
                  prev(kb), cur(kb), prev(vb), cur(vb),
                  pl.BlockSpec((None, tq, gw), lambda bi, kv, n: (bi, n, gb + kv))],
        out_specs=pl.BlockSpec((None, tq, gw), lambda bi, kv, n: (bi, n, kv)),
        out_shape=jax.ShapeDtypeStruct((b, s, n_heads * HEAD_DIM), BF16),
        compiler_params=_compiler_params(("parallel", "parallel", "arbitrary"), blocks),
        name="swa_attention",
    )(sinks, proj, proj, proj, proj, proj, proj)


def _conv_kernel(x_ref, xp_ref, c_ref, cp_ref, b_ref, g_ref, w_ref, o_ref):
    i = pl.program_id(1)
    h = c_ref[...].astype(F32) * x_ref[...].astype(F32)
    hp = cp_ref[...].astype(F32) * xp_ref[...].astype(F32)
    hp = jnp.where(i > 0, hp, 0.0)
    ext = jnp.concatenate([hp, h], axis=0)
    h1 = pltpu.roll(ext, 1, 0)[SUBLANES:]
    h2 = pltpu.roll(ext, 2, 0)[SUBLANES:]
    w = w_ref[...]
    y = w[0:1] * h2 + w[1:2] * h1 + w[2:3] * h
    y = b_ref[...].astype(F32) * y
    o_ref[...] = (y * _silu(g_ref[...].astype(F32))).astype(o_ref.dtype)


def _short_conv(proj, conv_w, x_off, b_off, c_off, g_off, width):
    b, s, _ = proj.shape
    tm = min(512, s)
    cw = width // 2
    assert s % tm == 0 and tm % SUBLANES == 0 and cw % LANES == 0
    xb, bb, cb, gb = (_col_block(o, cw) for o in (x_off, b_off, c_off, g_off))
    rows_per_blk = tm // SUBLANES

    def cur(col):
        return pl.BlockSpec((None, tm, cw), lambda bi, i, c: (bi, i, col + c))

    def prev(col):
        return pl.BlockSpec((None, SUBLANES, cw),
                            lambda bi, i, c: (bi, jnp.maximum(i * rows_per_blk - 1, 0), col + c))

    blocks = 5 * tm * cw * 2 + 6 * tm * cw * 4
    return pl.pallas_call(
        _conv_kernel,
        grid=(b, s // tm, width // cw),
        in_specs=[cur(xb), prev(xb), cur(cb), prev(cb), cur(bb), cur(gb),
                  pl.BlockSpec((CONV_WIDTH, cw), lambda bi, i, c: (0, c))],
        out_specs=pl.BlockSpec((None, tm, cw), lambda bi, i, c: (bi, i, c)),
        out_shape=jax.ShapeDtypeStruct((b, s, width), BF16),
        compiler_params=_compiler_params(("parallel", "arbitrary", "arbitrary"), blocks),
        name="short_conv",
    )(proj, proj, proj, proj, proj, proj, conv_w)


def _gmlp_kernel(u_ref, v_ref, v0_ref, v1_ref, g_ref, vg_ref, ws_ref, bias_ref, o_ref, *, n_chunks, n_groups):
    v0 = v0_ref[...].astype(F32)
    v1 = v1_ref[...].astype(F32)
    full_width = v0.shape[1] + v1.shape[1]
    ms = (jnp.sum(v0 * v0, axis=-1, keepdims=True) + jnp.sum(v1 * v1, axis=-1, keepdims=True)) / full_width
    vn = (v_ref[...].astype(F32) * lax.rsqrt(ms + EPS) * vg_ref[...]).astype(BF16)
    r = lax.broadcasted_iota(jnp.int32, (GMLP_CHUNK, GMLP_CHUNK), 0)
    c = lax.broadcasted_iota(jnp.int32, (GMLP_CHUNK, GMLP_CHUNK), 1)
    causal = c <= r
    gd = GMLP_GROUP_DIM
    mixed = []
    for g in range(n_groups):
        wsg = jnp.where(causal, ws_ref[g], 0.0).astype(BF16)
        rhs = jnp.concatenate(
            [vn[ch * GMLP_CHUNK:(ch + 1) * GMLP_CHUNK, g * gd:(g + 1) * gd] for ch in range(n_chunks)], axis=1)
        mixed.append(jnp.dot(wsg, rhs, preferred_element_type=F32))
    bias = bias_ref[...]
    rows = []
    for ch in range(n_chunks):
        rows.append(jnp.concatenate([mixed[g][:, ch * gd:(ch + 1) * gd] for g in range(n_groups)], axis=1) + bias)
    mix = jnp.concatenate(rows, axis=0)
    y = u_ref[...].astype(F32) * mix
    o_ref[...] = (y * _silu(g_ref[...].astype(F32))).astype(o_ref.dtype)


def _spatial_gating(proj, v_gain, w_s, bias_tw, u_off, v_off, g_off, width):
    b, s, _ = proj.shape
    n_chunks = min(GMLP_CHUNKS_PER_STEP, s // GMLP_CHUNK)
    tc = n_chunks * GMLP_CHUNK
    cw = width // 2
    n_groups = w_s.shape[0]
    gpb = cw // GMLP_GROUP_DIM
    assert s % tc == 0 and cw % GMLP_GROUP_DIM == 0 and n_groups == 2 * gpb
    ub, vb, gb = (_col_block(o, cw) for o in (u_off, v_off, g_off))

    def cur(col):
        return pl.BlockSpec((None, tc, cw), lambda bi, i, c: (bi, i, col + c))

    def fixed(col):
        return pl.BlockSpec((None, tc, cw), lambda bi, i, c: (bi, i, col))

    blocks = 6 * tc * cw * 2 + gpb * GMLP_CHUNK * GMLP_CHUNK * 4 + 8 * tc * cw * 4
    return pl.pallas_call(
        functools.partial(_gmlp_kernel, n_chunks=n_chunks, n_groups=gpb),
        grid=(b, s // tc, width // cw),
        in_specs=[cur(ub), cur(vb), fixed(vb), fixed(vb + 1), cur(gb),
                  pl.BlockSpec((1, cw), lambda bi, i, c: (0, c)),
                  pl.BlockSpec((gpb, GMLP_CHUNK, GMLP_CHUNK), lambda bi, i, c: (c, 0, 0)),
                  pl.BlockSpec((GMLP_CHUNK, cw), lambda bi, i, c: (0, c))],
        out_specs=pl.BlockSpec((None, tc, cw), lambda bi, i, c: (bi, i, c)),
        out_shape=jax.ShapeDtypeStruct((b, s, width), BF16),
        compiler_params=_compiler_params(("parallel", "arbitrary", "arbitrary"), blocks),
        name="spatial_gating",
    )(proj, proj, proj, proj, proj, v_gain.reshape(1, width), w_s, bias_tw)


def _outproj_kernel(ya_ref, yb_ref, yc_ref, yd_ref, w_ref, x_ref, o_ref, *, width):
    acc = x_ref[...]
    for idx, y_ref in enumerate((ya_ref, yb_ref, yc_ref, yd_ref)):
        acc = acc + jnp.dot(y_ref[...], w_ref[idx * width:(idx + 1) * width, :], preferred_element_type=F32)
    o_ref[...] = acc


def _outproj(ys, w_all, layer, x):
    m, d = x.shape
    width = ys[0].shape[1]
    tm = min(1024, m)
    tn = min(512, d)
    assert m % tm == 0 and d % tn == 0
    blocks = 4 * tm * width * 2 + 4 * width * tn * 2 + 3 * tm * tn * 4
    y_spec = pl.BlockSpec((tm, width), lambda i, j: (i, 0))
    return pl.pallas_call(
        functools.partial(_outproj_kernel, width=width),
        grid=(m // tm, d // tn),
        in_specs=[y_spec, y_spec, y_spec, y_spec,
                  pl.BlockSpec((None, 4 * width, tn), lambda i, j: (layer, 0, j)),
                  pl.BlockSpec((tm, tn), lambda i, j: (i, j))],
        out_specs=pl.BlockSpec((tm, tn), lambda i, j: (i, j)),
        out_shape=jax.ShapeDtypeStruct((m, d), F32),
        compiler_params=_compiler_params(("parallel", "arbitrary"), blocks),
        name="outproj",
    )(*ys, w_all, x)


def kernel(x, norm_gain, w_in, conv_w, gmlp_v_gain, gmlp_w_s, gmlp_b_s, swa_sinks, w_out, final_gain):
    batch, seq, d_model = x.shape
    depth = w_in.shape[0]
    m = batch * seq
    w_a = w_b = w_c = w_d = d_model // 4
    n_sb_heads = w_a // HEAD_DIM
    n_swa_heads = w_b // HEAD_DIM
    n_swa_kv = max(1, n_swa_heads // 4)
    w_kv = n_swa_kv * HEAD_DIM
    n_groups = gmlp_w_s.shape[1]
    assert n_swa_heads == swa_sinks.shape[1] and w_d == n_groups * GMLP_GROUP_DIM

    widths = (w_a, w_a, w_a, w_a, w_b, w_kv, w_kv, w_b, w_c, w_c, w_c, w_c, w_d, w_d, w_d)
    offs = [0]
    for wd in widths:
        offs.append(offs[-1] + wd)
    (qa, ka, va, ga, qb, kb, vb, gb, xc, bc, cc, gc, ud, vd, gd) = offs[:-1]
    n_proj = offs[-1]
    assert n_proj == w_in.shape[2]

    scale = HEAD_DIM ** -0.5
    col = jnp.arange(n_proj)
    col_scale = jnp.where((col >= qa) & (col < qa + w_a), scale * LOG2_E,
                          jnp.where((col >= qb) & (col < qb + w_b), scale, 1.0))
    col_scale = col_scale.astype(F32).reshape(1, n_proj)

    w_out_bf16 = w_out.astype(BF16)
    h = x.reshape(m, d_model)
    for l in range(depth):
        hn = _rmsnorm(h, norm_gain[l], BF16)
        proj = _inproj(hn, w_in, l, col_scale).reshape(batch, seq, n_proj)
        y_a = _sb_attention(proj, qa, ka, va, ga, n_sb_heads)
        y_b = _swa_attention(proj, swa_sinks[l], qb, kb, vb, gb, n_swa_heads, n_swa_kv)
        y_c = _short_conv(proj, conv_w[l], xc, bc, cc, gc, w_c)
        bias_tw = jnp.repeat(gmlp_b_s[l].T, GMLP_GROUP_DIM, axis=1)
        y_d = _spatial_gating(proj, gmlp_v_gain[l], gmlp_w_s[l], bias_tw, ud, vd, gd, w_d)
        ys = [y.reshape(m, -1) for y in (y_a, y_b, y_c, y_d)]
        h = _outproj(ys, w_out_bf16, l, h)
    out = _rmsnorm(h, final_gain, F32)
    return out.reshape(batch, seq, d_model)
```

```python
import functools

import jax
import jax.numpy as jnp
from jax import lax
from jax.experimental import pallas as pl
from jax.experimental.pallas import tpu as pltpu

HEAD_DIM = 128
GMLP_GROUP_DIM = 128
GMLP_CHUNK = 128
WINDOW = 128
CONV_WIDTH = 3
EPS = 1e-6
LOG2_E = 1.4426950408889634
EXP2_CLAMP = 64.0

LANES = 128
SUBLANES = 8
V7X_VMEM_BYTES = 64 * 1024 * 1024
VMEM_SPILL_ALLOWANCE = 6 * 1024 * 1024
MASKED_SCORE = -1e30
SB_HEADS_PER_STEP = 4
SWA_BLOCKS_PER_STEP = 4
GMLP_CHUNKS_PER_STEP = 8

F32 = jnp.float32
BF16 = jnp.bfloat16


def _compiler_params(semantics, block_bytes):
    limit = min(2 * block_bytes + VMEM_SPILL_ALLOWANCE, V7X_VMEM_BYTES - 4 * 1024 * 1024)
    return pltpu.CompilerParams(dimension_semantics=semantics, vmem_limit_bytes=int(limit))


def _col_block(offset, width):
    assert offset % width == 0, (offset, width)
    return offset // width


def _silu(g):
    return g * jax.nn.sigmoid(g)


def _rmsnorm_kernel(x_ref, g_ref, o_ref):
    x = x_ref[...]
    ms = jnp.mean(x * x, axis=-1, keepdims=True)
    o_ref[...] = (x * lax.rsqrt(ms + EPS) * g_ref[...]).astype(o_ref.dtype)


def _rmsnorm(x, gain, out_dtype):
    m, d = x.shape
    tm = min(256, m)
    return pl.pallas_call(
        _rmsnorm_kernel,
        grid=(m // tm,),
        in_specs=[pl.BlockSpec((tm, d), lambda i: (i, 0)),
                  pl.BlockSpec((1, d), lambda i: (0, 0))],
        out_specs=pl.BlockSpec((tm, d), lambda i: (i, 0)),
        out_shape=jax.ShapeDtypeStruct((m, d), out_dtype),
        compiler_params=_compiler_params(("parallel",), tm * d * 12),
        name="rmsnorm",
    )(x, gain.reshape(1, d))


def _inproj_kernel(h_ref, w_ref, s_ref, o_ref):
    acc = jnp.dot(h_ref[...], w_ref[...].astype(BF16), preferred_element_type=F32)
    o_ref[...] = (acc * s_ref[...]).astype(o_ref.dtype)


def _inproj(hn, w_all, layer, col_scale):
    m, d = hn.shape
    n = w_all.shape[2]
    tm = min(1024, m)
    tn = 512
    assert m % tm == 0 and n % tn == 0
    blocks = tm * d * 2 + d * tn * (4 + 2) + tm * tn * (2 + 4)
    return pl.pallas_call(
        _inproj_kernel,
        grid=(m // tm, n // tn),
        in_specs=[pl.BlockSpec((tm, d), lambda i, j: (i, 0)),
                  pl.BlockSpec((None, d, tn), lambda i, j: (layer, 0, j)),
                  pl.BlockSpec((1, tn), lambda i, j: (0, j))],
        out_specs=pl.BlockSpec((tm, tn), lambda i, j: (i, j)),
        out_shape=jax.ShapeDtypeStruct((m, n), BF16),
        compiler_params=_compiler_params(("parallel", "arbitrary"), blocks),
        name="inproj",
    )(hn, w_all, col_scale)


def _sb_attn_kernel(q_ref, k_ref, v_ref, g_ref, o_ref, z_scr, w_scr, carry_scr, acc_scr, *, tq, heads):
    i = pl.program_id(2)
    dh = HEAD_DIM
    hds = range(heads)
    row = lax.broadcasted_iota(jnp.int32, (tq, tq), 0)
    col = lax.broadcasted_iota(jnp.int32, (tq, tq), 1)
    strictly_causal = col < row
    later_keys = ((row > col) | (col == tq - 1)).astype(BF16)
    last_lane = lax.broadcasted_iota(jnp.int32, (tq, LANES), 1) == LANES - 1

    def key_rows(kb):
        return pl.ds(pl.multiple_of(kb * tq, tq), tq)

    def head_cols(hd):
        return slice(hd * dh, (hd + 1) * dh)

    def scores(kb, hd):
        return lax.dot_general(q_ref[:, head_cols(hd)], k_ref[key_rows(kb), head_cols(hd)],
                               (((1,), (1,)), ((), ())), preferred_element_type=F32)

    def softplus_terms(z, carry, diagonal):
        p = jnp.maximum(jnp.log2(1.0 + jnp.exp2(jnp.minimum(z, EXP2_CLAMP))), z)
        if diagonal:
            p = jnp.where(strictly_causal, p, 0.0)
        p = jnp.concatenate([p[:, :tq - LANES], p[:, tq - LANES:] + carry], axis=1)
        return p.astype(BF16), z - p

    def finish(u, cs, hd, diagonal):
        cs_tail = cs[:, tq - LANES:]
        carry_scr[hd] = jnp.where(last_lane, cs_tail, 0.0)
        cs = jnp.concatenate([cs[:, :tq - LANES], jnp.where(last_lane, 0.0, cs_tail)], axis=1)
        w = jnp.exp2(u - cs)
        if diagonal:
            w = jnp.where(strictly_causal, w, 0.0)
        w_scr[hd] = w.astype(BF16)

    def apply_weights(kb):
        for hd in hds:
            acc_scr[:, head_cols(hd)] += jnp.dot(w_scr[hd], v_ref[key_rows(kb), head_cols(hd)],
                                                 preferred_element_type=F32)

    acc_scr[...] = jnp.zeros_like(acc_scr)
    zs = [scores(i, hd) for hd in hds]
    zero_carry = jnp.zeros((tq, LANES), F32)
    terms = [softplus_terms(zs[hd], zero_carry, True) for hd in hds]
    nxt = jnp.maximum(i - 1, 0)
    for hd in hds:
        p_bf16, u = terms[hd]
        cs = jnp.dot(p_bf16, later_keys, preferred_element_type=F32)
        finish(u, cs, hd, True)
        z_scr[0, hd] = scores(nxt, hd)

    def step(kb, z_in, z_out):
        apply_weights(kb + 1)
        nxt = jnp.maximum(kb - 1, 0)
        for hd in hds:
            z_scr[z_out, hd] = scores(nxt, hd)
        terms = []
        for hd in hds:
            p_bf16, u = softplus_terms(z_scr[z_in, hd], carry_scr[hd], False)
            z_scr[z_in, hd] = u
            terms.append(p_bf16)
        for hd in hds:
            cs = jnp.dot(terms[hd], later_keys, preferred_element_type=F32)
            finish(z_scr[z_in, hd], cs, hd, False)

    def two_steps(m, _):
        kb = i - 1 - 2 * m
        step(kb, 0, 1)
        step(kb - 1, 1, 0)
        return 0

    lax.fori_loop(0, i // 2, two_steps, 0)

    @pl.when(i % 2 == 1)
    def _():
        step(0, 0, 1)

    apply_weights(0)
    o_ref[...] = (acc_scr[...] * _silu(g_ref[...].astype(F32))).astype(o_ref.dtype)


def _sb_attention(proj, q_off, k_off, v_off, g_off, n_heads):
    b, s, _ = proj.shape
    tq = min(256, s)
    heads = min(SB_HEADS_PER_STEP, n_heads)
    assert s % tq == 0 and tq % LANES == 0 and n_heads % heads == 0
    gw = heads * HEAD_DIM
    qb, kb, vb, gb = (_col_block(o, gw) for o in (q_off, k_off, v_off, g_off))
    blocks = 3 * tq * gw * 2 + 2 * s * gw * 2 + heads * 8 * tq * tq * 4
    return pl.pallas_call(
        functools.partial(_sb_attn_kernel, tq=tq, heads=heads),
        grid=(b, n_heads // heads, s // tq),
        in_specs=[pl.BlockSpec((None, tq, gw), lambda bi, h, i: (bi, i, qb + h)),
                  pl.BlockSpec((None, s, gw), lambda bi, h, i: (bi, 0, kb + h)),
                  pl.BlockSpec((None, s, gw), lambda bi, h, i: (bi, 0, vb + h)),
                  pl.BlockSpec((None, tq, gw), lambda bi, h, i: (bi, i, gb + h))],
        out_specs=pl.BlockSpec((None, tq, gw), lambda bi, h, i: (bi, i, h)),
        out_shape=jax.ShapeDtypeStruct((b, s, n_heads * HEAD_DIM), BF16),
        scratch_shapes=[pltpu.VMEM((2, heads, tq, tq), F32),
                        pltpu.VMEM((heads, tq, tq), BF16),
                        pltpu.VMEM((heads, tq, LANES), F32),
                        pltpu.VMEM((tq, gw), F32)],
        compiler_params=_compiler_params(("parallel", "parallel", "arbitrary"), blocks),
        name="sb_attention",
    )(proj, proj, proj, proj)


def _swa_kernel(sink_ref, q_ref, kp_ref, kc_ref, vp_ref, vc_ref, g_ref, o_ref, *, group, n_sub):
    kv = pl.program_id(1)
    n = pl.program_id(2)
    k_all = jnp.concatenate([kp_ref[...], kc_ref[...]], axis=0)
    v_all = jnp.concatenate([vp_ref[...], vc_ref[...]], axis=0)
    r = lax.broadcasted_iota(jnp.int32, (WINDOW, 2 * WINDOW), 0)
    c = lax.broadcasted_iota(jnp.int32, (WINDOW, 2 * WINDOW), 1)
    in_window = (c > r) & (c <= r + WINDOW)
    for sub in range(n_sub):
        rows = slice(sub * WINDOW, (sub + 1) * WINDOW)
        k = k_all[sub * WINDOW:(sub + 2) * WINDOW]
        v = v_all[sub * WINDOW:(sub + 2) * WINDOW]
        valid = in_window & ((c >= WINDOW) | (n > 0)) if sub == 0 else in_window
        outs = []
        for g in range(group):
            qg = q_ref[rows, g * HEAD_DIM:(g + 1) * HEAD_DIM]
            s = lax.dot_general(qg, k, (((1,), (1,)), ((), ())), preferred_element_type=F32)
            s = jnp.where(valid, s, MASKED_SCORE)
            sink = sink_ref[kv * group + g]
            m = jnp.maximum(jnp.max(s, axis=-1, keepdims=True), sink)
            p = jnp.exp(s - m)
            denom = jnp.sum(p, axis=-1, keepdims=True) + jnp.exp(sink - m)
            outs.append(jnp.dot(p.astype(BF16), v, preferred_element_type=F32) / denom)
        o = jnp.concatenate(outs, axis=1)
        o_ref[rows, :] = (o * _silu(g_ref[rows, :].astype(F32))).astype(o_ref.dtype)


def _swa_attention(proj, sinks, q_off, k_off, v_off, g_off, n_heads, n_kv):
    b, s, _ = proj.shape
    group = n_heads // n_kv
    gw = group * HEAD_DIM
    n_sub = min(SWA_BLOCKS_PER_STEP, s // WINDOW)
    tq = n_sub * WINDOW
    assert s % tq == 0
    qb, gb = _col_block(q_off, gw), _col_block(g_off, gw)
    kb, vb = _col_block(k_off, HEAD_DIM), _col_block(v_off, HEAD_DIM)
    blocks = 3 * tq * gw * 2 + 2 * (tq + WINDOW) * HEAD_DIM * 2 + 8 * group * n_sub * WINDOW * 2 * WINDOW * 4

    def cur(col):
        return pl.BlockSpec((None, tq, HEAD_DIM), lambda bi, kv, n: (bi, n, col + kv))

    def prev(col):
        return pl.BlockSpec((None, WINDOW, HEAD_DIM), lambda bi, kv, n: (bi, jnp.maximum(n * n_sub - 1, 0), col + kv))

    return pl.pallas_call(
        functools.partial(_swa_kernel, group=group, n_sub=n_sub),
        grid=(b, n_kv, s // tq),
        in_specs=[pl.BlockSpec(memory_space=pltpu.SMEM),
                  pl.BlockSpec((None, tq, gw), lambda bi, kv, n: (bi, n, qb + kv)),
                  prev(kb), cur(kb), prev(vb), cur(vb),
                  pl.BlockSpec((None, tq, gw), lambda bi, kv, n: (bi, n, gb + kv))],
        out_specs=pl.BlockSpec((None, tq, gw), lambda bi, kv, n: (bi, n, kv)),
        out_shape=jax.ShapeDtypeStruct((b, s, n_heads * HEAD_DIM), BF16),
        compiler_params=_compiler_params(("parallel", "parallel", "arbitrary"), blocks),
        name="swa_attention",
    )(sinks, proj, proj, proj, proj, proj, proj)


def _conv_kernel(x_ref, xp_ref, c_ref, cp_ref, b_ref, g_ref, w_ref, o_ref):
    i = pl.program_id(1)
    h = c_ref[...].astype(F32) * x_ref[...].astype(F32)
    hp = cp_ref[...].astype(F32) * xp_ref[...].astype(F32)
    hp = jnp.where(i > 0, hp, 0.0)
    ext = jnp.concatenate([hp, h], axis=0)
    h1 = pltpu.roll(ext, 1, 0)[SUBLANES:]
    h2 = pltpu.roll(ext, 2, 0)[SUBLANES:]
    w = w_ref[...]
    y = w[0:1] * h2 + w[1:2] * h1 + w[2:3] * h
    y = b_ref[...].astype(F32) * y
    o_ref[...] = (y * _silu(g_ref[...].astype(F32))).astype(o_ref.dtype)


def _short_conv(proj, conv_w, x_off, b_off, c_off, g_off, width):
    b, s, _ = proj.shape
    tm = min(1024, s)
    cw = width // 2
    assert s % tm == 0 and tm % SUBLANES == 0 and cw % LANES == 0
    xb, bb, cb, gb = (_col_block(o, cw) for o in (x_off, b_off, c_off, g_off))
    rows_per_blk = tm // SUBLANES

    def cur(col):
        return pl.BlockSpec((None, tm, cw), lambda bi, i, c: (bi, i, col + c))

    def prev(col):
        return pl.BlockSpec((None, SUBLANES, cw),
                            lambda bi, i, c: (bi, jnp.maximum(i * rows_per_blk - 1, 0), col + c))

    blocks = 5 * tm * cw * 2 + 6 * tm * cw * 4
    return pl.pallas_call(
        _conv_kernel,
        grid=(b, s // tm, width // cw),
        in_specs=[cur(xb), prev(xb), cur(cb), prev(cb), cur(bb), cur(gb),
                  pl.BlockSpec((CONV_WIDTH, cw), lambda bi, i, c: (0, c))],
        out_specs=pl.BlockSpec((None, tm, cw), lambda bi, i, c: (bi, i, c)),
        out_shape=jax.ShapeDtypeStruct((b, s, width), BF16),
        compiler_params=_compiler_params(("parallel", "arbitrary", "arbitrary"), blocks),
        name="short_conv",
    )(proj, proj, proj, proj, proj, proj, conv_w)


def _gmlp_kernel(u_ref, v_ref, v0_ref, v1_ref, g_ref, vg_ref, ws_ref, bias_ref, o_ref, *, n_chunks, n_groups):
    v0 = v0_ref[...].astype(F32)
    v1 = v1_ref[...].astype(F32)
    full_width = v0.shape[1] + v1.shape[1]
    ms = (jnp.sum(v0 * v0, axis=-1, keepdims=True) + jnp.sum(v1 * v1, axis=-1, keepdims=True)) / full_width
    vn = (v_ref[...].astype(F32) * lax.rsqrt(ms + EPS) * vg_ref[...]).astype(BF16)
    r = lax.broadcasted_iota(jnp.int32, (GMLP_CHUNK, GMLP_CHUNK), 0)
    c = lax.broadcasted_iota(jnp.int32, (GMLP_CHUNK, GMLP_CHUNK), 1)
    causal = c <= r
    gd = GMLP_GROUP_DIM
    mixed = []
    for g in range(n_groups):
        wsg = jnp.where(causal, ws_ref[g], 0.0).astype(BF16)
        rhs = jnp.concatenate(
            [vn[ch * GMLP_CHUNK:(ch + 1) * GMLP_CHUNK, g * gd:(g + 1) * gd] for ch in range(n_chunks)], axis=1)
        mixed.append(jnp.dot(wsg, rhs, preferred_element_type=F32))
    bias = bias_ref[...]
    rows = []
    for ch in range(n_chunks):
        rows.append(jnp.concatenate([mixed[g][:, ch * gd:(ch + 1) * gd] for g in range(n_groups)], axis=1) + bias)
    mix = jnp.concatenate(rows, axis=0)
    y = u_ref[...].astype(F32) * mix
    o_ref[...] = (y * _silu(g_ref[...].astype(F32))).astype(o_ref.dtype)


def _spatial_gating(proj, v_gain, w_s, bias_tw, u_off, v_off, g_off, width):
    b, s, _ = proj.shape
    n_chunks = min(GMLP_CHUNKS_PER_STEP, s // GMLP_CHUNK)
    tc = n_chunks * GMLP_CHUNK
    cw = width // 2
    n_groups = w_s.shape[0]
    gpb = cw // GMLP_GROUP_DIM
    assert s % tc == 0 and cw % GMLP_GROUP_DIM == 0 and n_groups == 2 * gpb
    ub, vb, gb = (_col_block(o, cw) for o in (u_off, v_off, g_off))

    def cur(col):
        return pl.BlockSpec((None, tc, cw), lambda bi, i, c: (bi, i, col + c))

    def fixed(col):
        return pl.BlockSpec((None, tc, cw), lambda bi, i, c: (bi, i, col))

    blocks = 6 * tc * cw * 2 + gpb * GMLP_CHUNK * GMLP_CHUNK * 4 + 8 * tc * cw * 4
    return pl.pallas_call(
        functools.partial(_gmlp_kernel, n_chunks=n_chunks, n_groups=gpb),
        grid=(b, s // tc, width // cw),
        in_specs=[cur(ub), cur(vb), fixed(vb), fixed(vb + 1), cur(gb),
                  pl.BlockSpec((1, cw), lambda bi, i, c: (0, c)),
                  pl.BlockSpec((gpb, GMLP_CHUNK, GMLP_CHUNK), lambda bi, i, c: (c, 0, 0)),
                  pl.BlockSpec((GMLP_CHUNK, cw), lambda bi, i, c: (0, c))],
        out_specs=pl.BlockSpec((None, tc, cw), lambda bi, i, c: (bi, i, c)),
        out_shape=jax.ShapeDtypeStruct((b, s, width), BF16),
        compiler_params=_compiler_params(("parallel", "arbitrary", "arbitrary"), blocks),
        name="spatial_gating",
    )(proj, proj, proj, proj, proj, v_gain.reshape(1, width), w_s, bias_tw)


def _outproj_kernel(ya_ref, yb_ref, yc_ref, yd_ref, w_ref, x_ref, o_ref, *, width):
    acc = x_ref[...]
    for idx, y_ref in enumerate((ya_ref, yb_ref, yc_ref, yd_ref)):
        acc = acc + jnp.dot(y_ref[...], w_ref[idx * width:(idx + 1) * width, :], preferred_element_type=F32)
    o_ref[...] = acc


def _outproj(ys, w_all, layer, x):
    m, d = x.shape
    width = ys[0].shape[1]
    tm = min(1024, m)
    tn = min(1024, d)
    assert m % tm == 0 and d % tn == 0
    blocks = 4 * tm * width * 2 + 4 * width * tn * 2 + 3 * tm * tn * 4
    y_spec = pl.BlockSpec((tm, width), lambda i, j: (i, 0))
    return pl.pallas_call(
        functools.partial(_outproj_kernel, width=width),
        grid=(m // tm, d // tn),
        in_specs=[y_spec, y_spec, y_spec, y_spec,
                  pl.BlockSpec((None, 4 * width, tn), lambda i, j: (layer, 0, j)),
                  pl.BlockSpec((tm, tn), lambda i, j: (i, j))],
        out_specs=pl.BlockSpec((tm, tn), lambda i, j: (i, j)),
        out_shape=jax.ShapeDtypeStruct((m, d), F32),
        compiler_params=_compiler_params(("parallel", "arbitrary"), blocks),
        name="outproj",
    )(*ys, w_all, x)


def kernel(x, norm_gain, w_in, conv_w, gmlp_v_gain, gmlp_w_s, gmlp_b_s, swa_sinks, w_out, final_gain):
    batch, seq, d_model = x.shape
    depth = w_in.shape[0]
    m = batch * seq
    w_a = w_b = w_c = w_d = d_model // 4
    n_sb_heads = w_a // HEAD_DIM
    n_swa_heads = w_b // HEAD_DIM
    n_swa_kv = max(1, n_swa_heads // 4)
    w_kv = n_swa_kv * HEAD_DIM
    n_groups = gmlp_w_s.shape[1]
    assert n_swa_heads == swa_sinks.shape[1] and w_d == n_groups * GMLP_GROUP_DIM

    widths = (w_a, w_a, w_a, w_a, w_b, w_kv, w_kv, w_b, w_c, w_c, w_c, w_c, w_d, w_d, w_d)
    offs = [0]
    for wd in widths:
        offs.append(offs[-1] + wd)
    (qa, ka, va, ga, qb, kb, vb, gb, xc, bc, cc, gc, ud, vd, gd) = offs[:-1]
    n_proj = offs[-1]
    assert n_proj == w_in.shape[2]

    scale = HEAD_DIM ** -0.5
    col = jnp.arange(n_proj)
    col_scale = jnp.where((col >= qa) & (col < qa + w_a), scale * LOG2_E,
                          jnp.where((col >= qb) & (col < qb + w_b), scale, 1.0))
    col_scale = col_scale.astype(F32).reshape(1, n_proj)

    w_out_bf16 = w_out.astype(BF16)
    h = x.reshape(m, d_model)
    for l in range(depth):
        hn = _rmsnorm(h, norm_gain[l], BF16)
        proj = _inproj(hn, w_in, l, col_scale).reshape(batch, seq, n_proj)
        y_a = _sb_attention(proj, qa, ka, va, ga, n_sb_heads)
        y_b = _swa_attention(proj, swa_sinks[l], qb, kb, vb, gb, n_swa_heads, n_swa_kv)
        y_c = _short_conv(proj, conv_w[l], xc, bc, cc, gc, w_c)
        bias_tw = jnp.repeat(gmlp_b_s[l].T, GMLP_GROUP_DIM, axis=1)
        y_d = _spatial_gating(proj, gmlp_v_gain[l], gmlp_w_s[l], bias_tw, ud, vd, gd, w_d)
        ys = [y.reshape(m, -1) for y in (y_a, y_b, y_c, y_d)]
        h = _outproj(ys, w_out_bf16, l, h)
    out = _rmsnorm(h, final_gain, F32)
    return out.reshape(batch, seq, d_model)
```

```python
import functools

import jax
import jax.numpy as jnp
from jax import lax
from jax.experimental import pallas as pl
from jax.experimental.pallas import tpu as pltpu

HEAD_DIM = 128
GMLP_GROUP_DIM = 128
GMLP_CHUNK = 128
WINDOW = 128
CONV_WIDTH = 3
EPS = 1e-6
LOG2_E = 1.4426950408889634
EXP2_CLAMP = 64.0
F32_EXP2_UNDERFLOW = 160.0

LANES = 128
SUBLANES = 8
V7X_VMEM_BYTES = 64 * 1024 * 1024
VMEM_SPILL_ALLOWANCE = 6 * 1024 * 1024
MASKED_SCORE = -1e30
SB_HEADS_PER_STEP = 4
SWA_BLOCKS_PER_STEP = 4
GMLP_CHUNKS_PER_STEP = 8

F32 = jnp.float32
BF16 = jnp.bfloat16


def _compiler_params(semantics, block_bytes):
    limit = min(2 * block_bytes + VMEM_SPILL_ALLOWANCE, V7X_VMEM_BYTES - 4 * 1024 * 1024)
    return pltpu.CompilerParams(dimension_semantics=semantics, vmem_limit_bytes=int(limit))


def _col_block(offset, width):
    assert offset % width == 0, (offset, width)
    return offset // width


def _silu(g):
    return g * jax.nn.sigmoid(g)


def _rmsnorm_kernel(x_ref, g_ref, o_ref):
    x = x_ref[...]
    ms = jnp.mean(x * x, axis=-1, keepdims=True)
    o_ref[...] = (x * lax.rsqrt(ms + EPS) * g_ref[...]).astype(o_ref.dtype)


def _rmsnorm(x, gain, out_dtype):
    m, d = x.shape
    tm = min(256, m)
    return pl.pallas_call(
        _rmsnorm_kernel,
        grid=(m // tm,),
        in_specs=[pl.BlockSpec((tm, d), lambda i: (i, 0)),
                  pl.BlockSpec((1, d), lambda i: (0, 0))],
        out_specs=pl.BlockSpec((tm, d), lambda i: (i, 0)),
        out_shape=jax.ShapeDtypeStruct((m, d), out_dtype),
        compiler_params=_compiler_params(("parallel",), tm * d * 12),
        name="rmsnorm",
    )(x, gain.reshape(1, d))


def _inproj_kernel(h_ref, w_ref, s_ref, o_ref):
    acc = jnp.dot(h_ref[...], w_ref[...].astype(BF16), preferred_element_type=F32)
    o_ref[...] = (acc * s_ref[...]).astype(o_ref.dtype)


def _inproj(hn, w_all, layer, col_scale):
    m, d = hn.shape
    n = w_all.shape[2]
    tm = min(1024, m)
    tn = 512
    assert m % tm == 0 and n % tn == 0
    blocks = tm * d * 2 + d * tn * (4 + 2) + tm * tn * (2 + 4)
    return pl.pallas_call(
        _inproj_kernel,
        grid=(m // tm, n // tn),
        in_specs=[pl.BlockSpec((tm, d), lambda i, j: (i, 0)),
                  pl.BlockSpec((None, d, tn), lambda i, j: (layer, 0, j)),
                  pl.BlockSpec((1, tn), lambda i, j: (0, j))],
        out_specs=pl.BlockSpec((tm, tn), lambda i, j: (i, j)),
        out_shape=jax.ShapeDtypeStruct((m, n), BF16),
        compiler_params=_compiler_params(("parallel", "arbitrary"), blocks),
        name="inproj",
    )(hn, w_all, col_scale)


def _sb_attn_kernel(q_ref, k_ref, v_ref, g_ref, o_ref, z_scr, w_scr, carry_scr, acc_scr, *, tq, heads):
    i = pl.program_id(2)
    dh = HEAD_DIM
    hds = range(heads)
    row = lax.broadcasted_iota(jnp.int32, (tq, tq), 0)
    col = lax.broadcasted_iota(jnp.int32, (tq, tq), 1)
    strictly_causal = col < row
    later_keys = ((row > col) | (col == tq - 1)).astype(BF16)
    last_lane = lax.broadcasted_iota(jnp.int32, (tq, LANES), 1) == LANES - 1

    def key_rows(kb):
        return pl.ds(pl.multiple_of(kb * tq, tq), tq)

    def head_cols(hd):
        return slice(hd * dh, (hd + 1) * dh)

    def scores(kb, hd):
        return lax.dot_general(q_ref[:, head_cols(hd)], k_ref[key_rows(kb), head_cols(hd)],
                               (((1,), (1,)), ((), ())), preferred_element_type=F32)

    def softplus_terms(z, carry, diagonal):
        p = jnp.maximum(jnp.log2(1.0 + jnp.exp2(jnp.minimum(z, EXP2_CLAMP))), z)
        if diagonal:
            p = jnp.where(strictly_causal, p, 0.0)
        p = jnp.concatenate([p[:, :tq - LANES], p[:, tq - LANES:] + carry], axis=1)
        return p.astype(BF16), z - p

    def finish(u, cs, hd, diagonal):
        cs_tail = cs[:, tq - LANES:]
        carry_scr[hd] = jnp.where(last_lane, cs_tail, 0.0)
        cs = jnp.concatenate([cs[:, :tq - LANES], jnp.where(last_lane, 0.0, cs_tail)], axis=1)
        w = jnp.exp2(u - cs)
        if diagonal:
            w = jnp.where(strictly_causal, w, 0.0)
        w_scr[hd] = w.astype(BF16)

    def apply_weights(kb):
        for hd in hds:
            acc_scr[:, head_cols(hd)] += jnp.dot(w_scr[hd], v_ref[key_rows(kb), head_cols(hd)],
                                                 preferred_element_type=F32)

    acc_scr[...] = jnp.zeros_like(acc_scr)
    zs = [scores(i, hd) for hd in hds]
    zero_carry = jnp.zeros((tq, LANES), F32)
    terms = [softplus_terms(zs[hd], zero_carry, True) for hd in hds]
    nxt = jnp.maximum(i - 1, 0)
    for hd in hds:
        p_bf16, u = terms[hd]
        cs = jnp.dot(p_bf16, later_keys, preferred_element_type=F32)
        finish(u, cs, hd, True)
        z_scr[0, hd] = scores(nxt, hd)

    def step(kb, z_in, z_out):
        apply_weights(kb + 1)
        nxt = jnp.maximum(kb - 1, 0)
        for hd in hds:
            z_scr[z_out, hd] = scores(nxt, hd)
        terms = []
        for hd in hds:
            p_bf16, u = softplus_terms(z_scr[z_in, hd], carry_scr[hd], False)
            z_scr[z_in, hd] = u
            terms.append(p_bf16)
        for hd in hds:
            cs = jnp.dot(terms[hd], later_keys, preferred_element_type=F32)
            finish(z_scr[z_in, hd], cs, hd, False)

    def stick_used_up():
        swept = carry_scr[0]
        for hd in range(1, heads):
            swept = jnp.minimum(swept, carry_scr[hd])
        return jnp.max(jnp.min(swept, axis=0, keepdims=True)) >= F32_EXP2_UNDERFLOW

    n_pairs = i // 2

    def two_steps(state):
        m, _ = state
        kb = i - 1 - 2 * m
        step(kb, 0, 1)
        step(kb - 1, 1, 0)
        return m + 1, stick_used_up()

    pairs_done, used_up = lax.while_loop(lambda st: jnp.logical_and(st[0] < n_pairs, jnp.logical_not(st[1])),
                                         two_steps, (jnp.int32(0), stick_used_up()))
    odd_step = jnp.logical_and(jnp.logical_and(i % 2 == 1, pairs_done == n_pairs), jnp.logical_not(used_up))

    @pl.when(odd_step)
    def _():
        step(0, 0, 1)

    apply_weights(jnp.where(odd_step, 0, i - 2 * pairs_done))
    o_ref[...] = (acc_scr[...] * _silu(g_ref[...].astype(F32))).astype(o_ref.dtype)


def _sb_attention(proj, q_off, k_off, v_off, g_off, n_heads):
    b, s, _ = proj.shape
    tq = min(256, s)
    heads = min(SB_HEADS_PER_STEP, n_heads)
    assert s % tq == 0 and tq % LANES == 0 and n_heads % heads == 0
    gw = heads * HEAD_DIM
    qb, kb, vb, gb = (_col_block(o, gw) for o in (q_off, k_off, v_off, g_off))
    blocks = 3 * tq * gw * 2 + 2 * s * gw * 2 + heads * 8 * tq * tq * 4
    return pl.pallas_call(
        functools.partial(_sb_attn_kernel, tq=tq, heads=heads),
        grid=(b, n_heads // heads, s // tq),
        in_specs=[pl.BlockSpec((None, tq, gw), lambda bi, h, i: (bi, i, qb + h)),
                  pl.BlockSpec((None, s, gw), lambda bi, h, i: (bi, 0, kb + h)),
                  pl.BlockSpec((None, s, gw), lambda bi, h, i: (bi, 0, vb + h)),
                  pl.BlockSpec((None, tq, gw), lambda bi, h, i: (bi, i, gb + h))],
        out_specs=pl.BlockSpec((None, tq, gw), lambda bi, h, i: (bi, i, h)),
        out_shape=jax.ShapeDtypeStruct((b, s, n_heads * HEAD_DIM), BF16),
        scratch_shapes=[pltpu.VMEM((2, heads, tq, tq), F32),
                        pltpu.VMEM((heads, tq, tq), BF16),
                        pltpu.VMEM((heads, tq, LANES), F32),
                        pltpu.VMEM((tq, gw), F32)],
        compiler_params=_compiler_params(("parallel", "parallel", "arbitrary"), blocks),
        name="sb_attention",
    )(proj, proj, proj, proj)


def _swa_kernel(sink_ref, q_ref, kp_ref, kc_ref, vp_ref, vc_ref, g_ref, o_ref, *, group, n_sub):
    kv = pl.program_id(1)
    n = pl.program_id(2)
    k_all = jnp.concatenate([kp_ref[...], kc_ref[...]], axis=0)
    v_all = jnp.concatenate([vp_ref[...], vc_ref[...]], axis=0)
    r = lax.broadcasted_iota(jnp.int32, (WINDOW, 2 * WINDOW), 0)
    c = lax.broadcasted_iota(jnp.int32, (WINDOW, 2 * WINDOW), 1)
    in_window = (c > r) & (c <= r + WINDOW)
    for sub in range(n_sub):
        rows = slice(sub * WINDOW, (sub + 1) * WINDOW)
        k = k_all[sub * WINDOW:(sub + 2) * WINDOW]
        v = v_all[sub * WINDOW:(sub + 2) * WINDOW]
        valid = in_window & ((c >= WINDOW) | (n > 0)) if sub == 0 else in_window
        outs = []
        for g in range(group):
            qg = q_ref[rows, g * HEAD_DIM:(g + 1) * HEAD_DIM]
            s = lax.dot_general(qg, k, (((1,), (1,)), ((), ())), preferred_element_type=F32)
            s = jnp.where(valid, s, MASKED_SCORE)
            sink = sink_ref[kv * group + g]
            m = jnp.maximum(jnp.max(s, axis=-1, keepdims=True), sink)
            p = jnp.exp(s - m)
            denom = jnp.sum(p, axis=-1, keepdims=True) + jnp.exp(sink - m)
            outs.append(jnp.dot(p.astype(BF16), v, preferred_element_type=F32) / denom)
        o = jnp.concatenate(outs, axis=1)
        o_ref[rows, :] = (o * _silu(g_ref[rows, :].astype(F32))).astype(o_ref.dtype)


def _swa_attention(proj, sinks, q_off, k_off, v_off, g_off, n_heads, n_kv):
    b, s, _ = proj.shape
    group = n_heads // n_kv
    gw = group * HEAD_DIM
    n_sub = min(SWA_BLOCKS_PER_STEP, s // WINDOW)
    tq = n_sub * WINDOW
    assert s % tq == 0
    qb, gb = _col_block(q_off, gw), _col_block(g_off, gw)
    kb, vb = _col_block(k_off, HEAD_DIM), _col_block(v_off, HEAD_DIM)
    blocks = 3 * tq * gw * 2 + 2 * (tq + WINDOW) * HEAD_DIM * 2 + 8 * group * n_sub * WINDOW * 2 * WINDOW * 4

    def cur(col):
        return pl.BlockSpec((None, tq, HEAD_DIM), lambda bi, kv, n: (bi, n, col + kv))

    def prev(col):
        return pl.BlockSpec((None, WINDOW, HEAD_DIM), lambda bi, kv, n: (bi, jnp.maximum(n * n_sub - 1, 0), col + kv))

    return pl.pallas_call(
        functools.partial(_swa_kernel, group=group, n_sub=n_sub),
        grid=(b, n_kv, s // tq),
        in_specs=[pl.BlockSpec(memory_space=pltpu.SMEM),
                  pl.BlockSpec((None, tq, gw), lambda bi, kv, n: (bi, n, qb + kv)),
                  prev(kb), cur(kb), prev(vb), cur(vb),
                  pl.BlockSpec((None, tq, gw), lambda bi, kv, n: (bi, n, gb + kv))],
        out_specs=pl.BlockSpec((None, tq, gw), lambda bi, kv, n: (bi, n, kv)),
        out_shape=jax.ShapeDtypeStruct((b, s, n_heads * HEAD_DIM), BF16),
        compiler_params=_compiler_params(("parallel", "parallel", "arbitrary"), blocks),
        name="swa_attention",
    )(sinks, proj, proj, proj, proj, proj, proj)


def _conv_kernel(x_ref, xp_ref, c_ref, cp_ref, b_ref, g_ref, w_ref, o_ref):
    i = pl.program_id(1)
    h = c_ref[...].astype(F32) * x_ref[...].astype(F32)
    hp = cp_ref[...].astype(F32) * xp_ref[...].astype(F32)
    hp = jnp.where(i > 0, hp, 0.0)
    ext = jnp.concatenate([hp, h], axis=0)
    h1 = pltpu.roll(ext, 1, 0)[SUBLANES:]
    h2 = pltpu.roll(ext, 2, 0)[SUBLANES:]
    w = w_ref[...]
    y = w[0:1] * h2 + w[1:2] * h1 + w[2:3] * h
    y = b_ref[...].astype(F32) * y
    o_ref[...] = (y * _silu(g_ref[...].astype(F32))).astype(o_ref.dtype)


def _short_conv(proj, conv_w, x_off, b_off, c_off, g_off, width):
    b, s, _ = proj.shape
    tm = min(1024, s)
    cw = width // 2
    assert s % tm == 0 and tm % SUBLANES == 0 and cw % LANES == 0
    xb, bb, cb, gb = (_col_block(o, cw) for o in (x_off, b_off, c_off, g_off))
    rows_per_blk = tm // SUBLANES

    def cur(col):
        return pl.BlockSpec((None, tm, cw), lambda bi, i, c: (bi, i, col + c))

    def prev(col):
        return pl.BlockSpec((None, SUBLANES, cw),
                            lambda bi, i, c: (bi, jnp.maximum(i * rows_per_blk - 1, 0), col + c))

    blocks = 5 * tm * cw * 2 + 6 * tm * cw * 4
    return pl.pallas_call(
        _conv_kernel,
        grid=(b, s // tm, width // cw),
        in_specs=[cur(xb), prev(xb), cur(cb), prev(cb), cur(bb), cur(gb),
                  pl.BlockSpec((CONV_WIDTH, cw), lambda bi, i, c: (0, c))],
        out_specs=pl.BlockSpec((None, tm, cw), lambda bi, i, c: (bi, i, c)),
        out_shape=jax.ShapeDtypeStruct((b, s, width), BF16),
        compiler_params=_compiler_params(("parallel", "arbitrary", "arbitrary"), blocks),
        name="short_conv",
    )(proj, proj, proj, proj, proj, proj, conv_w)


def _gmlp_kernel(u_ref, v_ref, v0_ref, v1_ref, g_ref, vg_ref, ws_ref, bias_ref, o_ref, *, n_chunks, n_groups):
    v0 = v0_ref[...].astype(F32)
    v1 = v1_ref[...].astype(F32)
    full_width = v0.shape[1] + v1.shape[1]
    ms = (jnp.sum(v0 * v0, axis=-1, keepdims=True) + jnp.sum(v1 * v1, axis=-1, keepdims=True)) / full_width
    vn = (v_ref[...].astype(F32) * lax.rsqrt(ms + EPS) * vg_ref[...]).astype(BF16)
    r = lax.broadcasted_iota(jnp.int32, (GMLP_CHUNK, GMLP_CHUNK), 0)
    c = lax.broadcasted_iota(jnp.int32, (GMLP_CHUNK, GMLP_CHUNK), 1)
    causal = c <= r
    gd = GMLP_GROUP_DIM
    mixed = []
    for g in range(n_groups):
        wsg = jnp.where(causal, ws_ref[g], 0.0).astype(BF16)
        rhs = jnp.concatenate(
            [vn[ch * GMLP_CHUNK:(ch + 1) * GMLP_CHUNK, g * gd:(g + 1) * gd] for ch in range(n_chunks)], axis=1)
        mixed.append(jnp.dot(wsg, rhs, preferred_element_type=F32))
    bias = bias_ref[...]
    rows = []
    for ch in range(n_chunks):
        rows.append(jnp.concatenate([mixed[g][:, ch * gd:(ch + 1) * gd] for g in range(n_groups)], axis=1) + bias)
    mix = jnp.concatenate(rows, axis=0)
    y = u_ref[...].astype(F32) * mix
    o_ref[...] = (y * _silu(g_ref[...].astype(F32))).astype(o_ref.dtype)


def _spatial_gating(proj, v_gain, w_s, bias_tw, u_off, v_off, g_off, width):
    b, s, _ = proj.shape
    n_chunks = min(GMLP_CHUNKS_PER_STEP, s // GMLP_CHUNK)
    tc = n_chunks * GMLP_CHUNK
    cw = width // 2
    n_groups = w_s.shape[0]
    gpb = cw // GMLP_GROUP_DIM
    assert s % tc == 0 and cw % GMLP_GROUP_DIM == 0 and n_groups == 2 * gpb
    ub, vb, gb = (_col_block(o, cw) for o in (u_off, v_off, g_off))

    def cur(col):
        return pl.BlockSpec((None, tc, cw), lambda bi, i, c: (bi, i, col + c))

    def fixed(col):
        return pl.BlockSpec((None, tc, cw), lambda bi, i, c: (bi, i, col))

    blocks = 6 * tc * cw * 2 + gpb * GMLP_CHUNK * GMLP_CHUNK * 4 + 8 * tc * cw * 4
    return pl.pallas_call(
        functools.partial(_gmlp_kernel, n_chunks=n_chunks, n_groups=gpb),
        grid=(b, s // tc, width // cw),
        in_specs=[cur(ub), cur(vb), fixed(vb), fixed(vb + 1), cur(gb),
                  pl.BlockSpec((1, cw), lambda bi, i, c: (0, c)),
                  pl.BlockSpec((gpb, GMLP_CHUNK, GMLP_CHUNK), lambda bi, i, c: (c, 0, 0)),
                  pl.BlockSpec((GMLP_CHUNK, cw), lambda bi, i, c: (0, c))],
        out_specs=pl.BlockSpec((None, tc, cw), lambda bi, i, c: (bi, i, c)),
        out_shape=jax.ShapeDtypeStruct((b, s, width), BF16),
        compiler_params=_compiler_params(("parallel", "arbitrary", "arbitrary"), blocks),
        name="spatial_gating",
    )(proj, proj, proj, proj, proj, v_gain.reshape(1, width), w_s, bias_tw)


def _outproj_kernel(ya_ref, yb_ref, yc_ref, yd_ref, w_ref, x_ref, o_ref, *, width):
    acc = x_ref[...]
    for idx, y_ref in enumerate((ya_ref, yb_ref, yc_ref, yd_ref)):
        acc = acc + jnp.dot(y_ref[...], w_ref[idx * width:(idx + 1) * width, :], preferred_element_type=F32)
    o_ref[...] = acc


def _outproj(ys, w_all, layer, x):
    m, d = x.shape
    width = ys[0].shape[1]
    tm = min(1024, m)
    tn = min(1024, d)
    assert m % tm == 0 and d % tn == 0
    blocks = 4 * tm * width * 2 + 4 * width * tn * 2 + 3 * tm * tn * 4
    y_spec = pl.BlockSpec((tm, width), lambda i, j: (i, 0))
    return pl.pallas_call(
        functools.partial(_outproj_kernel, width=width),
        grid=(m // tm, d // tn),
        in_specs=[y_spec, y_spec, y_spec, y_spec,
                  pl.BlockSpec((None, 4 * width, tn), lambda i, j: (layer, 0, j)),
                  pl.BlockSpec((tm, tn), lambda i, j: (i, j))],
        out_specs=pl.BlockSpec((tm, tn), lambda i, j: (i, j)),
        out_shape=jax.ShapeDtypeStruct((m, d), F32),
        compiler_params=_compiler_params(("parallel", "arbitrary"), blocks),
        name="outproj",
    )(*ys, w_all, x)


def kernel(x, norm_gain, w_in, conv_w, gmlp_v_gain, gmlp_w_s, gmlp_b_s, swa_sinks, w_out, final_gain):
    batch, seq, d_model = x.shape
    depth = w_in.shape[0]
    m = batch * seq
    w_a = w_b = w_c = w_d = d_model // 4
    n_sb_heads = w_a // HEAD_DIM
    n_swa_heads = w_b // HEAD_DIM
    n_swa_kv = max(1, n_swa_heads // 4)
    w_kv = n_swa_kv * HEAD_DIM
    n_groups = gmlp_w_s.shape[1]
    assert n_swa_heads == swa_sinks.shape[1] and w_d == n_groups * GMLP_GROUP_DIM

    widths = (w_a, w_a, w_a, w_a, w_b, w_kv, w_kv, w_b, w_c, w_c, w_c, w_c, w_d, w_d, w_d)
    offs = [0]
    for wd in widths:
        offs.append(offs[-1] + wd)
    (qa, ka, va, ga, qb, kb, vb, gb, xc, bc, cc, gc, ud, vd, gd) = offs[:-1]
    n_proj = offs[-1]
    assert n_proj == w_in.shape[2]

    scale = HEAD_DIM ** -0.5
    col = jnp.arange(n_proj)
    col_scale = jnp.where((col >= qa) & (col < qa + w_a), scale * LOG2_E,
                          jnp.where((col >= qb) & (col < qb + w_b), scale, 1.0))
    col_scale = col_scale.astype(F32).reshape(1, n_proj)

    w_out_bf16 = w_out.astype(BF16)
    h = x.reshape(m, d_model)
    for l in range(depth):
        hn = _rmsnorm(h, norm_gain[l], BF16)
        proj = _inproj(hn, w_in, l, col_scale).reshape(batch, seq, n_proj)
        y_a = _sb_attention(proj, qa, ka, va, ga, n_sb_heads)
        y_b = _swa_attention(proj, swa_sinks[l], qb, kb, vb, gb, n_swa_heads, n_swa_kv)
        y_c = _short_conv(proj, conv_w[l], xc, bc, cc, gc, w_c)
        bias_tw = jnp.repeat(gmlp_b_s[l].T, GMLP_GROUP_DIM, axis=1)
        y_d = _spatial_gating(proj, gmlp_v_gain[l], gmlp_w_s[l], bias_tw, ud, vd, gd, w_d)
        ys = [y.reshape(m, -1) for y in (y_a, y_b, y_c, y_d)]
        h = _outproj(ys, w_out_bf16, l, h)
    out = _rmsnorm(h, final_gain, F32)
    return out.reshape(batch, seq, d_model)
```

```python
import functools

import jax
import jax.numpy as jnp
from jax import lax
from jax.experimental import pallas as pl
from jax.experimental.pallas import tpu as pltpu

HEAD_DIM = 128
GMLP_GROUP_DIM = 128
GMLP_CHUNK = 128
WINDOW = 128
CONV_WIDTH = 3
EPS = 1e-6
LOG2_E = 1.4426950408889634
EXP2_CLAMP = 64.0
F32_EXP2_UNDERFLOW = 160.0

LANES = 128
SUBLANES = 8
V7X_VMEM_BYTES = 64 * 1024 * 1024
VMEM_SPILL_ALLOWANCE = 6 * 1024 * 1024
MASKED_SCORE = -1e30
SB_HEADS_PER_STEP = 4
SWA_BLOCKS_PER_STEP = 4
GMLP_CHUNKS_PER_STEP = 8

F32 = jnp.float32
BF16 = jnp.bfloat16


def _compiler_params(semantics, block_bytes):
    limit = min(2 * block_bytes + VMEM_SPILL_ALLOWANCE, V7X_VMEM_BYTES - 4 * 1024 * 1024)
    return pltpu.CompilerParams(dimension_semantics=semantics, vmem_limit_bytes=int(limit))


def _col_block(offset, width):
    assert offset % width == 0, (offset, width)
    return offset // width


def _silu(g):
    return g * jax.nn.sigmoid(g)


def _rmsnorm_kernel(x_ref, g_ref, o_ref):
    x = x_ref[...]
    ms = jnp.mean(x * x, axis=-1, keepdims=True)
    o_ref[...] = (x * lax.rsqrt(ms + EPS) * g_ref[...]).astype(o_ref.dtype)


def _rmsnorm(x, gain, out_dtype):
    m, d = x.shape
    tm = min(256, m)
    return pl.pallas_call(
        _rmsnorm_kernel,
        grid=(m // tm,),
        in_specs=[pl.BlockSpec((tm, d), lambda i: (i, 0)),
                  pl.BlockSpec((1, d), lambda i: (0, 0))],
        out_specs=pl.BlockSpec((tm, d), lambda i: (i, 0)),
        out_shape=jax.ShapeDtypeStruct((m, d), out_dtype),
        compiler_params=_compiler_params(("parallel",), tm * d * 12),
        name="rmsnorm",
    )(x, gain.reshape(1, d))


def _inproj_kernel(h_ref, w_ref, s_ref, o_ref):
    acc = jnp.dot(h_ref[...], w_ref[...].astype(BF16), preferred_element_type=F32)
    o_ref[...] = (acc * s_ref[...]).astype(o_ref.dtype)


def _inproj(hn, w_all, layer, col_scale):
    m, d = hn.shape
    n = w_all.shape[2]
    tm = min(1024, m)
    tn = 512
    assert m % tm == 0 and n % tn == 0
    blocks = tm * d * 2 + d * tn * (4 + 2) + tm * tn * (2 + 4)
    return pl.pallas_call(
        _inproj_kernel,
        grid=(m // tm, n // tn),
        in_specs=[pl.BlockSpec((tm, d), lambda i, j: (i, 0)),
                  pl.BlockSpec((None, d, tn), lambda i, j: (layer, 0, j)),
                  pl.BlockSpec((1, tn), lambda i, j: (0, j))],
        out_specs=pl.BlockSpec((tm, tn), lambda i, j: (i, j)),
        out_shape=jax.ShapeDtypeStruct((m, n), BF16),
        compiler_params=_compiler_params(("parallel", "arbitrary"), blocks),
        name="inproj",
    )(hn, w_all, col_scale)


def _sb_attn_kernel(q_ref, k_ref, v_ref, g_ref, o_ref, z_scr, w_scr, carry_scr, acc_scr, *, tq, heads):
    i = pl.program_id(2)
    dh = HEAD_DIM
    hds = range(heads)
    row = lax.broadcasted_iota(jnp.int32, (tq, tq), 0)
    col = lax.broadcasted_iota(jnp.int32, (tq, tq), 1)
    strictly_causal = col < row
    later_keys = ((row > col) | (col == tq - 1)).astype(BF16)
    last_lane = lax.broadcasted_iota(jnp.int32, (tq, LANES), 1) == LANES - 1

    def key_rows(kb):
        return pl.ds(pl.multiple_of(kb * tq, tq), tq)

    def head_cols(hd):
        return slice(hd * dh, (hd + 1) * dh)

    def scores(kb, hd):
        return lax.dot_general(q_ref[:, head_cols(hd)], k_ref[key_rows(kb), head_cols(hd)],
                               (((1,), (1,)), ((), ())), preferred_element_type=F32)

    def softplus_terms(z, carry, diagonal):
        p = jnp.maximum(jnp.log2(1.0 + jnp.exp2(jnp.minimum(z, EXP2_CLAMP))), z)
        if diagonal:
            p = jnp.where(strictly_causal, p, 0.0)
        p = jnp.concatenate([p[:, :tq - LANES], p[:, tq - LANES:] + carry], axis=1)
        return p.astype(BF16), z - p

    def finish(u, cs, hd, diagonal):
        cs_tail = cs[:, tq - LANES:]
        carry_scr[hd] = jnp.where(last_lane, cs_tail, 0.0)
        cs = jnp.concatenate([cs[:, :tq - LANES], jnp.where(last_lane, 0.0, cs_tail)], axis=1)
        w = jnp.exp2(u - cs)
        if diagonal:
            w = jnp.where(strictly_causal, w, 0.0)
        w_scr[hd] = w.astype(BF16)

    def apply_weights(kb):
        for hd in hds:
            acc_scr[:, head_cols(hd)] += jnp.dot(w_scr[hd], v_ref[key_rows(kb), head_cols(hd)],
                                                 preferred_element_type=F32)

    acc_scr[...] = jnp.zeros_like(acc_scr)
    zs = [scores(i, hd) for hd in hds]
    zero_carry = jnp.zeros((tq, LANES), F32)
    terms = [softplus_terms(zs[hd], zero_carry, True) for hd in hds]
    nxt = jnp.maximum(i - 1, 0)
    for hd in hds:
        p_bf16, u = terms[hd]
        cs = jnp.dot(p_bf16, later_keys, preferred_element_type=F32)
        finish(u, cs, hd, True)
        z_scr[0, hd] = scores(nxt, hd)

    def step(kb, z_in, z_out):
        apply_weights(kb + 1)
        nxt = jnp.maximum(kb - 1, 0)
        for hd in hds:
            z_scr[z_out, hd] = scores(nxt, hd)
        terms = []
        for hd in hds:
            p_bf16, u = softplus_terms(z_scr[z_in, hd], carry_scr[hd], False)
            z_scr[z_in, hd] = u
            terms.append(p_bf16)
        for hd in hds:
            cs = jnp.dot(terms[hd], later_keys, preferred_element_type=F32)
            finish(z_scr[z_in, hd], cs, hd, False)

    def stick_used_up():
        swept = carry_scr[0]
        for hd in range(1, heads):
            swept = jnp.minimum(swept, carry_scr[hd])
        return jnp.max(jnp.min(swept, axis=0, keepdims=True)) >= F32_EXP2_UNDERFLOW

    n_pairs = i // 2

    def two_steps(state):
        m, _, _ = state
        kb = i - 1 - 2 * m
        step(kb, 0, 1)
        used_up = stick_used_up()

        @pl.when(jnp.logical_not(used_up))
        def _():
            step(kb - 1, 1, 0)

        return m + 1, jnp.where(used_up, kb, kb - 1), jnp.logical_or(used_up, stick_used_up())

    pairs_done, last_block, used_up = lax.while_loop(
        lambda st: jnp.logical_and(st[0] < n_pairs, jnp.logical_not(st[2])),
        two_steps, (jnp.int32(0), i, stick_used_up()))
    odd_step = jnp.logical_and(jnp.logical_and(i % 2 == 1, pairs_done == n_pairs), jnp.logical_not(used_up))

    @pl.when(odd_step)
    def _():
        step(0, 0, 1)

    apply_weights(jnp.where(odd_step, 0, last_block))
    o_ref[...] = (acc_scr[...] * _silu(g_ref[...].astype(F32))).astype(o_ref.dtype)


def _sb_attention(proj, q_off, k_off, v_off, g_off, n_heads):
    b, s, _ = proj.shape
    tq = min(256, s)
    heads = min(SB_HEADS_PER_STEP, n_heads)
    assert s % tq == 0 and tq % LANES == 0 and n_heads % heads == 0
    gw = heads * HEAD_DIM
    qb, kb, vb, gb = (_col_block(o, gw) for o in (q_off, k_off, v_off, g_off))
    blocks = 3 * tq * gw * 2 + 2 * s * gw * 2 + heads * 8 * tq * tq * 4
    return pl.pallas_call(
        functools.partial(_sb_attn_kernel, tq=tq, heads=heads),
        grid=(b, n_heads // heads, s // tq),
        in_specs=[pl.BlockSpec((None, tq, gw), lambda bi, h, i: (bi, i, qb + h)),
                  pl.BlockSpec((None, s, gw), lambda bi, h, i: (bi, 0, kb + h)),
                  pl.BlockSpec((None, s, gw), lambda bi, h, i: (bi, 0, vb + h)),
                  pl.BlockSpec((None, tq, gw), lambda bi, h, i: (bi, i, gb + h))],
        out_specs=pl.BlockSpec((None, tq, gw), lambda bi, h, i: (bi, i, h)),
        out_shape=jax.ShapeDtypeStruct((b, s, n_heads * HEAD_DIM), BF16),
        scratch_shapes=[pltpu.VMEM((2, heads, tq, tq), F32),
                        pltpu.VMEM((heads, tq, tq), BF16),
                        pltpu.VMEM((heads, tq, LANES), F32),
                        pltpu.VMEM((tq, gw), F32)],
        compiler_params=_compiler_params(("parallel", "parallel", "arbitrary"), blocks),
        name="sb_attention",
    )(proj, proj, proj, proj)


def _swa_kernel(sink_ref, q_ref, kp_ref, kc_ref, vp_ref, vc_ref, g_ref, o_ref, *, group, n_sub):
    kv = pl.program_id(1)
    n = pl.program_id(2)
    k_all = jnp.concatenate([kp_ref[...], kc_ref[...]], axis=0)
    v_all = jnp.concatenate([vp_ref[...], vc_ref[...]], axis=0)
    r = lax.broadcasted_iota(jnp.int32, (WINDOW, 2 * WINDOW), 0)
    c = lax.broadcasted_iota(jnp.int32, (WINDOW, 2 * WINDOW), 1)
    in_window = (c > r) & (c <= r + WINDOW)
    for sub in range(n_sub):
        rows = slice(sub * WINDOW, (sub + 1) * WINDOW)
        k = k_all[sub * WINDOW:(sub + 2) * WINDOW]
        v = v_all[sub * WINDOW:(sub + 2) * WINDOW]
        valid = in_window & ((c >= WINDOW) | (n > 0)) if sub == 0 else in_window
        outs = []
        for g in range(group):
            qg = q_ref[rows, g * HEAD_DIM:(g + 1) * HEAD_DIM]
            s = lax.dot_general(qg, k, (((1,), (1,)), ((), ())), preferred_element_type=F32)
            s = jnp.where(valid, s, MASKED_SCORE)
            sink = sink_ref[kv * group + g]
            m = jnp.maximum(jnp.max(s, axis=-1, keepdims=True), sink)
            p = jnp.exp(s - m)
            denom = jnp.sum(p, axis=-1, keepdims=True) + jnp.exp(sink - m)
            outs.append(jnp.dot(p.astype(BF16), v, preferred_element_type=F32) / denom)
        o = jnp.concatenate(outs, axis=1)
        o_ref[rows, :] = (o * _silu(g_ref[rows, :].astype(F32))).astype(o_ref.dtype)


def _swa_attention(proj, sinks, q_off, k_off, v_off, g_off, n_heads, n_kv):
    b, s, _ = proj.shape
    group = n_heads // n_kv
    gw = group * HEAD_DIM
    n_sub = min(SWA_BLOCKS_PER_STEP, s // WINDOW)
    tq = n_sub * WINDOW
    assert s % tq == 0
    qb, gb = _col_block(q_off, gw), _col_block(g_off, gw)
    kb, vb = _col_block(k_off, HEAD_DIM), _col_block(v_off, HEAD_DIM)
    blocks = 3 * tq * gw * 2 + 2 * (tq + WINDOW) * HEAD_DIM * 2 + 8 * group * n_sub * WINDOW * 2 * WINDOW * 4

    def cur(col):
        return pl.BlockSpec((None, tq, HEAD_DIM), lambda bi, kv, n: (bi, n, col + kv))

    def prev(col):
        return pl.BlockSpec((None, WINDOW, HEAD_DIM), lambda bi, kv, n: (bi, jnp.maximum(n * n_sub - 1, 0), col + kv))

    return pl.pallas_call(
        functools.partial(_swa_kernel, group=group, n_sub=n_sub),
        grid=(b, n_kv, s // tq),
        in_specs=[pl.BlockSpec(memory_space=pltpu.SMEM),
                  pl.BlockSpec((None, tq, gw), lambda bi, kv, n: (bi, n, qb + kv)),
                  prev(kb), cur(kb), prev(vb), cur(vb),
                  pl.BlockSpec((None, tq, gw), lambda bi, kv, n: (bi, n, gb + kv))],
        out_specs=pl.BlockSpec((None, tq, gw), lambda bi, kv, n: (bi, n, kv)),
        out_shape=jax.ShapeDtypeStruct((b, s, n_heads * HEAD_DIM), BF16),
        compiler_params=_compiler_params(("parallel", "parallel", "arbitrary"), blocks),
        name="swa_attention",
    )(sinks, proj, proj, proj, proj, proj, proj)


def _conv_kernel(x_ref, xp_ref, c_ref, cp_ref, b_ref, g_ref, w_ref, o_ref):
    i = pl.program_id(1)
    h = c_ref[...].astype(F32) * x_ref[...].astype(F32)
    hp = cp_ref[...].astype(F32) * xp_ref[...].astype(F32)
    hp = jnp.where(i > 0, hp, 0.0)
    ext = jnp.concatenate([hp, h], axis=0)
    h1 = pltpu.roll(ext, 1, 0)[SUBLANES:]
    h2 = pltpu.roll(ext, 2, 0)[SUBLANES:]
    w = w_ref[...]
    y = w[0:1] * h2 + w[1:2] * h1 + w[2:3] * h
    y = b_ref[...].astype(F32) * y
    o_ref[...] = (y * _silu(g_ref[...].astype(F32))).astype(o_ref.dtype)


def _short_conv(proj, conv_w, x_off, b_off, c_off, g_off, width):
    b, s, _ = proj.shape
    tm = min(1024, s)
    cw = width // 2
    assert s % tm == 0 and tm % SUBLANES == 0 and cw % LANES == 0
    xb, bb, cb, gb = (_col_block(o, cw) for o in (x_off, b_off, c_off, g_off))
    rows_per_blk = tm // SUBLANES

    def cur(col):
        return pl.BlockSpec((None, tm, cw), lambda bi, i, c: (bi, i, col + c))

    def prev(col):
        return pl.BlockSpec((None, SUBLANES, cw),
                            lambda bi, i, c: (bi, jnp.maximum(i * rows_per_blk - 1, 0), col + c))

    blocks = 5 * tm * cw * 2 + 6 * tm * cw * 4
    return pl.pallas_call(
        _conv_kernel,
        grid=(b, s // tm, width // cw),
        in_specs=[cur(xb), prev(xb), cur(cb), prev(cb), cur(bb), cur(gb),
                  pl.BlockSpec((CONV_WIDTH, cw), lambda bi, i, c: (0, c))],
        out_specs=pl.BlockSpec((None, tm, cw), lambda bi, i, c: (bi, i, c)),
        out_shape=jax.ShapeDtypeStruct((b, s, width), BF16),
        compiler_params=_compiler_params(("parallel", "arbitrary", "arbitrary"), blocks),
        name="short_conv",
    )(proj, proj, proj, proj, proj, proj, conv_w)


def _gmlp_kernel(u_ref, v_ref, v0_ref, v1_ref, g_ref, vg_ref, ws_ref, bias_ref, o_ref, *, n_chunks, n_groups):
    v0 = v0_ref[...].astype(F32)
    v1 = v1_ref[...].astype(F32)
    full_width = v0.shape[1] + v1.shape[1]
    ms = (jnp.sum(v0 * v0, axis=-1, keepdims=True) + jnp.sum(v1 * v1, axis=-1, keepdims=True)) / full_width
    vn = (v_ref[...].astype(F32) * lax.rsqrt(ms + EPS) * vg_ref[...]).astype(BF16)
    r = lax.broadcasted_iota(jnp.int32, (GMLP_CHUNK, GMLP_CHUNK), 0)
    c = lax.broadcasted_iota(jnp.int32, (GMLP_CHUNK, GMLP_CHUNK), 1)
    causal = c <= r
    gd = GMLP_GROUP_DIM
    mixed = []
    for g in range(n_groups):
        wsg = jnp.where(causal, ws_ref[g], 0.0).astype(BF16)
        rhs = jnp.concatenate(
            [vn[ch * GMLP_CHUNK:(ch + 1) * GMLP_CHUNK, g * gd:(g + 1) * gd] for ch in range(n_chunks)], axis=1)
        mixed.append(jnp.dot(wsg, rhs, preferred_element_type=F32))
    bias = bias_ref[...]
    rows = []
    for ch in range(n_chunks):
        rows.append(jnp.concatenate([mixed[g][:, ch * gd:(ch + 1) * gd] for g in range(n_groups)], axis=1) + bias)
    mix = jnp.concatenate(rows, axis=0)
    y = u_ref[...].astype(F32) * mix
    o_ref[...] = (y * _silu(g_ref[...].astype(F32))).astype(o_ref.dtype)


def _spatial_gating(proj, v_gain, w_s, bias_tw, u_off, v_off, g_off, width):
    b, s, _ = proj.shape
    n_chunks = min(GMLP_CHUNKS_PER_STEP, s // GMLP_CHUNK)
    tc = n_chunks * GMLP_CHUNK
    cw = width // 2
    n_groups = w_s.shape[0]
    gpb = cw // GMLP_GROUP_DIM
    assert s % tc == 0 and cw % GMLP_GROUP_DIM == 0 and n_groups == 2 * gpb
    ub, vb, gb = (_col_block(o, cw) for o in (u_off, v_off, g_off))

    def cur(col):
        return pl.BlockSpec((None, tc, cw), lambda bi, i, c: (bi, i, col + c))

    def fixed(col):
        return pl.BlockSpec((None, tc, cw), lambda bi, i, c: (bi, i, col))

    blocks = 6 * tc * cw * 2 + gpb * GMLP_CHUNK * GMLP_CHUNK * 4 + 8 * tc * cw * 4
    return pl.pallas_call(
        functools.partial(_gmlp_kernel, n_chunks=n_chunks, n_groups=gpb),
        grid=(b, s // tc, width // cw),
        in_specs=[cur(ub), cur(vb), fixed(vb), fixed(vb + 1), cur(gb),
                  pl.BlockSpec((1, cw), lambda bi, i, c: (0, c)),
                  pl.BlockSpec((gpb, GMLP_CHUNK, GMLP_CHUNK), lambda bi, i, c: (c, 0, 0)),
                  pl.BlockSpec((GMLP_CHUNK, cw), lambda bi, i, c: (0, c))],
        out_specs=pl.BlockSpec((None, tc, cw), lambda bi, i, c: (bi, i, c)),
        out_shape=jax.ShapeDtypeStruct((b, s, width), BF16),
        compiler_params=_compiler_params(("parallel", "arbitrary", "arbitrary"), blocks),
        name="spatial_gating",
    )(proj, proj, proj, proj, proj, v_gain.reshape(1, width), w_s, bias_tw)


def _outproj_kernel(ya_ref, yb_ref, yc_ref, yd_ref, w_ref, x_ref, o_ref, *, width):
    acc = x_ref[...]
    for idx, y_ref in enumerate((ya_ref, yb_ref, yc_ref, yd_ref)):
        acc = acc + jnp.dot(y_ref[...], w_ref[idx * width:(idx + 1) * width, :], preferred_element_type=F32)
    o_ref[...] = acc


def _outproj(ys, w_all, layer, x):
    m, d = x.shape
    width = ys[0].shape[1]
    tm = min(1024, m)
    tn = min(1024, d)
    assert m % tm == 0 and d % tn == 0
    blocks = 4 * tm * width * 2 + 4 * width * tn * 2 + 3 * tm * tn * 4
    y_spec = pl.BlockSpec((tm, width), lambda i, j: (i, 0))
    return pl.pallas_call(
        functools.partial(_outproj_kernel, width=width),
        grid=(m // tm, d // tn),
        in_specs=[y_spec, y_spec, y_spec, y_spec,
                  pl.BlockSpec((None, 4 * width, tn), lambda i, j: (layer, 0, j)),
                  pl.BlockSpec((tm, tn), lambda i, j: (i, j))],
        out_specs=pl.BlockSpec((tm, tn), lambda i, j: (i, j)),
        out_shape=jax.ShapeDtypeStruct((m, d), F32),
        compiler_params=_compiler_params(("parallel", "arbitrary"), blocks),
        name="outproj",
    )(*ys, w_all, x)


def kernel(x, norm_gain, w_in, conv_w, gmlp_v_gain, gmlp_w_s, gmlp_b_s, swa_sinks, w_out, final_gain):
    batch, seq, d_model = x.shape
    depth = w_in.shape[0]
    m = batch * seq
    w_a = w_b = w_c = w_d = d_model // 4
    n_sb_heads = w_a // HEAD_DIM
    n_swa_heads = w_b // HEAD_DIM
    n_swa_kv = max(1, n_swa_heads // 4)
    w_kv = n_swa_kv * HEAD_DIM
    n_groups = gmlp_w_s.shape[1]
    assert n_swa_heads == swa_sinks.shape[1] and w_d == n_groups * GMLP_GROUP_DIM

    widths = (w_a, w_a, w_a, w_a, w_b, w_kv, w_kv, w_b, w_c, w_c, w_c, w_c, w_d, w_d, w_d)
    offs = [0]
    for wd in widths:
        offs.append(offs[-1] + wd)
    (qa, ka, va, ga, qb, kb, vb, gb, xc, bc, cc, gc, ud, vd, gd) = offs[:-1]
    n_proj = offs[-1]
    assert n_proj == w_in.shape[2]

    scale = HEAD_DIM ** -0.5
    col = jnp.arange(n_proj)
    col_scale = jnp.where((col >= qa) & (col < qa + w_a), scale * LOG2_E,
                          jnp.where((col >= qb) & (col < qb + w_b), scale, 1.0))
    col_scale = col_scale.astype(F32).reshape(1, n_proj)

    w_out_bf16 = w_out.astype(BF16)
    h = x.reshape(m, d_model)
    for l in range(depth):
        hn = _rmsnorm(h, norm_gain[l], BF16)
        proj = _inproj(hn, w_in, l, col_scale).reshape(batch, seq, n_proj)
        y_a = _sb_attention(proj, qa, ka, va, ga, n_sb_heads)
        y_b = _swa_attention(proj, swa_sinks[l], qb, kb, vb, gb, n_swa_heads, n_swa_kv)
        y_c = _short_conv(proj, conv_w[l], xc, bc, cc, gc, w_c)
        bias_tw = jnp.repeat(gmlp_b_s[l].T, GMLP_GROUP_DIM, axis=1)
        y_d = _spatial_gating(proj, gmlp_v_gain[l], gmlp_w_s[l], bias_tw, ud, vd, gd, w_d)
        ys = [y.reshape(m, -1) for y in (y_a, y_b, y_c, y_d)]
        h = _outproj(ys, w_out_bf16, l, h)
    out = _rmsnorm(h, final_gain, F32)
    return out.reshape(batch, seq, d_model)
```

```python
import functools

import jax
import jax.numpy as jnp
from jax import lax
from jax.experimental import pallas as pl
from jax.experimental.pallas import tpu as pltpu

HEAD_DIM = 128
GMLP_GROUP_DIM = 128
GMLP_CHUNK = 128
WINDOW = 128
CONV_WIDTH = 3
EPS = 1e-6
LOG2_E = 1.4426950408889634
EXP2_CLAMP = 64.0
F32_EXP2_UNDERFLOW = 160.0

LANES = 128
SUBLANES = 8
V7X_VMEM_BYTES = 64 * 1024 * 1024
VMEM_SPILL_ALLOWANCE = 6 * 1024 * 1024
MASKED_SCORE = -1e30
SB_HEADS_PER_STEP = 4
SWA_BLOCKS_PER_STEP = 4
GMLP_CHUNKS_PER_STEP = 8

F32 = jnp.float32
BF16 = jnp.bfloat16


def _compiler_params(semantics, block_bytes):
    limit = min(2 * block_bytes + VMEM_SPILL_ALLOWANCE, V7X_VMEM_BYTES - 4 * 1024 * 1024)
    return pltpu.CompilerParams(dimension_semantics=semantics, vmem_limit_bytes=int(limit))


def _col_block(offset, width):
    assert offset % width == 0, (offset, width)
    return offset // width


def _silu(g):
    return g * jax.nn.sigmoid(g)


def _rmsnorm_kernel(x_ref, g_ref, o_ref):
    x = x_ref[...]
    ms = jnp.mean(x * x, axis=-1, keepdims=True)
    o_ref[...] = (x * lax.rsqrt(ms + EPS) * g_ref[...]).astype(o_ref.dtype)


def _rmsnorm(x, gain, out_dtype):
    m, d = x.shape
    tm = min(256, m)
    return pl.pallas_call(
        _rmsnorm_kernel,
        grid=(m // tm,),
        in_specs=[pl.BlockSpec((tm, d), lambda i: (i, 0)),
                  pl.BlockSpec((1, d), lambda i: (0, 0))],
        out_specs=pl.BlockSpec((tm, d), lambda i: (i, 0)),
        out_shape=jax.ShapeDtypeStruct((m, d), out_dtype),
        compiler_params=_compiler_params(("parallel",), tm * d * 12),
        name="rmsnorm",
    )(x, gain.reshape(1, d))


def _inproj_kernel(h_ref, w_ref, s_ref, o_ref):
    acc = jnp.dot(h_ref[...], w_ref[...].astype(BF16), preferred_element_type=F32)
    o_ref[...] = (acc * s_ref[...]).astype(o_ref.dtype)


def _inproj(hn, w_all, layer, col_scale):
    m, d = hn.shape
    n = w_all.shape[2]
    tm = min(1024, m)
    tn = 768 if n % 768 == 0 else 512
    assert m % tm == 0 and n % tn == 0
    blocks = tm * d * 2 + d * tn * (4 + 2) + tm * tn * (2 + 4)
    return pl.pallas_call(
        _inproj_kernel,
        grid=(m // tm, n // tn),
        in_specs=[pl.BlockSpec((tm, d), lambda i, j: (i, 0)),
                  pl.BlockSpec((None, d, tn), lambda i, j: (layer, 0, j)),
                  pl.BlockSpec((1, tn), lambda i, j: (0, j))],
        out_specs=pl.BlockSpec((tm, tn), lambda i, j: (i, j)),
        out_shape=jax.ShapeDtypeStruct((m, n), BF16),
        compiler_params=_compiler_params(("parallel", "arbitrary"), blocks),
        name="inproj",
    )(hn, w_all, col_scale)


def _sb_attn_kernel(q_ref, k_ref, v_ref, g_ref, o_ref, z_scr, w_scr, carry_scr, acc_scr, *, tq, heads):
    i = pl.program_id(2)
    dh = HEAD_DIM
    hds = range(heads)
    row = lax.broadcasted_iota(jnp.int32, (tq, tq), 0)
    col = lax.broadcasted_iota(jnp.int32, (tq, tq), 1)
    strictly_causal = col < row
    later_keys = ((row > col) | (col == tq - 1)).astype(BF16)
    last_lane = lax.broadcasted_iota(jnp.int32, (tq, LANES), 1) == LANES - 1

    def key_rows(kb):
        return pl.ds(pl.multiple_of(kb * tq, tq), tq)

    def head_cols(hd):
        return slice(hd * dh, (hd + 1) * dh)

    def scores(kb, hd):
        return lax.dot_general(q_ref[:, head_cols(hd)], k_ref[key_rows(kb), head_cols(hd)],
                               (((1,), (1,)), ((), ())), preferred_element_type=F32)

    def softplus_terms(z, carry, diagonal):
        p = jnp.maximum(jnp.log2(1.0 + jnp.exp2(jnp.minimum(z, EXP2_CLAMP))), z)
        if diagonal:
            p = jnp.where(strictly_causal, p, 0.0)
        p = jnp.concatenate([p[:, :tq - LANES], p[:, tq - LANES:] + carry], axis=1)
        return p.astype(BF16), z - p

    def finish(u, cs, hd, diagonal):
        cs_tail = cs[:, tq - LANES:]
        carry_scr[hd] = jnp.where(last_lane, cs_tail, 0.0)
        cs = jnp.concatenate([cs[:, :tq - LANES], jnp.where(last_lane, 0.0, cs_tail)], axis=1)
        w = jnp.exp2(u - cs)
        if diagonal:
            w = jnp.where(strictly_causal, w, 0.0)
        w_scr[hd] = w.astype(BF16)

    def apply_weights(kb):
        for hd in hds:
            acc_scr[:, head_cols(hd)] += jnp.dot(w_scr[hd], v_ref[key_rows(kb), head_cols(hd)],
                                                 preferred_element_type=F32)

    acc_scr[...] = jnp.zeros_like(acc_scr)
    zs = [scores(i, hd) for hd in hds]
    zero_carry = jnp.zeros((tq, LANES), F32)
    terms = [softplus_terms(zs[hd], zero_carry, True) for hd in hds]
    nxt = jnp.maximum(i - 1, 0)
    for hd in hds:
        p_bf16, u = terms[hd]
        cs = jnp.dot(p_bf16, later_keys, preferred_element_type=F32)
        finish(u, cs, hd, True)
        z_scr[0, hd] = scores(nxt, hd)

    def step(kb, z_in, z_out):
        apply_weights(kb + 1)
        nxt = jnp.maximum(kb - 1, 0)
        for hd in hds:
            z_scr[z_out, hd] = scores(nxt, hd)
        terms = []
        for hd in hds:
            p_bf16, u = softplus_terms(z_scr[z_in, hd], carry_scr[hd], False)
            z_scr[z_in, hd] = u
            terms.append(p_bf16)
        for hd in hds:
            cs = jnp.dot(terms[hd], later_keys, preferred_element_type=F32)
            finish(z_scr[z_in, hd], cs, hd, False)

    def stick_used_up():
        swept = carry_scr[0]
        for hd in range(1, heads):
            swept = jnp.minimum(swept, carry_scr[hd])
        return jnp.max(jnp.min(swept, axis=0, keepdims=True)) >= F32_EXP2_UNDERFLOW

    n_pairs = i // 2

    def two_steps(state):
        m, _, _ = state
        kb = i - 1 - 2 * m
        step(kb, 0, 1)
        used_up = stick_used_up()

        @pl.when(jnp.logical_not(used_up))
        def _():
            step(kb - 1, 1, 0)

        return m + 1, jnp.where(used_up, kb, kb - 1), jnp.logical_or(used_up, stick_used_up())

    pairs_done, last_block, used_up = lax.while_loop(
        lambda st: jnp.logical_and(st[0] < n_pairs, jnp.logical_not(st[2])),
        two_steps, (jnp.int32(0), i, stick_used_up()))
    odd_step = jnp.logical_and(jnp.logical_and(i % 2 == 1, pairs_done == n_pairs), jnp.logical_not(used_up))

    @pl.when(odd_step)
    def _():
        step(0, 0, 1)

    apply_weights(jnp.where(odd_step, 0, last_block))
    o_ref[...] = (acc_scr[...] * _silu(g_ref[...].astype(F32))).astype(o_ref.dtype)


def _sb_attention(proj, q_off, k_off, v_off, g_off, n_heads):
    b, s, _ = proj.shape
    tq = min(256, s)
    heads = min(SB_HEADS_PER_STEP, n_heads)
    assert s % tq == 0 and tq % LANES == 0 and n_heads % heads == 0
    gw = heads * HEAD_DIM
    qb, kb, vb, gb = (_col_block(o, gw) for o in (q_off, k_off, v_off, g_off))
    blocks = 3 * tq * gw * 2 + 2 * s * gw * 2 + heads * 8 * tq * tq * 4
    return pl.pallas_call(
        functools.partial(_sb_attn_kernel, tq=tq, heads=heads),
        grid=(b, n_heads // heads, s // tq),
        in_specs=[pl.BlockSpec((None, tq, gw), lambda bi, h, i: (bi, i, qb + h)),
                  pl.BlockSpec((None, s, gw), lambda bi, h, i: (bi, 0, kb + h)),
                  pl.BlockSpec((None, s, gw), lambda bi, h, i: (bi, 0, vb + h)),
                  pl.BlockSpec((None, tq, gw), lambda bi, h, i: (bi, i, gb + h))],
        out_specs=pl.BlockSpec((None, tq, gw), lambda bi, h, i: (bi, i, h)),
        out_shape=jax.ShapeDtypeStruct((b, s, n_heads * HEAD_DIM), BF16),
        scratch_shapes=[pltpu.VMEM((2, heads, tq, tq), F32),
                        pltpu.VMEM((heads, tq, tq), BF16),
                        pltpu.VMEM((heads, tq, LANES), F32),
                        pltpu.VMEM((tq, gw), F32)],
        compiler_params=_compiler_params(("parallel", "parallel", "arbitrary"), blocks),
        name="sb_attention",
    )(proj, proj, proj, proj)


def _swa_kernel(sink_ref, q_ref, kp_ref, kc_ref, vp_ref, vc_ref, g_ref, o_ref, *, group, n_sub):
    kv = pl.program_id(1)
    n = pl.program_id(2)
    k_all = jnp.concatenate([kp_ref[...], kc_ref[...]], axis=0)
    v_all = jnp.concatenate([vp_ref[...], vc_ref[...]], axis=0)
    r = lax.broadcasted_iota(jnp.int32, (WINDOW, 2 * WINDOW), 0)
    c = lax.broadcasted_iota(jnp.int32, (WINDOW, 2 * WINDOW), 1)
    in_window = (c > r) & (c <= r + WINDOW)
    for sub in range(n_sub):
        rows = slice(sub * WINDOW, (sub + 1) * WINDOW)
        k = k_all[sub * WINDOW:(sub + 2) * WINDOW]
        v = v_all[sub * WINDOW:(sub + 2) * WINDOW]
        valid = in_window & ((c >= WINDOW) | (n > 0)) if sub == 0 else in_window
        outs = []
        for g in range(group):
            qg = q_ref[rows, g * HEAD_DIM:(g + 1) * HEAD_DIM]
            s = lax.dot_general(qg, k, (((1,), (1,)), ((), ())), preferred_element_type=F32)
            s = jnp.where(valid, s, MASKED_SCORE)
            sink = sink_ref[kv * group + g]
            m = jnp.maximum(jnp.max(s, axis=-1, keepdims=True), sink)
            p = jnp.exp(s - m)
            denom = jnp.sum(p, axis=-1, keepdims=True) + jnp.exp(sink - m)
            outs.append(jnp.dot(p.astype(BF16), v, preferred_element_type=F32) / denom)
        o = jnp.concatenate(outs, axis=1)
        o_ref[rows, :] = (o * _silu(g_ref[rows, :].astype(F32))).astype(o_ref.dtype)


def _swa_attention(proj, sinks, q_off, k_off, v_off, g_off, n_heads, n_kv):
    b, s, _ = proj.shape
    group = n_heads // n_kv
    gw = group * HEAD_DIM
    n_sub = min(SWA_BLOCKS_PER_STEP, s // WINDOW)
    tq = n_sub * WINDOW
    assert s % tq == 0
    qb, gb = _col_block(q_off, gw), _col_block(g_off, gw)
    kb, vb = _col_block(k_off, HEAD_DIM), _col_block(v_off, HEAD_DIM)
    blocks = 3 * tq * gw * 2 + 2 * (tq + WINDOW) * HEAD_DIM * 2 + 8 * group * n_sub * WINDOW * 2 * WINDOW * 4

    def cur(col):
        return pl.BlockSpec((None, tq, HEAD_DIM), lambda bi, kv, n: (bi, n, col + kv))

    def prev(col):
        return pl.BlockSpec((None, WINDOW, HEAD_DIM), lambda bi, kv, n: (bi, jnp.maximum(n * n_sub - 1, 0), col + kv))

    return pl.pallas_call(
        functools.partial(_swa_kernel, group=group, n_sub=n_sub),
        grid=(b, n_kv, s // tq),
        in_specs=[pl.BlockSpec(memory_space=pltpu.SMEM),
                  pl.BlockSpec((None, tq, gw), lambda bi, kv, n: (bi, n, qb + kv)),
                  prev(kb), cur(kb), prev(vb), cur(vb),
                  pl.BlockSpec((None, tq, gw), lambda bi, kv, n: (bi, n, gb + kv))],
        out_specs=pl.BlockSpec((None, tq, gw), lambda bi, kv, n: (bi, n, kv)),
        out_shape=jax.ShapeDtypeStruct((b, s, n_heads * HEAD_DIM), BF16),
        compiler_params=_compiler_params(("parallel", "parallel", "arbitrary"), blocks),
        name="swa_attention",
    )(sinks, proj, proj, proj, proj, proj, proj)


def _conv_kernel(x_ref, xp_ref, c_ref, cp_ref, b_ref, g_ref, w_ref, o_ref):
    i = pl.program_id(1)
    h = c_ref[...].astype(F32) * x_ref[...].astype(F32)
    hp = cp_ref[...].astype(F32) * xp_ref[...].astype(F32)
    hp = jnp.where(i > 0, hp, 0.0)
    ext = jnp.concatenate([hp, h], axis=0)
    h1 = pltpu.roll(ext, 1, 0)[SUBLANES:]
    h2 = pltpu.roll(ext, 2, 0)[SUBLANES:]
    w = w_ref[...]
    y = w[0:1] * h2 + w[1:2] * h1 + w[2:3] * h
    y = b_ref[...].astype(F32) * y
    o_ref[...] = (y * _silu(g_ref[...].astype(F32))).astype(o_ref.dtype)


def _short_conv(proj, conv_w, x_off, b_off, c_off, g_off, width):
    b, s, _ = proj.shape
    tm = min(1024, s)
    cw = width // 2
    assert s % tm == 0 and tm % SUBLANES == 0 and cw % LANES == 0
    xb, bb, cb, gb = (_col_block(o, cw) for o in (x_off, b_off, c_off, g_off))
    rows_per_blk = tm // SUBLANES

    def cur(col):
        return pl.BlockSpec((None, tm, cw), lambda bi, i, c: (bi, i, col + c))

    def prev(col):
        return pl.BlockSpec((None, SUBLANES, cw),
                            lambda bi, i, c: (bi, jnp.maximum(i * rows_per_blk - 1, 0), col + c))

    blocks = 5 * tm * cw * 2 + 6 * tm * cw * 4
    return pl.pallas_call(
        _conv_kernel,
        grid=(b, s // tm, width // cw),
        in_specs=[cur(xb), prev(xb), cur(cb), prev(cb), cur(bb), cur(gb),
                  pl.BlockSpec((CONV_WIDTH, cw), lambda bi, i, c: (0, c))],
        out_specs=pl.BlockSpec((None, tm, cw), lambda bi, i, c: (bi, i, c)),
        out_shape=jax.ShapeDtypeStruct((b, s, width), BF16),
        compiler_params=_compiler_params(("parallel", "arbitrary", "arbitrary"), blocks),
        name="short_conv",
    )(proj, proj, proj, proj, proj, proj, conv_w)


def _gmlp_kernel(u_ref, v_ref, v0_ref, v1_ref, g_ref, vg_ref, ws_ref, bias_ref, o_ref, *, n_chunks, n_groups):
    v0 = v0_ref[...].astype(F32)
    v1 = v1_ref[...].astype(F32)
    full_width = v0.shape[1] + v1.shape[1]
    ms = (jnp.sum(v0 * v0, axis=-1, keepdims=True) + jnp.sum(v1 * v1, axis=-1, keepdims=True)) / full_width
    vn = (v_ref[...].astype(F32) * lax.rsqrt(ms + EPS) * vg_ref[...]).astype(BF16)
    r = lax.broadcasted_iota(jnp.int32, (GMLP_CHUNK, GMLP_CHUNK), 0)
    c = lax.broadcasted_iota(jnp.int32, (GMLP_CHUNK, GMLP_CHUNK), 1)
    causal = c <= r
    gd = GMLP_GROUP_DIM
    mixed = []
    for g in range(n_groups):
        wsg = jnp.where(causal, ws_ref[g], 0.0).astype(BF16)
        rhs = jnp.concatenate(
            [vn[ch * GMLP_CHUNK:(ch + 1) * GMLP_CHUNK, g * gd:(g + 1) * gd] for ch in range(n_chunks)], axis=1)
        mixed.append(jnp.dot(wsg, rhs, preferred_element_type=F32))
    bias = bias_ref[...]
    rows = []
    for ch in range(n_chunks):
        rows.append(jnp.concatenate([mixed[g][:, ch * gd:(ch + 1) * gd] for g in range(n_groups)], axis=1) + bias)
    mix = jnp.concatenate(rows, axis=0)
    y = u_ref[...].astype(F32) * mix
    o_ref[...] = (y * _silu(g_ref[...].astype(F32))).astype(o_ref.dtype)


def _spatial_gating(proj, v_gain, w_s, bias_tw, u_off, v_off, g_off, width):
    b, s, _ = proj.shape
    n_chunks = min(GMLP_CHUNKS_PER_STEP, s // GMLP_CHUNK)
    tc = n_chunks * GMLP_CHUNK
    cw = width // 2
    n_groups = w_s.shape[0]
    gpb = cw // GMLP_GROUP_DIM
    assert s % tc == 0 and cw % GMLP_GROUP_DIM == 0 and n_groups == 2 * gpb
    ub, vb, gb = (_col_block(o, cw) for o in (u_off, v_off, g_off))

    def cur(col):
        return pl.BlockSpec((None, tc, cw), lambda bi, i, c: (bi, i, col + c))

    def fixed(col):
        return pl.BlockSpec((None, tc, cw), lambda bi, i, c: (bi, i, col))

    blocks = 6 * tc * cw * 2 + gpb * GMLP_CHUNK * GMLP_CHUNK * 4 + 8 * tc * cw * 4
    return pl.pallas_call(
        functools.partial(_gmlp_kernel, n_chunks=n_chunks, n_groups=gpb),
        grid=(b, s // tc, width // cw),
        in_specs=[cur(ub), cur(vb), fixed(vb), fixed(vb + 1), cur(gb),
                  pl.BlockSpec((1, cw), lambda bi, i, c: (0, c)),
                  pl.BlockSpec((gpb, GMLP_CHUNK, GMLP_CHUNK), lambda bi, i, c: (c, 0, 0)),
                  pl.BlockSpec((GMLP_CHUNK, cw), lambda bi, i, c: (0, c))],
        out_specs=pl.BlockSpec((None, tc, cw), lambda bi, i, c: (bi, i, c)),
        out_shape=jax.ShapeDtypeStruct((b, s, width), BF16),
        compiler_params=_compiler_params(("parallel", "arbitrary", "arbitrary"), blocks),
        name="spatial_gating",
    )(proj, proj, proj, proj, proj, v_gain.reshape(1, width), w_s, bias_tw)


def _outproj_kernel(ya_ref, yb_ref, yc_ref, yd_ref, w_ref, x_ref, o_ref, *, width):
    acc = x_ref[...]
    for idx, y_ref in enumerate((ya_ref, yb_ref, yc_ref, yd_ref)):
        acc = acc + jnp.dot(y_ref[...], w_ref[idx * width:(idx + 1) * width, :], preferred_element_type=F32)
    o_ref[...] = acc


def _outproj(ys, w_all, layer, x):
    m, d = x.shape
    width = ys[0].shape[1]
    tm = min(1024, m)
    tn = min(1024, d)
    assert m % tm == 0 and d % tn == 0
    blocks = 4 * tm * width * 2 + 4 * width * tn * 2 + 3 * tm * tn * 4
    y_spec = pl.BlockSpec((tm, width), lambda i, j: (i, 0))
    return pl.pallas_call(
        functools.partial(_outproj_kernel, width=width),
        grid=(m // tm, d // tn),
        in_specs=[y_spec, y_spec, y_spec, y_spec,
                  pl.BlockSpec((None, 4 * width, tn), lambda i, j: (layer, 0, j)),
                  pl.BlockSpec((tm, tn), lambda i, j: (i, j))],
        out_specs=pl.BlockSpec((tm, tn), lambda i, j: (i, j)),
        out_shape=jax.ShapeDtypeStruct((m, d), F32),
        compiler_params=_compiler_params(("parallel", "arbitrary"), blocks),
        name="outproj",
    )(*ys, w_all, x)


def kernel(x, norm_gain, w_in, conv_w, gmlp_v_gain, gmlp_w_s, gmlp_b_s, swa_sinks, w_out, final_gain):
    batch, seq, d_model = x.shape
    depth = w_in.shape[0]
    m = batch * seq
    w_a = w_b = w_c = w_d = d_model // 4
    n_sb_heads = w_a // HEAD_DIM
    n_swa_heads = w_b // HEAD_DIM
    n_swa_kv = max(1, n_swa_heads // 4)
    w_kv = n_swa_kv * HEAD_DIM
    n_groups = gmlp_w_s.shape[1]
    assert n_swa_heads == swa_sinks.shape[1] and w_d == n_groups * GMLP_GROUP_DIM

    widths = (w_a, w_a, w_a, w_a, w_b, w_kv, w_kv, w_b, w_c, w_c, w_c, w_c, w_d, w_d, w_d)
    offs = [0]
    for wd in widths:
        offs.append(offs[-1] + wd)
    (qa, ka, va, ga, qb, kb, vb, gb, xc, bc, cc, gc, ud, vd, gd) = offs[:-1]
    n_proj = offs[-1]
    assert n_proj == w_in.shape[2]

    scale = HEAD_DIM ** -0.5
    col = jnp.arange(n_proj)
    col_scale = jnp.where((col >= qa) & (col < qa + w_a), scale * LOG2_E,
                          jnp.where((col >= qb) & (col < qb + w_b), scale, 1.0))
    col_scale = col_scale.astype(F32).reshape(1, n_proj)

    w_out_bf16 = w_out.astype(BF16)
    h = x.reshape(m, d_model)
    for l in range(depth):
        hn = _rmsnorm(h, norm_gain[l], BF16)
        proj = _inproj(hn, w_in, l, col_scale).reshape(batch, seq, n_proj)
        y_a = _sb_attention(proj, qa, ka, va, ga, n_sb_heads)
        y_b = _swa_attention(proj, swa_sinks[l], qb, kb, vb, gb, n_swa_heads, n_swa_kv)
        y_c = _short_conv(proj, conv_w[l], xc, bc, cc, gc, w_c)
        bias_tw = jnp.repeat(gmlp_b_s[l].T, GMLP_GROUP_DIM, axis=1)
        y_d = _spatial_gating(proj, gmlp_v_gain[l], gmlp_w_s[l], bias_tw, ud, vd, gd, w_d)
        ys = [y.reshape(m, -1) for y in (y_a, y_b, y_c, y_d)]
        h = _outproj(ys, w_out_bf16, l, h)
    out = _rmsnorm(h, final_gain, F32)
    return out.reshape(batch, seq, d_model)
```

```python
import functools

import jax
import jax.numpy as jnp
from jax import lax
from jax.experimental import pallas as pl
from jax.experimental.pallas import tpu as pltpu

HEAD_DIM = 128
GMLP_GROUP_DIM = 128
GMLP_CHUNK = 128
WINDOW = 128
CONV_WIDTH = 3
EPS = 1e-6
LOG2_E = 1.4426950408889634
EXP2_CLAMP = 64.0
F32_EXP2_UNDERFLOW = 160.0

LANES = 128
SUBLANES = 8
V7X_VMEM_BYTES = 64 * 1024 * 1024
VMEM_SPILL_ALLOWANCE = 6 * 1024 * 1024
MASKED_SCORE = -1e30
SB_HEADS_PER_STEP = 4
SWA_BLOCKS_PER_STEP = 4
GMLP_CHUNKS_PER_STEP = 8

F32 = jnp.float32
BF16 = jnp.bfloat16


def _compiler_params(semantics, block_bytes):
    limit = min(2 * block_bytes + VMEM_SPILL_ALLOWANCE, V7X_VMEM_BYTES - 4 * 1024 * 1024)
    return pltpu.CompilerParams(dimension_semantics=semantics, vmem_limit_bytes=int(limit))


def _col_block(offset, width):
    assert offset % width == 0, (offset, width)
    return offset // width


def _silu(g):
    return g * jax.nn.sigmoid(g)


def _rmsnorm_kernel(x_ref, g_ref, o_ref):
    x = x_ref[...]
    ms = jnp.mean(x * x, axis=-1, keepdims=True)
    o_ref[...] = (x * lax.rsqrt(ms + EPS) * g_ref[...]).astype(o_ref.dtype)


def _rmsnorm(x, gain, out_dtype):
    m, d = x.shape
    tm = min(256, m)
    return pl.pallas_call(
        _rmsnorm_kernel,
        grid=(m // tm,),
        in_specs=[pl.BlockSpec((tm, d), lambda i: (i, 0)),
                  pl.BlockSpec((1, d), lambda i: (0, 0))],
        out_specs=pl.BlockSpec((tm, d), lambda i: (i, 0)),
        out_shape=jax.ShapeDtypeStruct((m, d), out_dtype),
        compiler_params=_compiler_params(("parallel",), tm * d * 12),
        name="rmsnorm",
    )(x, gain.reshape(1, d))


def _inproj_kernel(h_ref, w_ref, s_ref, o_ref):
    acc = jnp.dot(h_ref[...], w_ref[...].astype(BF16), preferred_element_type=F32)
    o_ref[...] = (acc * s_ref[...]).astype(o_ref.dtype)


def _inproj(hn, w_all, layer, col_scale):
    m, d = hn.shape
    n = w_all.shape[2]
    tm = min(1024, m)
    tn = 768 if n % 768 == 0 else 512
    assert m % tm == 0 and n % tn == 0
    blocks = tm * d * 2 + d * tn * (4 + 2) + tm * tn * (2 + 4)
    return pl.pallas_call(
        _inproj_kernel,
        grid=(m // tm, n // tn),
        in_specs=[pl.BlockSpec((tm, d), lambda i, j: (i, 0)),
                  pl.BlockSpec((None, d, tn), lambda i, j: (layer, 0, j)),
                  pl.BlockSpec((1, tn), lambda i, j: (0, j))],
        out_specs=pl.BlockSpec((tm, tn), lambda i, j: (i, j)),
        out_shape=jax.ShapeDtypeStruct((m, n), BF16),
        compiler_params=_compiler_params(("parallel", "arbitrary"), blocks),
        name="inproj",
    )(hn, w_all, col_scale)


def _sb_attn_kernel(q_ref, k_ref, v_ref, g_ref, o_ref, z_scr, w_scr, carry_scr, acc_scr, *, tq, heads):
    i = pl.program_id(2)
    dh = HEAD_DIM
    hds = range(heads)
    row = lax.broadcasted_iota(jnp.int32, (tq, tq), 0)
    col = lax.broadcasted_iota(jnp.int32, (tq, tq), 1)
    strictly_causal = col < row
    later_keys = ((row > col) | (col == tq - 1)).astype(BF16)
    last_lane = lax.broadcasted_iota(jnp.int32, (tq, LANES), 1) == LANES - 1

    def key_rows(kb):
        return pl.ds(pl.multiple_of(kb * tq, tq), tq)

    def head_cols(hd):
        return slice(hd * dh, (hd + 1) * dh)

    def scores(kb, hd):
        return lax.dot_general(q_ref[:, head_cols(hd)], k_ref[key_rows(kb), head_cols(hd)],
                               (((1,), (1,)), ((), ())), preferred_element_type=F32)

    def softplus2(z, diagonal):
        p = jnp.maximum(jnp.log2(1.0 + jnp.exp2(jnp.minimum(z, EXP2_CLAMP))), z)
        return jnp.where(strictly_causal, p, 0.0) if diagonal else p

    def with_carry(p, z, carry):
        p = jnp.concatenate([p[:, :tq - LANES], p[:, tq - LANES:] + carry], axis=1)
        return p.astype(BF16), z - p

    def weights(u, cs, diagonal):
        cs_tail = cs[:, tq - LANES:]
        carry = jnp.where(last_lane, cs_tail, 0.0)
        cs = jnp.concatenate([cs[:, :tq - LANES], jnp.where(last_lane, 0.0, cs_tail)], axis=1)
        w = jnp.exp2(u - cs)
        if diagonal:
            w = jnp.where(strictly_causal, w, 0.0)
        return w.astype(BF16), carry

    def finish(u, cs, hd, diagonal):
        w_scr[hd], carry_scr[hd] = weights(u, cs, diagonal)

    def apply_weights(kb):
        for hd in hds:
            acc_scr[:, head_cols(hd)] += jnp.dot(w_scr[hd], v_ref[key_rows(kb), head_cols(hd)],
                                                 preferred_element_type=F32)

    acc_scr[...] = jnp.zeros_like(acc_scr)
    zero_carry = jnp.zeros((tq, LANES), F32)

    @pl.when(i == 0)
    def _():
        zs = [scores(0, hd) for hd in hds]
        terms = [with_carry(softplus2(zs[hd], True), zs[hd], zero_carry) for hd in hds]
        for hd in hds:
            cs = jnp.dot(terms[hd][0], later_keys, preferred_element_type=F32)
            finish(terms[hd][1], cs, hd, True)

    @pl.when(i > 0)
    def _():
        z_d = [scores(i, hd) for hd in hds]
        z_p = [scores(i - 1, hd) for hd in hds]
        p_d = [softplus2(z_d[hd], True) for hd in hds]
        p_p = [softplus2(z_p[hd], False) for hd in hds]
        for hd in hds:
            p_bf16, u = with_carry(p_d[hd], z_d[hd], zero_carry)
            w_d, carry = weights(u, jnp.dot(p_bf16, later_keys, preferred_element_type=F32), True)
            acc_scr[:, head_cols(hd)] += jnp.dot(w_d, v_ref[key_rows(i), head_cols(hd)],
                                                 preferred_element_type=F32)
            p_bf16, u = with_carry(p_p[hd], z_p[hd], carry)
            finish(u, jnp.dot(p_bf16, later_keys, preferred_element_type=F32), hd, False)
        nxt = jnp.maximum(i - 2, 0)
        for hd in hds:
            z_scr[0, hd] = scores(nxt, hd)

    def step(kb, z_in, z_out):
        apply_weights(kb + 1)
        nxt = jnp.maximum(kb - 1, 0)
        for hd in hds:
            z_scr[z_out, hd] = scores(nxt, hd)
        terms = []
        for hd in hds:
            z = z_scr[z_in, hd]
            p_bf16, u = with_carry(softplus2(z, False), z, carry_scr[hd])
            z_scr[z_in, hd] = u
            terms.append(p_bf16)
        for hd in hds:
            cs = jnp.dot(terms[hd], later_keys, preferred_element_type=F32)
            finish(z_scr[z_in, hd], cs, hd, False)

    def stick_used_up():
        swept = carry_scr[0]
        for hd in range(1, heads):
            swept = jnp.minimum(swept, carry_scr[hd])
        return jnp.max(jnp.min(swept, axis=0, keepdims=True)) >= F32_EXP2_UNDERFLOW

    n_left = jnp.maximum(i - 1, 0)
    n_pairs = n_left // 2

    def two_steps(state):
        m, _, _ = state
        kb = i - 2 - 2 * m
        step(kb, 0, 1)
        used_up = stick_used_up()

        @pl.when(jnp.logical_not(used_up))
        def _():
            step(kb - 1, 1, 0)

        return m + 1, jnp.where(used_up, kb, kb - 1), jnp.logical_or(used_up, stick_used_up())

    pairs_done, last_block, used_up = lax.while_loop(
        lambda st: jnp.logical_and(st[0] < n_pairs, jnp.logical_not(st[2])),
        two_steps, (jnp.int32(0), n_left, stick_used_up()))
    odd_step = jnp.logical_and(jnp.logical_and(n_left % 2 == 1, pairs_done == n_pairs), jnp.logical_not(used_up))

    @pl.when(odd_step)
    def _():
        step(0, 0, 1)

    apply_weights(jnp.where(odd_step, 0, last_block))
    o_ref[...] = (acc_scr[...] * _silu(g_ref[...].astype(F32))).astype(o_ref.dtype)


def _sb_attention(proj, q_off, k_off, v_off, g_off, n_heads):
    b, s, _ = proj.shape
    tq = min(256, s)
    heads = min(SB_HEADS_PER_STEP, n_heads)
    assert s % tq == 0 and tq % LANES == 0 and n_heads % heads == 0
    gw = heads * HEAD_DIM
    qb, kb, vb, gb = (_col_block(o, gw) for o in (q_off, k_off, v_off, g_off))
    blocks = 3 * tq * gw * 2 + 2 * s * gw * 2 + heads * 8 * tq * tq * 4
    return pl.pallas_call(
        functools.partial(_sb_attn_kernel, tq=tq, heads=heads),
        grid=(b, n_heads // heads, s // tq),
        in_specs=[pl.BlockSpec((None, tq, gw), lambda bi, h, i: (bi, i, qb + h)),
                  pl.BlockSpec((None, s, gw), lambda bi, h, i: (bi, 0, kb + h)),
                  pl.BlockSpec((None, s, gw), lambda bi, h, i: (bi, 0, vb + h)),
                  pl.BlockSpec((None, tq, gw), lambda bi, h, i: (bi, i, gb + h))],
        out_specs=pl.BlockSpec((None, tq, gw), lambda bi, h, i: (bi, i, h)),
        out_shape=jax.ShapeDtypeStruct((b, s, n_heads * HEAD_DIM), BF16),
        scratch_shapes=[pltpu.VMEM((2, heads, tq, tq), F32),
                        pltpu.VMEM((heads, tq, tq), BF16),
                        pltpu.VMEM((heads, tq, LANES), F32),
                        pltpu.VMEM((tq, gw), F32)],
        compiler_params=_compiler_params(("parallel", "parallel", "arbitrary"), blocks),
        name="sb_attention",
    )(proj, proj, proj, proj)


def _swa_kernel(sink_ref, q_ref, kp_ref, kc_ref, vp_ref, vc_ref, g_ref, o_ref, *, group, n_sub):
    kv = pl.program_id(1)
    n = pl.program_id(2)
    k_all = jnp.concatenate([kp_ref[...], kc_ref[...]], axis=0)
    v_all = jnp.concatenate([vp_ref[...], vc_ref[...]], axis=0)
    r = lax.broadcasted_iota(jnp.int32, (WINDOW, 2 * WINDOW), 0)
    c = lax.broadcasted_iota(jnp.int32, (WINDOW, 2 * WINDOW), 1)
    in_window = (c > r) & (c <= r + WINDOW)
    for sub in range(n_sub):
        rows = slice(sub * WINDOW, (sub + 1) * WINDOW)
        k = k_all[sub * WINDOW:(sub + 2) * WINDOW]
        v = v_all[sub * WINDOW:(sub + 2) * WINDOW]
        valid = in_window & ((c >= WINDOW) | (n > 0)) if sub == 0 else in_window
        outs = []
        for g in range(group):
            qg = q_ref[rows, g * HEAD_DIM:(g + 1) * HEAD_DIM]
            s = lax.dot_general(qg, k, (((1,), (1,)), ((), ())), preferred_element_type=F32)
            s = jnp.where(valid, s, MASKED_SCORE)
            sink = sink_ref[kv * group + g]
            m = jnp.maximum(jnp.max(s, axis=-1, keepdims=True), sink)
            p = jnp.exp(s - m)
            denom = jnp.sum(p, axis=-1, keepdims=True) + jnp.exp(sink - m)
            outs.append(jnp.dot(p.astype(BF16), v, preferred_element_type=F32) / denom)
        o = jnp.concatenate(outs, axis=1)
        o_ref[rows, :] = (o * _silu(g_ref[rows, :].astype(F32))).astype(o_ref.dtype)


def _swa_attention(proj, sinks, q_off, k_off, v_off, g_off, n_heads, n_kv):
    b, s, _ = proj.shape
    group = n_heads // n_kv
    gw = group * HEAD_DIM
    n_sub = min(SWA_BLOCKS_PER_STEP, s // WINDOW)
    tq = n_sub * WINDOW
    assert s % tq == 0
    qb, gb = _col_block(q_off, gw), _col_block(g_off, gw)
    kb, vb = _col_block(k_off, HEAD_DIM), _col_block(v_off, HEAD_DIM)
    blocks = 3 * tq * gw * 2 + 2 * (tq + WINDOW) * HEAD_DIM * 2 + 8 * group * n_sub * WINDOW * 2 * WINDOW * 4

    def cur(col):
        return pl.BlockSpec((None, tq, HEAD_DIM), lambda bi, kv, n: (bi, n, col + kv))

    def prev(col):
        return pl.BlockSpec((None, WINDOW, HEAD_DIM), lambda bi, kv, n: (bi, jnp.maximum(n * n_sub - 1, 0), col + kv))

    return pl.pallas_call(
        functools.partial(_swa_kernel, group=group, n_sub=n_sub),
        grid=(b, n_kv, s // tq),
        in_specs=[pl.BlockSpec(memory_space=pltpu.SMEM),
                  pl.BlockSpec((None, tq, gw), lambda bi, kv, n: (bi, n, qb + kv)),
                  prev(kb), cur(kb), prev(vb), cur(vb),
                  pl.BlockSpec((None, tq, gw), lambda bi, kv, n: (bi, n, gb + kv))],
        out_specs=pl.BlockSpec((None, tq, gw), lambda bi, kv, n: (bi, n, kv)),
        out_shape=jax.ShapeDtypeStruct((b, s, n_heads * HEAD_DIM), BF16),
        compiler_params=_compiler_params(("parallel", "parallel", "arbitrary"), blocks),
        name="swa_attention",
    )(sinks, proj, proj, proj, proj, proj, proj)


def _conv_kernel(x_ref, xp_ref, c_ref, cp_ref, b_ref, g_ref, w_ref, o_ref):
    i = pl.program_id(1)
    h = c_ref[...].astype(F32) * x_ref[...].astype(F32)
    hp = cp_ref[...].astype(F32) * xp_ref[...].astype(F32)
    hp = jnp.where(i > 0, hp, 0.0)
    ext = jnp.concatenate([hp, h], axis=0)
    h1 = pltpu.roll(ext, 1, 0)[SUBLANES:]
    h2 = pltpu.roll(ext, 2, 0)[SUBLANES:]
    w = w_ref[...]
    y = w[0:1] * h2 + w[1:2] * h1 + w[2:3] * h
    y = b_ref[...].astype(F32) * y
    o_ref[...] = (y * _silu(g_ref[...].astype(F32))).astype(o_ref.dtype)


def _short_conv(proj, conv_w, x_off, b_off, c_off, g_off, width):
    b, s, _ = proj.shape
    tm = min(1024, s)
    cw = width // 2
    assert s % tm == 0 and tm % SUBLANES == 0 and cw % LANES == 0
    xb, bb, cb, gb = (_col_block(o, cw) for o in (x_off, b_off, c_off, g_off))
    rows_per_blk = tm // SUBLANES

    def cur(col):
        return pl.BlockSpec((None, tm, cw), lambda bi, i, c: (bi, i, col + c))

    def prev(col):
        return pl.BlockSpec((None, SUBLANES, cw),
                            lambda bi, i, c: (bi, jnp.maximum(i * rows_per_blk - 1, 0), col + c))

    blocks = 5 * tm * cw * 2 + 6 * tm * cw * 4
    return pl.pallas_call(
        _conv_kernel,
        grid=(b, s // tm, width // cw),
        in_specs=[cur(xb), prev(xb), cur(cb), prev(cb), cur(bb), cur(gb),
                  pl.BlockSpec((CONV_WIDTH, cw), lambda bi, i, c: (0, c))],
        out_specs=pl.BlockSpec((None, tm, cw), lambda bi, i, c: (bi, i, c)),
        out_shape=jax.ShapeDtypeStruct((b, s, width), BF16),
        compiler_params=_compiler_params(("parallel", "arbitrary", "arbitrary"), blocks),
        name="short_conv",
    )(proj, proj, proj, proj, proj, proj, conv_w)


def _gmlp_kernel(u_ref, v_ref, v0_ref, v1_ref, g_ref, vg_ref, ws_ref, bias_ref, o_ref, *, n_chunks, n_groups):
    v0 = v0_ref[...].astype(F32)
    v1 = v1_ref[...].astype(F32)
    full_width = v0.shape[1] + v1.shape[1]
    ms = (jnp.sum(v0 * v0, axis=-1, keepdims=True) + jnp.sum(v1 * v1, axis=-1, keepdims=True)) / full_width
    vn = (v_ref[...].astype(F32) * lax.rsqrt(ms + EPS) * vg_ref[...]).astype(BF16)
    r = lax.broadcasted_iota(jnp.int32, (GMLP_CHUNK, GMLP_CHUNK), 0)
    c = lax.broadcasted_iota(jnp.int32, (GMLP_CHUNK, GMLP_CHUNK), 1)
    causal = c <= r
    gd = GMLP_GROUP_DIM
    mixed = []
    for g in range(n_groups):
        wsg = jnp.where(causal, ws_ref[g], 0.0).astype(BF16)
        rhs = jnp.concatenate(
            [vn[ch * GMLP_CHUNK:(ch + 1) * GMLP_CHUNK, g * gd:(g + 1) * gd] for ch in range(n_chunks)], axis=1)
        mixed.append(jnp.dot(wsg, rhs, preferred_element_type=F32))
    bias = bias_ref[...]
    rows = []
    for ch in range(n_chunks):
        rows.append(jnp.concatenate([mixed[g][:, ch * gd:(ch + 1) * gd] for g in range(n_groups)], axis=1) + bias)
    mix = jnp.concatenate(rows, axis=0)
    y = u_ref[...].astype(F32) * mix
    o_ref[...] = (y * _silu(g_ref[...].astype(F32))).astype(o_ref.dtype)


def _spatial_gating(proj, v_gain, w_s, bias_tw, u_off, v_off, g_off, width):
    b, s, _ = proj.shape
    n_chunks = min(GMLP_CHUNKS_PER_STEP, s // GMLP_CHUNK)
    tc = n_chunks * GMLP_CHUNK
    cw = width // 2
    n_groups = w_s.shape[0]
    gpb = cw // GMLP_GROUP_DIM
    assert s % tc == 0 and cw % GMLP_GROUP_DIM == 0 and n_groups == 2 * gpb
    ub, vb, gb = (_col_block(o, cw) for o in (u_off, v_off, g_off))

    def cur(col):
        return pl.BlockSpec((None, tc, cw), lambda bi, i, c: (bi, i, col + c))

    def fixed(col):
        return pl.BlockSpec((None, tc, cw), lambda bi, i, c: (bi, i, col))

    blocks = 6 * tc * cw * 2 + gpb * GMLP_CHUNK * GMLP_CHUNK * 4 + 8 * tc * cw * 4
    return pl.pallas_call(
        functools.partial(_gmlp_kernel, n_chunks=n_chunks, n_groups=gpb),
        grid=(b, s // tc, width // cw),
        in_specs=[cur(ub), cur(vb), fixed(vb), fixed(vb + 1), cur(gb),
                  pl.BlockSpec((1, cw), lambda bi, i, c: (0, c)),
                  pl.BlockSpec((gpb, GMLP_CHUNK, GMLP_CHUNK), lambda bi, i, c: (c, 0, 0)),
                  pl.BlockSpec((GMLP_CHUNK, cw), lambda bi, i, c: (0, c))],
        out_specs=pl.BlockSpec((None, tc, cw), lambda bi, i, c: (bi, i, c)),
        out_shape=jax.ShapeDtypeStruct((b, s, width), BF16),
        compiler_params=_compiler_params(("parallel", "arbitrary", "arbitrary"), blocks),
        name="spatial_gating",
    )(proj, proj, proj, proj, proj, v_gain.reshape(1, width), w_s, bias_tw)


def _outproj_kernel(ya_ref, yb_ref, yc_ref, yd_ref, w_ref, x_ref, o_ref, *, width):
    acc = x_ref[...]
    for idx, y_ref in enumerate((ya_ref, yb_ref, yc_ref, yd_ref)):
        acc = acc + jnp.dot(y_ref[...], w_ref[idx * width:(idx + 1) * width, :], preferred_element_type=F32)
    o_ref[...] = acc


def _outproj(ys, w_all, layer, x):
    m, d = x.shape
    width = ys[0].shape[1]
    tm = min(1024, m)
    tn = min(1024, d)
    assert m % tm == 0 and d % tn == 0
    blocks = 4 * tm * width * 2 + 4 * width * tn * 2 + 3 * tm * tn * 4
    y_spec = pl.BlockSpec((tm, width), lambda i, j: (i, 0))
    return pl.pallas_call(
        functools.partial(_outproj_kernel, width=width),
        grid=(m // tm, d // tn),
        in_specs=[y_spec, y_spec, y_spec, y_spec,
                  pl.BlockSpec((None, 4 * width, tn), lambda i, j: (layer, 0, j)),
                  pl.BlockSpec((tm, tn), lambda i, j: (i, j))],
        out_specs=pl.BlockSpec((tm, tn), lambda i, j: (i, j)),
        out_shape=jax.ShapeDtypeStruct((m, d), F32),
        compiler_params=_compiler_params(("parallel", "arbitrary"), blocks),
        name="outproj",
    )(*ys, w_all, x)


def kernel(x, norm_gain, w_in, conv_w, gmlp_v_gain, gmlp_w_s, gmlp_b_s, swa_sinks, w_out, final_gain):
    batch, seq, d_model = x.shape
    depth = w_in.shape[0]
    m = batch * seq
    w_a = w_b = w_c = w_d = d_model // 4
    n_sb_heads = w_a // HEAD_DIM
    n_swa_heads = w_b // HEAD_DIM
    n_swa_kv = max(1, n_swa_heads // 4)
    w_kv = n_swa_kv * HEAD_DIM
    n_groups = gmlp_w_s.shape[1]
    assert n_swa_heads == swa_sinks.shape[1] and w_d == n_groups * GMLP_GROUP_DIM

    widths = (w_a, w_a, w_a, w_a, w_b, w_kv, w_kv, w_b, w_c, w_c, w_c, w_c, w_d, w_d, w_d)
    offs = [0]
    for wd in widths:
        offs.append(offs[-1] + wd)
    (qa, ka, va, ga, qb, kb, vb, gb, xc, bc, cc, gc, ud, vd, gd) = offs[:-1]
    n_proj = offs[-1]
    assert n_proj == w_in.shape[2]

    scale = HEAD_DIM ** -0.5
    col = jnp.arange(n_proj)
    col_scale = jnp.where((col >= qa) & (col < qa + w_a), scale * LOG2_E,
                          jnp.where((col >= qb) & (col < qb + w_b), scale, 1.0))
    col_scale = col_scale.astype(F32).reshape(1, n_proj)

    w_out_bf16 = w_out.astype(BF16)
    h = x.reshape(m, d_model)
    for l in range(depth):
        hn = _rmsnorm(h, norm_gain[l], BF16)
        proj = _inproj(hn, w_in, l, col_scale).reshape(batch, seq, n_proj)
        y_a = _sb_attention(proj, qa, ka, va, ga, n_sb_heads)
        y_b = _swa_attention(proj, swa_sinks[l], qb, kb, vb, gb, n_swa_heads, n_swa_kv)
        y_c = _short_conv(proj, conv_w[l], xc, bc, cc, gc, w_c)
        bias_tw = jnp.repeat(gmlp_b_s[l].T, GMLP_GROUP_DIM, axis=1)
        y_d = _spatial_gating(proj, gmlp_v_gain[l], gmlp_w_s[l], bias_tw, ud, vd, gd, w_d)
        ys = [y.reshape(m, -1) for y in (y_a, y_b, y_c, y_d)]
        h = _outproj(ys, w_out_bf16, l, h)
    out = _rmsnorm(h, final_gain, F32)
    return out.reshape(batch, seq, d_model)
```

```python
import functools

import jax
import jax.numpy as jnp
from jax import lax
from jax.experimental import pallas as pl
from jax.experimental.pallas import tpu as pltpu

HEAD_DIM = 128
GMLP_GROUP_DIM = 128
GMLP_CHUNK = 128
WINDOW = 128
CONV_WIDTH = 3
EPS = 1e-6
LOG2_E = 1.4426950408889634
EXP2_CLAMP = 64.0
F32_EXP2_UNDERFLOW = 160.0

LANES = 128
SUBLANES = 8
V7X_VMEM_BYTES = 64 * 1024 * 1024
VMEM_SPILL_ALLOWANCE = 6 * 1024 * 1024
MASKED_SCORE = -1e30
SB_HEADS_PER_STEP = 4
LOCAL_ROWS_PER_STEP = 512

F32 = jnp.float32
BF16 = jnp.bfloat16


def _compiler_params(semantics, block_bytes):
    limit = min(2 * block_bytes + VMEM_SPILL_ALLOWANCE, V7X_VMEM_BYTES - 4 * 1024 * 1024)
    return pltpu.CompilerParams(dimension_semantics=semantics, vmem_limit_bytes=int(limit))


def _col_block(offset, width):
    assert offset % width == 0, (offset, width)
    return offset // width


def _silu(g):
    return g * jax.nn.sigmoid(g)


def _rmsnorm_kernel(x_ref, g_ref, o_ref):
    x = x_ref[...]
    ms = jnp.mean(x * x, axis=-1, keepdims=True)
    o_ref[...] = (x * lax.rsqrt(ms + EPS) * g_ref[...]).astype(o_ref.dtype)


def _rmsnorm(x, gain, out_dtype):
    m, d = x.shape
    tm = min(256, m)
    return pl.pallas_call(
        _rmsnorm_kernel,
        grid=(m // tm,),
        in_specs=[pl.BlockSpec((tm, d), lambda i: (i, 0)),
                  pl.BlockSpec((1, d), lambda i: (0, 0))],
        out_specs=pl.BlockSpec((tm, d), lambda i: (i, 0)),
        out_shape=jax.ShapeDtypeStruct((m, d), out_dtype),
        compiler_params=_compiler_params(("parallel",), tm * d * 12),
        name="rmsnorm",
    )(x, gain.reshape(1, d))


def _inproj_kernel(h_ref, w_ref, s_ref, o_ref):
    acc = jnp.dot(h_ref[...], w_ref[...].astype(BF16), preferred_element_type=F32)
    o_ref[...] = (acc * s_ref[...]).astype(o_ref.dtype)


def _inproj(hn, w_all, layer, col_scale):
    m, d = hn.shape
    n = w_all.shape[2]
    tm = min(1024, m)
    tn = 768 if n % 768 == 0 else 512
    assert m % tm == 0 and n % tn == 0
    blocks = tm * d * 2 + d * tn * (4 + 2) + tm * tn * (2 + 4)
    return pl.pallas_call(
        _inproj_kernel,
        grid=(m // tm, n // tn),
        in_specs=[pl.BlockSpec((tm, d), lambda i, j: (i, 0)),
                  pl.BlockSpec((None, d, tn), lambda i, j: (layer, 0, j)),
                  pl.BlockSpec((1, tn), lambda i, j: (0, j))],
        out_specs=pl.BlockSpec((tm, tn), lambda i, j: (i, j)),
        out_shape=jax.ShapeDtypeStruct((m, n), BF16),
        compiler_params=_compiler_params(("parallel", "arbitrary"), blocks),
        name="inproj",
    )(hn, w_all, col_scale)


def _sb_attn_kernel(q_ref, k_ref, v_ref, g_ref, o_ref, z_scr, w_scr, carry_scr, acc_scr, *, tq, heads):
    i = pl.program_id(2)
    dh = HEAD_DIM
    hds = range(heads)
    row = lax.broadcasted_iota(jnp.int32, (tq, tq), 0)
    col = lax.broadcasted_iota(jnp.int32, (tq, tq), 1)
    strictly_causal = col < row
    later_keys = ((row > col) | (col == tq - 1)).astype(BF16)
    last_lane = lax.broadcasted_iota(jnp.int32, (tq, LANES), 1) == LANES - 1

    def key_rows(kb):
        return pl.ds(pl.multiple_of(kb * tq, tq), tq)

    def head_cols(hd):
        return slice(hd * dh, (hd + 1) * dh)

    def scores(kb, hd):
        return lax.dot_general(q_ref[:, head_cols(hd)], k_ref[key_rows(kb), head_cols(hd)],
                               (((1,), (1,)), ((), ())), preferred_element_type=F32)

    def softplus_terms(z, carry, diagonal):
        p = jnp.maximum(jnp.log2(1.0 + jnp.exp2(jnp.minimum(z, EXP2_CLAMP))), z)
        if diagonal:
            p = jnp.where(strictly_causal, p, 0.0)
        p = jnp.concatenate([p[:, :tq - LANES], p[:, tq - LANES:] + carry], axis=1)
        return p.astype(BF16), z - p

    def finish(u, cs, hd, diagonal):
        cs_tail = cs[:, tq - LANES:]
        carry_scr[hd] = jnp.where(last_lane, cs_tail, 0.0)
        cs = jnp.concatenate([cs[:, :tq - LANES], jnp.where(last_lane, 0.0, cs_tail)], axis=1)
        w = jnp.exp2(u - cs)
        if diagonal:
            w = jnp.where(strictly_causal, w, 0.0)
        w_scr[hd] = w.astype(BF16)

    def apply_weights(kb):
        for hd in hds:
            acc_scr[:, head_cols(hd)] += jnp.dot(w_scr[hd], v_ref[key_rows(kb), head_cols(hd)],
                                                 preferred_element_type=F32)

    acc_scr[...] = jnp.zeros_like(acc_scr)
    zs = [scores(i, hd) for hd in hds]
    zero_carry = jnp.zeros((tq, LANES), F32)
    terms = [softplus_terms(zs[hd], zero_carry, True) for hd in hds]
    nxt = jnp.maximum(i - 1, 0)
    for hd in hds:
        p_bf16, u = terms[hd]
        cs = jnp.dot(p_bf16, later_keys, preferred_element_type=F32)
        finish(u, cs, hd, True)
        z_scr[0, hd] = scores(nxt, hd)

    def step(kb, z_in, z_out):
        apply_weights(kb + 1)
        nxt = jnp.maximum(kb - 1, 0)
        for hd in hds:
            z_scr[z_out, hd] = scores(nxt, hd)
        terms = []
        for hd in hds:
            p_bf16, u = softplus_terms(z_scr[z_in, hd], carry_scr[hd], False)
            z_scr[z_in, hd] = u
            terms.append(p_bf16)
        for hd in hds:
            cs = jnp.dot(terms[hd], later_keys, preferred_element_type=F32)
            finish(z_scr[z_in, hd], cs, hd, False)

    def stick_used_up():
        swept = carry_scr[0]
        for hd in range(1, heads):
            swept = jnp.minimum(swept, carry_scr[hd])
        return jnp.max(jnp.min(swept, axis=0, keepdims=True)) >= F32_EXP2_UNDERFLOW

    n_pairs = i // 2

    def two_steps(state):
        m, _, _ = state
        kb = i - 1 - 2 * m
        step(kb, 0, 1)
        used_up = stick_used_up()

        @pl.when(jnp.logical_not(used_up))
        def _():
            step(kb - 1, 1, 0)

        return m + 1, jnp.where(used_up, kb, kb - 1), jnp.logical_or(used_up, stick_used_up())

    pairs_done, last_block, used_up = lax.while_loop(
        lambda st: jnp.logical_and(st[0] < n_pairs, jnp.logical_not(st[2])),
        two_steps, (jnp.int32(0), i, stick_used_up()))
    odd_step = jnp.logical_and(jnp.logical_and(i % 2 == 1, pairs_done == n_pairs), jnp.logical_not(used_up))

    @pl.when(odd_step)
    def _():
        step(0, 0, 1)

    apply_weights(jnp.where(odd_step, 0, last_block))
    o_ref[...] = (acc_scr[...] * _silu(g_ref[...].astype(F32))).astype(o_ref.dtype)


def _sb_attention(proj, q_off, k_off, v_off, g_off, n_heads):
    b, s, _ = proj.shape
    tq = min(256, s)
    heads = min(SB_HEADS_PER_STEP, n_heads)
    assert s % tq == 0 and tq % LANES == 0 and n_heads % heads == 0
    gw = heads * HEAD_DIM
    qb, kb, vb, gb = (_col_block(o, gw) for o in (q_off, k_off, v_off, g_off))
    blocks = 3 * tq * gw * 2 + 2 * s * gw * 2 + heads * 8 * tq * tq * 4
    return pl.pallas_call(
        functools.partial(_sb_attn_kernel, tq=tq, heads=heads),
        grid=(b, n_heads // heads, s // tq),
        in_specs=[pl.BlockSpec((None, tq, gw), lambda bi, h, i: (bi, i, qb + h)),
                  pl.BlockSpec((None, s, gw), lambda bi, h, i: (bi, 0, kb + h)),
                  pl.BlockSpec((None, s, gw), lambda bi, h, i: (bi, 0, vb + h)),
                  pl.BlockSpec((None, tq, gw), lambda bi, h, i: (bi, i, gb + h))],
        out_specs=pl.BlockSpec((None, tq, gw), lambda bi, h, i: (bi, i, h)),
        out_shape=jax.ShapeDtypeStruct((b, s, n_heads * HEAD_DIM), BF16),
        scratch_shapes=[pltpu.VMEM((2, heads, tq, tq), F32),
                        pltpu.VMEM((heads, tq, tq), BF16),
                        pltpu.VMEM((heads, tq, LANES), F32),
                        pltpu.VMEM((tq, gw), F32)],
        compiler_params=_compiler_params(("parallel", "parallel", "arbitrary"), blocks),
        name="sb_attention",
    )(proj, proj, proj, proj)


def _local_mixers_kernel(sink_ref,
                         q_ref, kp_ref, kc_ref, vp_ref, vc_ref, gb0_ref, gb1_ref,
                         x0_ref, x1_ref, xp0_ref, xp1_ref, c0_ref, c1_ref, cp0_ref, cp1_ref,
                         b0_ref, b1_ref, gc0_ref, gc1_ref, cw_ref,
                         u0_ref, u1_ref, v0_ref, v1_ref, gd0_ref, gd1_ref, vg_ref, ws_ref, bias_ref,
                         yb_ref, yc_ref, yd_ref, *, n_kv, group, n_sub):
    n = pl.program_id(1)

    def both(r0, r1):
        return jnp.concatenate([r0[...], r1[...]], axis=1).astype(F32)

    k_all = jnp.concatenate([kp_ref[...], kc_ref[...]], axis=0)
    v_all = jnp.concatenate([vp_ref[...], vc_ref[...]], axis=0)
    r = lax.broadcasted_iota(jnp.int32, (WINDOW, 2 * WINDOW), 0)
    c = lax.broadcasted_iota(jnp.int32, (WINDOW, 2 * WINDOW), 1)
    in_window = (c > r) & (c <= r + WINDOW)
    gate_b = _silu(both(gb0_ref, gb1_ref))
    for sub in range(n_sub):
        rows = slice(sub * WINDOW, (sub + 1) * WINDOW)
        valid = in_window & ((c >= WINDOW) | (n > 0)) if sub == 0 else in_window
        outs = []
        for kv in range(n_kv):
            k = k_all[sub * WINDOW:(sub + 2) * WINDOW, kv * HEAD_DIM:(kv + 1) * HEAD_DIM]
            v = v_all[sub * WINDOW:(sub + 2) * WINDOW, kv * HEAD_DIM:(kv + 1) * HEAD_DIM]
            for g in range(group):
                head = kv * group + g
                qg = q_ref[rows, head * HEAD_DIM:(head + 1) * HEAD_DIM]
                s = lax.dot_general(qg, k, (((1,), (1,)), ((), ())), preferred_element_type=F32)
                s = jnp.where(valid, s, MASKED_SCORE)
                sink = sink_ref[head]
                m = jnp.maximum(jnp.max(s, axis=-1, keepdims=True), sink)
                p = jnp.exp(s - m)
                denom = jnp.sum(p, axis=-1, keepdims=True) + jnp.exp(sink - m)
                outs.append(jnp.dot(p.astype(BF16), v, preferred_element_type=F32) / denom)
        o = jnp.concatenate(outs, axis=1)
        yb_ref[rows, :] = (o * gate_b[rows]).astype(yb_ref.dtype)

    h = both(c0_ref, c1_ref) * both(x0_ref, x1_ref)
    hp = both(cp0_ref, cp1_ref) * both(xp0_ref, xp1_ref)
    hp = jnp.where(n > 0, hp, 0.0)
    ext = jnp.concatenate([hp, h], axis=0)
    h1 = pltpu.roll(ext, 1, 0)[SUBLANES:]
    h2 = pltpu.roll(ext, 2, 0)[SUBLANES:]
    w = cw_ref[...]
    y = w[0:1] * h2 + w[1:2] * h1 + w[2:3] * h
    y = both(b0_ref, b1_ref) * y
    yc_ref[...] = (y * _silu(both(gc0_ref, gc1_ref))).astype(yc_ref.dtype)

    v = both(v0_ref, v1_ref)
    ms = jnp.mean(v * v, axis=-1, keepdims=True)
    vn = (v * lax.rsqrt(ms + EPS) * vg_ref[...]).astype(BF16)
    rr = lax.broadcasted_iota(jnp.int32, (GMLP_CHUNK, GMLP_CHUNK), 0)
    cc = lax.broadcasted_iota(jnp.int32, (GMLP_CHUNK, GMLP_CHUNK), 1)
    causal = cc <= rr
    gd = GMLP_GROUP_DIM
    n_groups = ws_ref.shape[0]
    n_chunks = v.shape[0] // GMLP_CHUNK
    mixed = []
    for g in range(n_groups):
        wsg = jnp.where(causal, ws_ref[g], 0.0).astype(BF16)
        rhs = jnp.concatenate(
            [vn[ch * GMLP_CHUNK:(ch + 1) * GMLP_CHUNK, g * gd:(g + 1) * gd] for ch in range(n_chunks)], axis=1)
        mixed.append(jnp.dot(wsg, rhs, preferred_element_type=F32))
    bias = bias_ref[...]
    chunks = []
    for ch in range(n_chunks):
        chunks.append(jnp.concatenate([mixed[g][:, ch * gd:(ch + 1) * gd] for g in range(n_groups)], axis=1) + bias)
    mix = jnp.concatenate(chunks, axis=0)
    y = both(u0_ref, u1_ref) * mix
    yd_ref[...] = (y * _silu(both(gd0_ref, gd1_ref))).astype(yd_ref.dtype)


def _local_mixers(proj, sinks, conv_w, v_gain, w_s, bias_tw, offs, n_heads, n_kv, width):
    qb, kb, vb, gb, xc, bc, cc, gc, ud, vd, gd = offs
    b, s, _ = proj.shape
    rows = min(LOCAL_ROWS_PER_STEP, s)
    cw = width // 2
    w_kv = n_kv * HEAD_DIM
    assert s % rows == 0 and rows % WINDOW == 0 and rows % GMLP_CHUNK == 0
    assert n_heads * HEAD_DIM == width and w_s.shape[0] * GMLP_GROUP_DIM == width

    def seg(off, w):
        col = _col_block(off, w)
        return pl.BlockSpec((None, rows, w), lambda bi, n: (bi, n, col))

    def seg_before(off, w, n_rows):
        col = _col_block(off, w)
        per_step = rows // n_rows
        return pl.BlockSpec((None, n_rows, w), lambda bi, n: (bi, jnp.maximum(n * per_step - 1, 0), col))

    def halves(off):
        return [seg(off, cw), seg(off + cw, cw)]

    def halves_before(off):
        return [seg_before(off, cw, SUBLANES), seg_before(off + cw, cw, SUBLANES)]

    def whole(shape):
        return pl.BlockSpec(shape, lambda bi, n: (0,) * len(shape))

    in_specs = ([pl.BlockSpec(memory_space=pltpu.SMEM),
                 seg(qb, width), seg_before(kb, w_kv, WINDOW), seg(kb, w_kv),
                 seg_before(vb, w_kv, WINDOW), seg(vb, w_kv)] + halves(gb)
                + halves(xc) + halves_before(xc) + halves(cc) + halves_before(cc) + halves(bc) + halves(gc)
                + [whole((CONV_WIDTH, width))]
                + halves(ud) + halves(vd) + halves(gd)
                + [whole((1, width)), whole(w_s.shape), whole((GMLP_CHUNK, width))])
    operands = [sinks] + [proj] * 7 + [proj] * 12 + [conv_w] + [proj] * 6 + [v_gain.reshape(1, width), w_s, bias_tw]
    assert len(operands) == len(in_specs)
    out_spec = pl.BlockSpec((None, rows, width), lambda bi, n: (bi, n, 0))
    out_shape = jax.ShapeDtypeStruct((b, s, width), BF16)
    blocks = (17 + 3) * rows * cw * 2 + 12 * rows * width * 4
    return pl.pallas_call(
        functools.partial(_local_mixers_kernel, n_kv=n_kv, group=n_heads // n_kv, n_sub=rows // WINDOW),
        grid=(b, s // rows),
        in_specs=in_specs,
        out_specs=[out_spec, out_spec, out_spec],
        out_shape=[out_shape, out_shape, out_shape],
        compiler_params=_compiler_params(("parallel", "arbitrary"), blocks),
        name="local_mixers",
    )(*operands)


def _outproj_kernel(ya_ref, yb_ref, yc_ref, yd_ref, w_ref, x_ref, o_ref, *, width):
    acc = x_ref[...]
    for idx, y_ref in enumerate((ya_ref, yb_ref, yc_ref, yd_ref)):
        acc = acc + jnp.dot(y_ref[...], w_ref[idx * width:(idx + 1) * width, :], preferred_element_type=F32)
    o_ref[...] = acc


def _outproj(ys, w_all, layer, x):
    m, d = x.shape
    width = ys[0].shape[1]
    tm = min(1024, m)
    tn = min(1024, d)
    assert m % tm == 0 and d % tn == 0
    blocks = 4 * tm * width * 2 + 4 * width * tn * 2 + 3 * tm * tn * 4
    y_spec = pl.BlockSpec((tm, width), lambda i, j: (i, 0))
    return pl.pallas_call(
        functools.partial(_outproj_kernel, width=width),
        grid=(m // tm, d // tn),
        in_specs=[y_spec, y_spec, y_spec, y_spec,
                  pl.BlockSpec((None, 4 * width, tn), lambda i, j: (layer, 0, j)),
                  pl.BlockSpec((tm, tn), lambda i, j: (i, j))],
        out_specs=pl.BlockSpec((tm, tn), lambda i, j: (i, j)),
        out_shape=jax.ShapeDtypeStruct((m, d), F32),
        compiler_params=_compiler_params(("parallel", "arbitrary"), blocks),
        name="outproj",
    )(*ys, w_all, x)


def kernel(x, norm_gain, w_in, conv_w, gmlp_v_gain, gmlp_w_s, gmlp_b_s, swa_sinks, w_out, final_gain):
    batch, seq, d_model = x.shape
    depth = w_in.shape[0]
    m = batch * seq
    w_a = w_b = w_c = w_d = d_model // 4
    n_sb_heads = w_a // HEAD_DIM
    n_swa_heads = w_b // HEAD_DIM
    n_swa_kv = max(1, n_swa_heads // 4)
    w_kv = n_swa_kv * HEAD_DIM
    n_groups = gmlp_w_s.shape[1]
    assert n_swa_heads == swa_sinks.shape[1] and w_d == n_groups * GMLP_GROUP_DIM

    widths = (w_a, w_a, w_a, w_a, w_b, w_kv, w_kv, w_b, w_c, w_c, w_c, w_c, w_d, w_d, w_d)
    offs = [0]
    for wd in widths:
        offs.append(offs[-1] + wd)
    (qa, ka, va, ga, qb, kb, vb, gb, xc, bc, cc, gc, ud, vd, gd) = offs[:-1]
    n_proj = offs[-1]
    assert n_proj == w_in.shape[2]

    scale = HEAD_DIM ** -0.5
    col = jnp.arange(n_proj)
    col_scale = jnp.where((col >= qa) & (col < qa + w_a), scale * LOG2_E,
                          jnp.where((col >= qb) & (col < qb + w_b), scale, 1.0))
    col_scale = col_scale.astype(F32).reshape(1, n_proj)

    w_out_bf16 = w_out.astype(BF16)
    h = x.reshape(m, d_model)
    for l in range(depth):
        hn = _rmsnorm(h, norm_gain[l], BF16)
        proj = _inproj(hn, w_in, l, col_scale).reshape(batch, seq, n_proj)
        y_a = _sb_attention(proj, qa, ka, va, ga, n_sb_heads)
        bias_tw = jnp.repeat(gmlp_b_s[l].T, GMLP_GROUP_DIM, axis=1)
        y_b, y_c, y_d = _local_mixers(proj, swa_sinks[l], conv_w[l], gmlp_v_gain[l], gmlp_w_s[l], bias_tw,
                                      (qb, kb, vb, gb, xc, bc, cc, gc, ud, vd, gd), n_swa_heads, n_swa_kv, w_b)
        ys = [y.reshape(m, -1) for y in (y_a, y_b, y_c, y_d)]
        h = _outproj(ys, w_out_bf16, l, h)
    out = _rmsnorm(h, final_gain, F32)
    return out.reshape(batch, seq, d_model)
```

```python
import functools

import jax
import jax.numpy as jnp
from jax import lax
from jax.experimental import pallas as pl
from jax.experimental.pallas import tpu as pltpu

HEAD_DIM = 128
GMLP_GROUP_DIM = 128
GMLP_CHUNK = 128
WINDOW = 128
CONV_WIDTH = 3
EPS = 1e-6
LOG2_E = 1.4426950408889634
EXP2_CLAMP = 64.0
F32_EXP2_UNDERFLOW = 160.0

LANES = 128
SUBLANES = 8
V7X_VMEM_BYTES = 64 * 1024 * 1024
VMEM_SPILL_ALLOWANCE = 6 * 1024 * 1024
MASKED_SCORE = -1e30
SB_HEADS_PER_STEP = 4
LOCAL_ROWS_PER_STEP = 512

F32 = jnp.float32
BF16 = jnp.bfloat16


def _compiler_params(semantics, block_bytes):
    limit = min(2 * block_bytes + VMEM_SPILL_ALLOWANCE, V7X_VMEM_BYTES - 4 * 1024 * 1024)
    return pltpu.CompilerParams(dimension_semantics=semantics, vmem_limit_bytes=int(limit))


def _col_block(offset, width):
    assert offset % width == 0, (offset, width)
    return offset // width


def _silu(g):
    return g * jax.nn.sigmoid(g)


def _rmsnorm_kernel(x_ref, g_ref, o_ref):
    x = x_ref[...]
    ms = jnp.mean(x * x, axis=-1, keepdims=True)
    o_ref[...] = (x * lax.rsqrt(ms + EPS) * g_ref[...]).astype(o_ref.dtype)


def _rmsnorm(x, gain, out_dtype):
    m, d = x.shape
    tm = min(256, m)
    return pl.pallas_call(
        _rmsnorm_kernel,
        grid=(m // tm,),
        in_specs=[pl.BlockSpec((tm, d), lambda i: (i, 0)),
                  pl.BlockSpec((1, d), lambda i: (0, 0))],
        out_specs=pl.BlockSpec((tm, d), lambda i: (i, 0)),
        out_shape=jax.ShapeDtypeStruct((m, d), out_dtype),
        compiler_params=_compiler_params(("parallel",), tm * d * 12),
        name="rmsnorm",
    )(x, gain.reshape(1, d))


def _inproj_kernel(h_ref, w_ref, s_ref, o_ref):
    acc = jnp.dot(h_ref[...], w_ref[...].astype(BF16), preferred_element_type=F32)
    o_ref[...] = (acc * s_ref[...]).astype(o_ref.dtype)


def _inproj_rowscaled_kernel(h_ref, ssq_ref, w_ref, s_ref, o_ref, *, d_model):
    acc = jnp.dot(h_ref[...], w_ref[...].astype(BF16), preferred_element_type=F32)
    inv_rms = lax.rsqrt(ssq_ref[...] * (1.0 / d_model) + EPS)
    o_ref[...] = (acc * inv_rms * s_ref[...]).astype(o_ref.dtype)


def _inproj(hn, w_all, layer, col_scale, row_ssq=None):
    m, d = hn.shape
    n = w_all.shape[2]
    tm = min(1024, m)
    tn = 768 if n % 768 == 0 else 512
    assert m % tm == 0 and n % tn == 0
    blocks = tm * d * 2 + d * tn * (4 + 2) + tm * tn * (2 + 4)
    h_spec = pl.BlockSpec((tm, d), lambda i, j: (i, 0))
    w_spec = pl.BlockSpec((None, d, tn), lambda i, j: (layer, 0, j))
    s_spec = pl.BlockSpec((1, tn), lambda i, j: (0, j))
    if row_ssq is None:
        body, in_specs, operands = _inproj_kernel, [h_spec, w_spec, s_spec], (hn, w_all, col_scale)
    else:
        body = functools.partial(_inproj_rowscaled_kernel, d_model=d)
        in_specs = [h_spec, pl.BlockSpec((tm, 1), lambda i, j: (i, 0)), w_spec, s_spec]
        operands = (hn, row_ssq, w_all, col_scale)
    return pl.pallas_call(
        body,
        grid=(m // tm, n // tn),
        in_specs=in_specs,
        out_specs=pl.BlockSpec((tm, tn), lambda i, j: (i, j)),
        out_shape=jax.ShapeDtypeStruct((m, n), BF16),
        compiler_params=_compiler_params(("parallel", "arbitrary"), blocks),
        name="inproj",
    )(*operands)


def _sb_attn_kernel(q_ref, k_ref, v_ref, g_ref, o_ref, z_scr, w_scr, carry_scr, acc_scr, *, tq, heads):
    i = pl.program_id(2)
    dh = HEAD_DIM
    hds = range(heads)
    row = lax.broadcasted_iota(jnp.int32, (tq, tq), 0)
    col = lax.broadcasted_iota(jnp.int32, (tq, tq), 1)
    strictly_causal = col < row
    later_keys = ((row > col) | (col == tq - 1)).astype(BF16)
    last_lane = lax.broadcasted_iota(jnp.int32, (tq, LANES), 1) == LANES - 1

    def key_rows(kb):
        return pl.ds(pl.multiple_of(kb * tq, tq), tq)

    def head_cols(hd):
        return slice(hd * dh, (hd + 1) * dh)

    def scores(kb, hd):
        return lax.dot_general(q_ref[:, head_cols(hd)], k_ref[key_rows(kb), head_cols(hd)],
                               (((1,), (1,)), ((), ())), preferred_element_type=F32)

    def softplus_terms(z, carry, diagonal):
        p = jnp.maximum(jnp.log2(1.0 + jnp.exp2(jnp.minimum(z, EXP2_CLAMP))), z)
        if diagonal:
            p = jnp.where(strictly_causal, p, 0.0)
        p = jnp.concatenate([p[:, :tq - LANES], p[:, tq - LANES:] + carry], axis=1)
        return p.astype(BF16), z - p

    def finish(u, cs, hd, diagonal):
        cs_tail = cs[:, tq - LANES:]
        carry_scr[hd] = jnp.where(last_lane, cs_tail, 0.0)
        cs = jnp.concatenate([cs[:, :tq - LANES], jnp.where(last_lane, 0.0, cs_tail)], axis=1)
        w = jnp.exp2(u - cs)
        if diagonal:
            w = jnp.where(strictly_causal, w, 0.0)
        w_scr[hd] = w.astype(BF16)

    def apply_weights(kb):
        for hd in hds:
            acc_scr[:, head_cols(hd)] += jnp.dot(w_scr[hd], v_ref[key_rows(kb), head_cols(hd)],
                                                 preferred_element_type=F32)

    acc_scr[...] = jnp.zeros_like(acc_scr)
    zs = [scores(i, hd) for hd in hds]
    zero_carry = jnp.zeros((tq, LANES), F32)
    terms = [softplus_terms(zs[hd], zero_carry, True) for hd in hds]
    nxt = jnp.maximum(i - 1, 0)
    for hd in hds:
        p_bf16, u = terms[hd]
        cs = jnp.dot(p_bf16, later_keys, preferred_element_type=F32)
        finish(u, cs, hd, True)
        z_scr[0, hd] = scores(nxt, hd)

    def step(kb, z_in, z_out):
        apply_weights(kb + 1)
        nxt = jnp.maximum(kb - 1, 0)
        for hd in hds:
            z_scr[z_out, hd] = scores(nxt, hd)
        terms = []
        for hd in hds:
            p_bf16, u = softplus_terms(z_scr[z_in, hd], carry_scr[hd], False)
            z_scr[z_in, hd] = u
            terms.append(p_bf16)
        for hd in hds:
            cs = jnp.dot(terms[hd], later_keys, preferred_element_type=F32)
            finish(z_scr[z_in, hd], cs, hd, False)

    def stick_used_up():
        swept = carry_scr[0]
        for hd in range(1, heads):
            swept = jnp.minimum(swept, carry_scr[hd])
        return jnp.max(jnp.min(swept, axis=0, keepdims=True)) >= F32_EXP2_UNDERFLOW

    n_pairs = i // 2

    def two_steps(state):
        m, _, _ = state
        kb = i - 1 - 2 * m
        step(kb, 0, 1)
        used_up = stick_used_up()

        @pl.when(jnp.logical_not(used_up))
        def _():
            step(kb - 1, 1, 0)

        return m + 1, jnp.where(used_up, kb, kb - 1), jnp.logical_or(used_up, stick_used_up())

    pairs_done, last_block, used_up = lax.while_loop(
        lambda st: jnp.logical_and(st[0] < n_pairs, jnp.logical_not(st[2])),
        two_steps, (jnp.int32(0), i, stick_used_up()))
    odd_step = jnp.logical_and(jnp.logical_and(i % 2 == 1, pairs_done == n_pairs), jnp.logical_not(used_up))

    @pl.when(odd_step)
    def _():
        step(0, 0, 1)

    apply_weights(jnp.where(odd_step, 0, last_block))
    o_ref[...] = (acc_scr[...] * _silu(g_ref[...].astype(F32))).astype(o_ref.dtype)


def _sb_attention(proj, q_off, k_off, v_off, g_off, n_heads):
    b, s, _ = proj.shape
    tq = min(256, s)
    heads = min(SB_HEADS_PER_STEP, n_heads)
    assert s % tq == 0 and tq % LANES == 0 and n_heads % heads == 0
    gw = heads * HEAD_DIM
    qb, kb, vb, gb = (_col_block(o, gw) for o in (q_off, k_off, v_off, g_off))
    blocks = 3 * tq * gw * 2 + 2 * s * gw * 2 + heads * 8 * tq * tq * 4
    return pl.pallas_call(
        functools.partial(_sb_attn_kernel, tq=tq, heads=heads),
        grid=(b, n_heads // heads, s // tq),
        in_specs=[pl.BlockSpec((None, tq, gw), lambda bi, h, i: (bi, i, qb + h)),
                  pl.BlockSpec((None, s, gw), lambda bi, h, i: (bi, 0, kb + h)),
                  pl.BlockSpec((None, s, gw), lambda bi, h, i: (bi, 0, vb + h)),
                  pl.BlockSpec((None, tq, gw), lambda bi, h, i: (bi, i, gb + h))],
        out_specs=pl.BlockSpec((None, tq, gw), lambda bi, h, i: (bi, i, h)),
        out_shape=jax.ShapeDtypeStruct((b, s, n_heads * HEAD_DIM), BF16),
        scratch_shapes=[pltpu.VMEM((2, heads, tq, tq), F32),
                        pltpu.VMEM((heads, tq, tq), BF16),
                        pltpu.VMEM((heads, tq, LANES), F32),
                        pltpu.VMEM((tq, gw), F32)],
        compiler_params=_compiler_params(("parallel", "parallel", "arbitrary"), blocks),
        name="sb_attention",
    )(proj, proj, proj, proj)


def _local_mixers_kernel(sink_ref,
                         q_ref, kp_ref, kc_ref, vp_ref, vc_ref, gb0_ref, gb1_ref,
                         x0_ref, x1_ref, xp0_ref, xp1_ref, c0_ref, c1_ref, cp0_ref, cp1_ref,
                         b0_ref, b1_ref, gc0_ref, gc1_ref, cw_ref,
                         u0_ref, u1_ref, v0_ref, v1_ref, gd0_ref, gd1_ref, vg_ref, ws_ref, bias_ref,
                         yb_ref, yc_ref, yd_ref, *, n_kv, group, n_sub):
    n = pl.program_id(1)

    def both(r0, r1):
        return jnp.concatenate([r0[...], r1[...]], axis=1).astype(F32)

    k_all = jnp.concatenate([kp_ref[...], kc_ref[...]], axis=0)
    v_all = jnp.concatenate([vp_ref[...], vc_ref[...]], axis=0)
    r = lax.broadcasted_iota(jnp.int32, (WINDOW, 2 * WINDOW), 0)
    c = lax.broadcasted_iota(jnp.int32, (WINDOW, 2 * WINDOW), 1)
    in_window = (c > r) & (c <= r + WINDOW)
    gate_b = _silu(both(gb0_ref, gb1_ref))
    for sub in range(n_sub):
        rows = slice(sub * WINDOW, (sub + 1) * WINDOW)
        valid = in_window & ((c >= WINDOW) | (n > 0)) if sub == 0 else in_window
        outs = []
        for kv in range(n_kv):
            k = k_all[sub * WINDOW:(sub + 2) * WINDOW, kv * HEAD_DIM:(kv + 1) * HEAD_DIM]
            v = v_all[sub * WINDOW:(sub + 2) * WINDOW, kv * HEAD_DIM:(kv + 1) * HEAD_DIM]
            for g in range(group):
                head = kv * group + g
                qg = q_ref[rows, head * HEAD_DIM:(head + 1) * HEAD_DIM]
                s = lax.dot_general(qg, k, (((1,), (1,)), ((), ())), preferred_element_type=F32)
                s = jnp.where(valid, s, MASKED_SCORE)
                sink = sink_ref[head]
                m = jnp.maximum(jnp.max(s, axis=-1, keepdims=True), sink)
                p = jnp.exp(s - m)
                denom = jnp.sum(p, axis=-1, keepdims=True) + jnp.exp(sink - m)
                outs.append(jnp.dot(p.astype(BF16), v, preferred_element_type=F32) / denom)
        o = jnp.concatenate(outs, axis=1)
        yb_ref[rows, :] = (o * gate_b[rows]).astype(yb_ref.dtype)

    h = both(c0_ref, c1_ref) * both(x0_ref, x1_ref)
    hp = both(cp0_ref, cp1_ref) * both(xp0_ref, xp1_ref)
    hp = jnp.where(n > 0, hp, 0.0)
    ext = jnp.concatenate([hp, h], axis=0)
    h1 = pltpu.roll(ext, 1, 0)[SUBLANES:]
    h2 = pltpu.roll(ext, 2, 0)[SUBLANES:]
    w = cw_ref[...]
    y = w[0:1] * h2 + w[1:2] * h1 + w[2:3] * h
    y = both(b0_ref, b1_ref) * y
    yc_ref[...] = (y * _silu(both(gc0_ref, gc1_ref))).astype(yc_ref.dtype)

    v = both(v0_ref, v1_ref)
    ms = jnp.mean(v * v, axis=-1, keepdims=True)
    vn = (v * lax.rsqrt(ms + EPS) * vg_ref[...]).astype(BF16)
    rr = lax.broadcasted_iota(jnp.int32, (GMLP_CHUNK, GMLP_CHUNK), 0)
    cc = lax.broadcasted_iota(jnp.int32, (GMLP_CHUNK, GMLP_CHUNK), 1)
    causal = cc <= rr
    gd = GMLP_GROUP_DIM
    n_groups = ws_ref.shape[0]
    n_chunks = v.shape[0] // GMLP_CHUNK
    mixed = []
    for g in range(n_groups):
        wsg = jnp.where(causal, ws_ref[g], 0.0).astype(BF16)
        rhs = jnp.concatenate(
            [vn[ch * GMLP_CHUNK:(ch + 1) * GMLP_CHUNK, g * gd:(g + 1) * gd] for ch in range(n_chunks)], axis=1)
        mixed.append(jnp.dot(wsg, rhs, preferred_element_type=F32))
    bias = bias_ref[...]
    chunks = []
    for ch in range(n_chunks):
        chunks.append(jnp.concatenate([mixed[g][:, ch * gd:(ch + 1) * gd] for g in range(n_groups)], axis=1) + bias)
    mix = jnp.concatenate(chunks, axis=0)
    y = both(u0_ref, u1_ref) * mix
    yd_ref[...] = (y * _silu(both(gd0_ref, gd1_ref))).astype(yd_ref.dtype)


def _local_mixers(proj, sinks, conv_w, v_gain, w_s, bias_tw, offs, n_heads, n_kv, width):
    qb, kb, vb, gb, xc, bc, cc, gc, ud, vd, gd = offs
    b, s, _ = proj.shape
    rows = min(LOCAL_ROWS_PER_STEP, s)
    cw = width // 2
    w_kv = n_kv * HEAD_DIM
    assert s % rows == 0 and rows % WINDOW == 0 and rows % GMLP_CHUNK == 0
    assert n_heads * HEAD_DIM == width and w_s.shape[0] * GMLP_GROUP_DIM == width

    def seg(off, w):
        col = _col_block(off, w)
        return pl.BlockSpec((None, rows, w), lambda bi, n: (bi, n, col))

    def seg_before(off, w, n_rows):
        col = _col_block(off, w)
        per_step = rows // n_rows
        return pl.BlockSpec((None, n_rows, w), lambda bi, n: (bi, jnp.maximum(n * per_step - 1, 0), col))

    def halves(off):
        return [seg(off, cw), seg(off + cw, cw)]

    def halves_before(off):
        return [seg_before(off, cw, SUBLANES), seg_before(off + cw, cw, SUBLANES)]

    def whole(shape):
        return pl.BlockSpec(shape, lambda bi, n: (0,) * len(shape))

    in_specs = ([pl.BlockSpec(memory_space=pltpu.SMEM),
                 seg(qb, width), seg_before(kb, w_kv, WINDOW), seg(kb, w_kv),
                 seg_before(vb, w_kv, WINDOW), seg(vb, w_kv)] + halves(gb)
                + halves(xc) + halves_before(xc) + halves(cc) + halves_before(cc) + halves(bc) + halves(gc)
                + [whole((CONV_WIDTH, width))]
                + halves(ud) + halves(vd) + halves(gd)
                + [whole((1, width)), whole(w_s.shape), whole((GMLP_CHUNK, width))])
    operands = [sinks] + [proj] * 7 + [proj] * 12 + [conv_w] + [proj] * 6 + [v_gain.reshape(1, width), w_s, bias_tw]
    assert len(operands) == len(in_specs)
    out_spec = pl.BlockSpec((None, rows, width), lambda bi, n: (bi, n, 0))
    out_shape = jax.ShapeDtypeStruct((b, s, width), BF16)
    blocks = (17 + 3) * rows * cw * 2 + 12 * rows * width * 4
    return pl.pallas_call(
        functools.partial(_local_mixers_kernel, n_kv=n_kv, group=n_heads // n_kv, n_sub=rows // WINDOW),
        grid=(b, s // rows),
        in_specs=in_specs,
        out_specs=[out_spec, out_spec, out_spec],
        out_shape=[out_shape, out_shape, out_shape],
        compiler_params=_compiler_params(("parallel", "arbitrary"), blocks),
        name="local_mixers",
    )(*operands)


def _residual_update(y_refs, w_ref, x_ref, width):
    acc = x_ref[...]
    for idx, y_ref in enumerate(y_refs):
        acc = acc + jnp.dot(y_ref[...], w_ref[idx * width:(idx + 1) * width, :], preferred_element_type=F32)
    return acc


def _outproj_kernel(ya_ref, yb_ref, yc_ref, yd_ref, w_ref, x_ref, o_ref, *, width):
    o_ref[...] = _residual_update((ya_ref, yb_ref, yc_ref, yd_ref), w_ref, x_ref, width)


def _outproj_prenorm_kernel(ya_ref, yb_ref, yc_ref, yd_ref, w_ref, x_ref, g_ref, o_ref, hg_ref, ssq_ref, *, width):
    acc = _residual_update((ya_ref, yb_ref, yc_ref, yd_ref), w_ref, x_ref, width)
    o_ref[...] = acc
    hg_ref[...] = (acc * g_ref[...]).astype(hg_ref.dtype)
    part = jnp.sum(acc * acc, axis=-1, keepdims=True)
    j = pl.program_id(1)

    @pl.when(j == 0)
    def _():
        ssq_ref[...] = part

    @pl.when(j > 0)
    def _():
        ssq_ref[...] += part


def _outproj(ys, w_all, layer, x, next_gain=None):
    m, d = x.shape
    width = ys[0].shape[1]
    tm = min(1024, m)
    tn = min(1024 if next_gain is None else 512, d)
    assert m % tm == 0 and d % tn == 0
    blocks = 4 * tm * width * 2 + 4 * width * tn * 2 + 3 * tm * tn * 4
    y_spec = pl.BlockSpec((tm, width), lambda i, j: (i, 0))
    tile = pl.BlockSpec((tm, tn), lambda i, j: (i, j))
    in_specs = [y_spec, y_spec, y_spec, y_spec,
                pl.BlockSpec((None, 4 * width, tn), lambda i, j: (layer, 0, j)), tile]
    if next_gain is None:
        return pl.pallas_call(
            functools.partial(_outproj_kernel, width=width),
            grid=(m // tm, d // tn),
            in_specs=in_specs,
            out_specs=tile,
            out_shape=jax.ShapeDtypeStruct((m, d), F32),
            compiler_params=_compiler_params(("parallel", "arbitrary"), blocks),
            name="outproj",
        )(*ys, w_all, x)
    return pl.pallas_call(
        functools.partial(_outproj_prenorm_kernel, width=width),
        grid=(m // tm, d // tn),
        in_specs=in_specs + [pl.BlockSpec((1, tn), lambda i, j: (0, j))],
        out_specs=[tile, tile, pl.BlockSpec((tm, 1), lambda i, j: (i, 0))],
        out_shape=[jax.ShapeDtypeStruct((m, d), F32), jax.ShapeDtypeStruct((m, d), BF16),
                   jax.ShapeDtypeStruct((m, 1), F32)],
        compiler_params=_compiler_params(("parallel", "arbitrary"), blocks + tm * tn * 2),
        name="outproj_prenorm",
    )(*ys, w_all, x, next_gain.reshape(1, d))


def kernel(x, norm_gain, w_in, conv_w, gmlp_v_gain, gmlp_w_s, gmlp_b_s, swa_sinks, w_out, final_gain):
    batch, seq, d_model = x.shape
    depth = w_in.shape[0]
    m = batch * seq
    w_a = w_b = w_c = w_d = d_model // 4
    n_sb_heads = w_a // HEAD_DIM
    n_swa_heads = w_b // HEAD_DIM
    n_swa_kv = max(1, n_swa_heads // 4)
    w_kv = n_swa_kv * HEAD_DIM
    n_groups = gmlp_w_s.shape[1]
    assert n_swa_heads == swa_sinks.shape[1] and w_d == n_groups * GMLP_GROUP_DIM

    widths = (w_a, w_a, w_a, w_a, w_b, w_kv, w_kv, w_b, w_c, w_c, w_c, w_c, w_d, w_d, w_d)
    offs = [0]
    for wd in widths:
        offs.append(offs[-1] + wd)
    (qa, ka, va, ga, qb, kb, vb, gb, xc, bc, cc, gc, ud, vd, gd) = offs[:-1]
    n_proj = offs[-1]
    assert n_proj == w_in.shape[2]

    scale = HEAD_DIM ** -0.5
    col = jnp.arange(n_proj)
    col_scale = jnp.where((col >= qa) & (col < qa + w_a), scale * LOG2_E,
                          jnp.where((col >= qb) & (col < qb + w_b), scale, 1.0))
    col_scale = col_scale.astype(F32).reshape(1, n_proj)

    w_out_bf16 = w_out.astype(BF16)
    h = x.reshape(m, d_model)
    hn, row_ssq = _rmsnorm(h, norm_gain[0], BF16), None
    for l in range(depth):
        proj = _inproj(hn, w_in, l, col_scale, row_ssq).reshape(batch, seq, n_proj)
        y_a = _sb_attention(proj, qa, ka, va, ga, n_sb_heads)
        bias_tw = jnp.repeat(gmlp_b_s[l].T, GMLP_GROUP_DIM, axis=1)
        y_b, y_c, y_d = _local_mixers(proj, swa_sinks[l], conv_w[l], gmlp_v_gain[l], gmlp_w_s[l], bias_tw,
                                      (qb, kb, vb, gb, xc, bc, cc, gc, ud, vd, gd), n_swa_heads, n_swa_kv, w_b)
        ys = [y.reshape(m, -1) for y in (y_a, y_b, y_c, y_d)]
        if l + 1 < depth:
            h, hn, row_ssq = _outproj(ys, w_out_bf16, l, h, norm_gain[l + 1])
        else:
            h = _outproj(ys, w_out_bf16, l, h)
    out = _rmsnorm(h, final_gain, F32)
    return out.reshape(batch, seq, d_model)
```

```python
import functools

import jax
import jax.numpy as jnp
from jax import lax
from jax.experimental import pallas as pl
from jax.experimental.pallas import tpu as pltpu

HEAD_DIM = 128
GMLP_GROUP_DIM = 128
GMLP_CHUNK = 128
WINDOW = 128
CONV_WIDTH = 3
EPS = 1e-6
LOG2_E = 1.4426950408889634
EXP2_CLAMP = 64.0
F32_EXP2_UNDERFLOW = 160.0

LANES = 128
SUBLANES = 8
V7X_VMEM_BYTES = 64 * 1024 * 1024
VMEM_SPILL_ALLOWANCE = 6 * 1024 * 1024
MASKED_SCORE = -1e30
SB_HEADS_PER_STEP = 8
LOCAL_ROWS_PER_STEP = 512

F32 = jnp.float32
BF16 = jnp.bfloat16


def _compiler_params(semantics, block_bytes):
    limit = min(2 * block_bytes + VMEM_SPILL_ALLOWANCE, V7X_VMEM_BYTES - 4 * 1024 * 1024)
    return pltpu.CompilerParams(dimension_semantics=semantics, vmem_limit_bytes=int(limit))


def _col_block(offset, width):
    assert offset % width == 0, (offset, width)
    return offset // width


def _silu(g):
    return g * jax.nn.sigmoid(g)


def _rmsnorm_kernel(x_ref, g_ref, o_ref):
    x = x_ref[...]
    ms = jnp.mean(x * x, axis=-1, keepdims=True)
    o_ref[...] = (x * lax.rsqrt(ms + EPS) * g_ref[...]).astype(o_ref.dtype)


def _rmsnorm(x, gain, out_dtype):
    m, d = x.shape
    tm = min(256, m)
    return pl.pallas_call(
        _rmsnorm_kernel,
        grid=(m // tm,),
        in_specs=[pl.BlockSpec((tm, d), lambda i: (i, 0)),
                  pl.BlockSpec((1, d), lambda i: (0, 0))],
        out_specs=pl.BlockSpec((tm, d), lambda i: (i, 0)),
        out_shape=jax.ShapeDtypeStruct((m, d), out_dtype),
        compiler_params=_compiler_params(("parallel",), tm * d * 12),
        name="rmsnorm",
    )(x, gain.reshape(1, d))


def _inproj_kernel(h_ref, w_ref, s_ref, o_ref):
    acc = jnp.dot(h_ref[...], w_ref[...].astype(BF16), preferred_element_type=F32)
    o_ref[...] = (acc * s_ref[...]).astype(o_ref.dtype)


def _inproj_rowscaled_kernel(h_ref, ssq_ref, w_ref, s_ref, o_ref, *, d_model):
    acc = jnp.dot(h_ref[...], w_ref[...].astype(BF16), preferred_element_type=F32)
    inv_rms = lax.rsqrt(ssq_ref[...] * (1.0 / d_model) + EPS)
    o_ref[...] = (acc * inv_rms * s_ref[...]).astype(o_ref.dtype)


def _inproj(hn, w_all, layer, col_scale, row_ssq=None):
    m, d = hn.shape
    n = w_all.shape[2]
    tm = min(1024, m)
    tn = 768 if n % 768 == 0 else 512
    assert m % tm == 0 and n % tn == 0
    blocks = tm * d * 2 + d * tn * (4 + 2) + tm * tn * (2 + 4)
    h_spec = pl.BlockSpec((tm, d), lambda i, j: (i, 0))
    w_spec = pl.BlockSpec((None, d, tn), lambda i, j: (layer, 0, j))
    s_spec = pl.BlockSpec((1, tn), lambda i, j: (0, j))
    if row_ssq is None:
        body, in_specs, operands = _inproj_kernel, [h_spec, w_spec, s_spec], (hn, w_all, col_scale)
    else:
        body = functools.partial(_inproj_rowscaled_kernel, d_model=d)
        in_specs = [h_spec, pl.BlockSpec((tm, 1), lambda i, j: (i, 0)), w_spec, s_spec]
        operands = (hn, row_ssq, w_all, col_scale)
    return pl.pallas_call(
        body,
        grid=(m // tm, n // tn),
        in_specs=in_specs,
        out_specs=pl.BlockSpec((tm, tn), lambda i, j: (i, j)),
        out_shape=jax.ShapeDtypeStruct((m, n), BF16),
        compiler_params=_compiler_params(("parallel", "arbitrary"), blocks),
        name="inproj",
    )(*operands)


def _sb_attn_kernel(q_ref, k_ref, v_ref, g_ref, o_ref, z_scr, w_scr, carry_scr, acc_scr, *, tq, heads):
    i = pl.program_id(2)
    dh = HEAD_DIM
    hds = range(heads)
    row = lax.broadcasted_iota(jnp.int32, (tq, tq), 0)
    col = lax.broadcasted_iota(jnp.int32, (tq, tq), 1)
    strictly_causal = col < row
    later_keys = ((row > col) | (col == tq - 1)).astype(BF16)
    last_lane = lax.broadcasted_iota(jnp.int32, (tq, LANES), 1) == LANES - 1

    def key_rows(kb):
        return pl.ds(pl.multiple_of(kb * tq, tq), tq)

    def head_cols(hd):
        return slice(hd * dh, (hd + 1) * dh)

    def scores(kb, hd):
        return lax.dot_general(q_ref[:, head_cols(hd)], k_ref[key_rows(kb), head_cols(hd)],
                               (((1,), (1,)), ((), ())), preferred_element_type=F32)

    def softplus_terms(z, carry, diagonal):
        p = jnp.maximum(jnp.log2(1.0 + jnp.exp2(jnp.minimum(z, EXP2_CLAMP))), z)
        if diagonal:
            p = jnp.where(strictly_causal, p, 0.0)
        p = jnp.concatenate([p[:, :tq - LANES], p[:, tq - LANES:] + carry], axis=1)
        return p.astype(BF16), z - p

    def finish(u, cs, hd, diagonal):
        cs_tail = cs[:, tq - LANES:]
        carry_scr[hd] = jnp.where(last_lane, cs_tail, 0.0)
        cs = jnp.concatenate([cs[:, :tq - LANES], jnp.where(last_lane, 0.0, cs_tail)], axis=1)
        w = jnp.exp2(u - cs)
        if diagonal:
            w = jnp.where(strictly_causal, w, 0.0)
        w_scr[hd] = w.astype(BF16)

    def apply_weights(kb):
        for hd in hds:
            acc_scr[:, head_cols(hd)] += jnp.dot(w_scr[hd], v_ref[key_rows(kb), head_cols(hd)],
                                                 preferred_element_type=F32)

    acc_scr[...] = jnp.zeros_like(acc_scr)
    zs = [scores(i, hd) for hd in hds]
    zero_carry = jnp.zeros((tq, LANES), F32)
    terms = [softplus_terms(zs[hd], zero_carry, True) for hd in hds]
    nxt = jnp.maximum(i - 1, 0)
    for hd in hds:
        p_bf16, u = terms[hd]
        cs = jnp.dot(p_bf16, later_keys, preferred_element_type=F32)
        finish(u, cs, hd, True)
        z_scr[0, hd] = scores(nxt, hd)

    def step(kb, z_in, z_out):
        apply_weights(kb + 1)
        nxt = jnp.maximum(kb - 1, 0)
        for hd in hds:
            z_scr[z_out, hd] = scores(nxt, hd)
        terms = []
        for hd in hds:
            p_bf16, u = softplus_terms(z_scr[z_in, hd], carry_scr[hd], False)
            z_scr[z_in, hd] = u
            terms.append(p_bf16)
        for hd in hds:
            cs = jnp.dot(terms[hd], later_keys, preferred_element_type=F32)
            finish(z_scr[z_in, hd], cs, hd, False)

    def stick_used_up():
        swept = carry_scr[0]
        for hd in range(1, heads):
            swept = jnp.minimum(swept, carry_scr[hd])
        return jnp.max(jnp.min(swept, axis=0, keepdims=True)) >= F32_EXP2_UNDERFLOW

    n_pairs = i // 2

    def two_steps(state):
        m, _, _ = state
        kb = i - 1 - 2 * m
        step(kb, 0, 1)
        used_up = stick_used_up()

        @pl.when(jnp.logical_not(used_up))
        def _():
            step(kb - 1, 1, 0)

        return m + 1, jnp.where(used_up, kb, kb - 1), jnp.logical_or(used_up, stick_used_up())

    pairs_done, last_block, used_up = lax.while_loop(
        lambda st: jnp.logical_and(st[0] < n_pairs, jnp.logical_not(st[2])),
        two_steps, (jnp.int32(0), i, stick_used_up()))
    odd_step = jnp.logical_and(jnp.logical_and(i % 2 == 1, pairs_done == n_pairs), jnp.logical_not(used_up))

    @pl.when(odd_step)
    def _():
        step(0, 0, 1)

    apply_weights(jnp.where(odd_step, 0, last_block))
    o_ref[...] = (acc_scr[...] * _silu(g_ref[...].astype(F32))).astype(o_ref.dtype)


def _sb_attention(proj, q_off, k_off, v_off, g_off, n_heads):
    b, s, _ = proj.shape
    tq = min(256, s)
    heads = min(SB_HEADS_PER_STEP, n_heads)
    assert s % tq == 0 and tq % LANES == 0 and n_heads % heads == 0
    gw = heads * HEAD_DIM
    qb, kb, vb, gb = (_col_block(o, gw) for o in (q_off, k_off, v_off, g_off))
    blocks = 3 * tq * gw * 2 + s * gw * 2 + heads * 6 * tq * tq * 4
    return pl.pallas_call(
        functools.partial(_sb_attn_kernel, tq=tq, heads=heads),
        grid=(b, n_heads // heads, s // tq),
        in_specs=[pl.BlockSpec((None, tq, gw), lambda bi, h, i: (bi, i, qb + h)),
                  pl.BlockSpec((None, s, gw), lambda bi, h, i: (bi, 0, kb + h), pipeline_mode=pl.Buffered(1)),
                  pl.BlockSpec((None, s, gw), lambda bi, h, i: (bi, 0, vb + h), pipeline_mode=pl.Buffered(1)),
                  pl.BlockSpec((None, tq, gw), lambda bi, h, i: (bi, i, gb + h))],
        out_specs=pl.BlockSpec((None, tq, gw), lambda bi, h, i: (bi, i, h)),
        out_shape=jax.ShapeDtypeStruct((b, s, n_heads * HEAD_DIM), BF16),
        scratch_shapes=[pltpu.VMEM((2, heads, tq, tq), F32),
                        pltpu.VMEM((heads, tq, tq), BF16),
                        pltpu.VMEM((heads, tq, LANES), F32),
                        pltpu.VMEM((tq, gw), F32)],
        compiler_params=_compiler_params(("parallel", "parallel", "arbitrary"), blocks),
        name="sb_attention",
    )(proj, proj, proj, proj)


def _local_mixers_kernel(sink_ref,
                         q_ref, kp_ref, kc_ref, vp_ref, vc_ref, gb0_ref, gb1_ref,
                         x0_ref, x1_ref, xp0_ref, xp1_ref, c0_ref, c1_ref, cp0_ref, cp1_ref,
                         b0_ref, b1_ref, gc0_ref, gc1_ref, cw_ref,
                         u0_ref, u1_ref, v0_ref, v1_ref, gd0_ref, gd1_ref, vg_ref, ws_ref, bias_ref,
                         yb_ref, yc_ref, yd_ref, *, n_kv, group, n_sub):
    n = pl.program_id(1)

    def both(r0, r1):
        return jnp.concatenate([r0[...], r1[...]], axis=1).astype(F32)

    k_all = jnp.concatenate([kp_ref[...], kc_ref[...]], axis=0)
    v_all = jnp.concatenate([vp_ref[...], vc_ref[...]], axis=0)
    r = lax.broadcasted_iota(jnp.int32, (WINDOW, 2 * WINDOW), 0)
    c = lax.broadcasted_iota(jnp.int32, (WINDOW, 2 * WINDOW), 1)
    in_window = (c > r) & (c <= r + WINDOW)
    gate_b = _silu(both(gb0_ref, gb1_ref))
    for sub in range(n_sub):
        rows = slice(sub * WINDOW, (sub + 1) * WINDOW)
        valid = in_window & ((c >= WINDOW) | (n > 0)) if sub == 0 else in_window
        outs = []
        for kv in range(n_kv):
            k = k_all[sub * WINDOW:(sub + 2) * WINDOW, kv * HEAD_DIM:(kv + 1) * HEAD_DIM]
            v = v_all[sub * WINDOW:(sub + 2) * WINDOW, kv * HEAD_DIM:(kv + 1) * HEAD_DIM]
            for g in range(group):
                head = kv * group + g
                qg = q_ref[rows, head * HEAD_DIM:(head + 1) * HEAD_DIM]
                s = lax.dot_general(qg, k, (((1,), (1,)), ((), ())), preferred_element_type=F32)
                s = jnp.where(valid, s, MASKED_SCORE)
                sink = sink_ref[head]
                m = jnp.maximum(jnp.max(s, axis=-1, keepdims=True), sink)
                p = jnp.exp(s - m)
                denom = jnp.sum(p, axis=-1, keepdims=True) + jnp.exp(sink - m)
                outs.append(jnp.dot(p.astype(BF16), v, preferred_element_type=F32) / denom)
        o = jnp.concatenate(outs, axis=1)
        yb_ref[rows, :] = (o * gate_b[rows]).astype(yb_ref.dtype)

    h = both(c0_ref, c1_ref) * both(x0_ref, x1_ref)
    hp = both(cp0_ref, cp1_ref) * both(xp0_ref, xp1_ref)
    hp = jnp.where(n > 0, hp, 0.0)
    ext = jnp.concatenate([hp, h], axis=0)
    h1 = pltpu.roll(ext, 1, 0)[SUBLANES:]
    h2 = pltpu.roll(ext, 2, 0)[SUBLANES:]
    w = cw_ref[...]
    y = w[0:1] * h2 + w[1:2] * h1 + w[2:3] * h
    y = both(b0_ref, b1_ref) * y
    yc_ref[...] = (y * _silu(both(gc0_ref, gc1_ref))).astype(yc_ref.dtype)

    v = both(v0_ref, v1_ref)
    ms = jnp.mean(v * v, axis=-1, keepdims=True)
    vn = (v * lax.rsqrt(ms + EPS) * vg_ref[...]).astype(BF16)
    rr = lax.broadcasted_iota(jnp.int32, (GMLP_CHUNK, GMLP_CHUNK), 0)
    cc = lax.broadcasted_iota(jnp.int32, (GMLP_CHUNK, GMLP_CHUNK), 1)
    causal = cc <= rr
    gd = GMLP_GROUP_DIM
    n_groups = ws_ref.shape[0]
    n_chunks = v.shape[0] // GMLP_CHUNK
    mixed = []
    for g in range(n_groups):
        wsg = jnp.where(causal, ws_ref[g], 0.0).astype(BF16)
        rhs = jnp.concatenate(
            [vn[ch * GMLP_CHUNK:(ch + 1) * GMLP_CHUNK, g * gd:(g + 1) * gd] for ch in range(n_chunks)], axis=1)
        mixed.append(jnp.dot(wsg, rhs, preferred_element_type=F32))
    bias = bias_ref[...]
    chunks = []
    for ch in range(n_chunks):
        chunks.append(jnp.concatenate([mixed[g][:, ch * gd:(ch + 1) * gd] for g in range(n_groups)], axis=1) + bias)
    mix = jnp.concatenate(chunks, axis=0)
    y = both(u0_ref, u1_ref) * mix
    yd_ref[...] = (y * _silu(both(gd0_ref, gd1_ref))).astype(yd_ref.dtype)


def _local_mixers(proj, sinks, conv_w, v_gain, w_s, bias_tw, offs, n_heads, n_kv, width):
    qb, kb, vb, gb, xc, bc, cc, gc, ud, vd, gd = offs
    b, s, _ = proj.shape
    rows = min(LOCAL_ROWS_PER_STEP, s)
    cw = width // 2
    w_kv = n_kv * HEAD_DIM
    assert s % rows == 0 and rows % WINDOW == 0 and rows % GMLP_CHUNK == 0
    assert n_heads * HEAD_DIM == width and w_s.shape[0] * GMLP_GROUP_DIM == width

    def seg(off, w):
        col = _col_block(off, w)
        return pl.BlockSpec((None, rows, w), lambda bi, n: (bi, n, col))

    def seg_before(off, w, n_rows):
        col = _col_block(off, w)
        per_step = rows // n_rows
        return pl.BlockSpec((None, n_rows, w), lambda bi, n: (bi, jnp.maximum(n * per_step - 1, 0), col))

    def halves(off):
        return [seg(off, cw), seg(off + cw, cw)]

    def halves_before(off):
        return [seg_before(off, cw, SUBLANES), seg_before(off + cw, cw, SUBLANES)]

    def whole(shape):
        return pl.BlockSpec(shape, lambda bi, n: (0,) * len(shape))

    in_specs = ([pl.BlockSpec(memory_space=pltpu.SMEM),
                 seg(qb, width), seg_before(kb, w_kv, WINDOW), seg(kb, w_kv),
                 seg_before(vb, w_kv, WINDOW), seg(vb, w_kv)] + halves(gb)
                + halves(xc) + halves_before(xc) + halves(cc) + halves_before(cc) + halves(bc) + halves(gc)
                + [whole((CONV_WIDTH, width))]
                + halves(ud) + halves(vd) + halves(gd)
                + [whole((1, width)), whole(w_s.shape), whole((GMLP_CHUNK, width))])
    operands = [sinks] + [proj] * 7 + [proj] * 12 + [conv_w] + [proj] * 6 + [v_gain.reshape(1, width), w_s, bias_tw]
    assert len(operands) == len(in_specs)
    out_spec = pl.BlockSpec((None, rows, width), lambda bi, n: (bi, n, 0))
    out_shape = jax.ShapeDtypeStruct((b, s, width), BF16)
    blocks = (17 + 3) * rows * cw * 2 + 12 * rows * width * 4
    return pl.pallas_call(
        functools.partial(_local_mixers_kernel, n_kv=n_kv, group=n_heads // n_kv, n_sub=rows // WINDOW),
        grid=(b, s // rows),
        in_specs=in_specs,
        out_specs=[out_spec, out_spec, out_spec],
        out_shape=[out_shape, out_shape, out_shape],
        compiler_params=_compiler_params(("parallel", "arbitrary"), blocks),
        name="local_mixers",
    )(*operands)


def _residual_update(y_refs, w_ref, x_ref, width):
    acc = x_ref[...]
    for idx, y_ref in enumerate(y_refs):
        acc = acc + jnp.dot(y_ref[...], w_ref[idx * width:(idx + 1) * width, :], preferred_element_type=F32)
    return acc


def _outproj_kernel(ya_ref, yb_ref, yc_ref, yd_ref, w_ref, x_ref, o_ref, *, width):
    o_ref[...] = _residual_update((ya_ref, yb_ref, yc_ref, yd_ref), w_ref, x_ref, width)


def _outproj_prenorm_kernel(ya_ref, yb_ref, yc_ref, yd_ref, w_ref, x_ref, g_ref, o_ref, hg_ref, ssq_ref, *, width):
    acc = _residual_update((ya_ref, yb_ref, yc_ref, yd_ref), w_ref, x_ref, width)
    o_ref[...] = acc
    hg_ref[...] = (acc * g_ref[...]).astype(hg_ref.dtype)
    part = jnp.sum(acc * acc, axis=-1, keepdims=True)
    j = pl.program_id(1)

    @pl.when(j == 0)
    def _():
        ssq_ref[...] = part

    @pl.when(j > 0)
    def _():
        ssq_ref[...] += part


def _outproj(ys, w_all, layer, x, next_gain=None):
    m, d = x.shape
    width = ys[0].shape[1]
    tm = min(1024, m)
    tn = min(1024 if next_gain is None else 512, d)
    assert m % tm == 0 and d % tn == 0
    blocks = 4 * tm * width * 2 + 4 * width * tn * 2 + 3 * tm * tn * 4
    y_spec = pl.BlockSpec((tm, width), lambda i, j: (i, 0))
    tile = pl.BlockSpec((tm, tn), lambda i, j: (i, j))
    in_specs = [y_spec, y_spec, y_spec, y_spec,
                pl.BlockSpec((None, 4 * width, tn), lambda i, j: (layer, 0, j)), tile]
    if next_gain is None:
        return pl.pallas_call(
            functools.partial(_outproj_kernel, width=width),
            grid=(m // tm, d // tn),
            in_specs=in_specs,
            out_specs=tile,
            out_shape=jax.ShapeDtypeStruct((m, d), F32),
            compiler_params=_compiler_params(("parallel", "arbitrary"), blocks),
            name="outproj",
        )(*ys, w_all, x)
    return pl.pallas_call(
        functools.partial(_outproj_prenorm_kernel, width=width),
        grid=(m // tm, d // tn),
        in_specs=in_specs + [pl.BlockSpec((1, tn), lambda i, j: (0, j))],
        out_specs=[tile, tile, pl.BlockSpec((tm, 1), lambda i, j: (i, 0))],
        out_shape=[jax.ShapeDtypeStruct((m, d), F32), jax.ShapeDtypeStruct((m, d), BF16),
                   jax.ShapeDtypeStruct((m, 1), F32)],
        compiler_params=_compiler_params(("parallel", "arbitrary"), blocks + tm * tn * 2),
        name="outproj_prenorm",
    )(*ys, w_all, x, next_gain.reshape(1, d))


def kernel(x, norm_gain, w_in, conv_w, gmlp_v_gain, gmlp_w_s, gmlp_b_s, swa_sinks, w_out, final_gain):
    batch, seq, d_model = x.shape
    depth = w_in.shape[0]
    m = batch * seq
    w_a = w_b = w_c = w_d = d_model // 4
    n_sb_heads = w_a // HEAD_DIM
    n_swa_heads = w_b // HEAD_DIM
    n_swa_kv = max(1, n_swa_heads // 4)
    w_kv = n_swa_kv * HEAD_DIM
    n_groups = gmlp_w_s.shape[1]
    assert n_swa_heads == swa_sinks.shape[1] and w_d == n_groups * GMLP_GROUP_DIM

    widths = (w_a, w_a, w_a, w_a, w_b, w_kv, w_kv, w_b, w_c, w_c, w_c, w_c, w_d, w_d, w_d)
    offs = [0]
    for wd in widths:
        offs.append(offs[-1] + wd)
    (qa, ka, va, ga, qb, kb, vb, gb, xc, bc, cc, gc, ud, vd, gd) = offs[:-1]
    n_proj = offs[-1]
    assert n_proj == w_in.shape[2]

    scale = HEAD_DIM ** -0.5
    col = jnp.arange(n_proj)
    col_scale = jnp.where((col >= qa) & (col < qa + w_a), scale * LOG2_E,
                          jnp.where((col >= qb) & (col < qb + w_b), scale, 1.0))
    col_scale = col_scale.astype(F32).reshape(1, n_proj)

    w_out_bf16 = w_out.astype(BF16)
    h = x.reshape(m, d_model)
    hn, row_ssq = _rmsnorm(h, norm_gain[0], BF16), None
    for l in range(depth):
        proj = _inproj(hn, w_in, l, col_scale, row_ssq).reshape(batch, seq, n_proj)
        y_a = _sb_attention(proj, qa, ka, va, ga, n_sb_heads)
        bias_tw = jnp.repeat(gmlp_b_s[l].T, GMLP_GROUP_DIM, axis=1)
        y_b, y_c, y_d = _local_mixers(proj, swa_sinks[l], conv_w[l], gmlp_v_gain[l], gmlp_w_s[l], bias_tw,
                                      (qb, kb, vb, gb, xc, bc, cc, gc, ud, vd, gd), n_swa_heads, n_swa_kv, w_b)
        ys = [y.reshape(m, -1) for y in (y_a, y_b, y_c, y_d)]
        if l + 1 < depth:
            h, hn, row_ssq = _outproj(ys, w_out_bf16, l, h, norm_gain[l + 1])
        else:
            h = _outproj(ys, w_out_bf16, l, h)
    out = _rmsnorm(h, final_gain, F32)
    return out.reshape(batch, seq, d_model)
```

```python
import functools

import jax
import jax.numpy as jnp
from jax import lax
from jax.experimental import pallas as pl
from jax.experimental.pallas import tpu as pltpu

HEAD_DIM = 128
GMLP_GROUP_DIM = 128
GMLP_CHUNK = 128
WINDOW = 128
CONV_WIDTH = 3
EPS = 1e-6
LOG2_E = 1.4426950408889634
EXP2_CLAMP = 64.0
F32_EXP2_UNDERFLOW = 160.0

LANES = 128
SUBLANES = 8
V7X_VMEM_BYTES = 64 * 1024 * 1024
VMEM_SPILL_ALLOWANCE = 6 * 1024 * 1024
MASKED_SCORE = -1e30
SB_HEADS_PER_STEP = 8
LOCAL_ROWS_PER_STEP = 512

F32 = jnp.float32
BF16 = jnp.bfloat16


def _compiler_params(semantics, block_bytes):
    limit = min(2 * block_bytes + VMEM_SPILL_ALLOWANCE, V7X_VMEM_BYTES - 4 * 1024 * 1024)
    return pltpu.CompilerParams(dimension_semantics=semantics, vmem_limit_bytes=int(limit))


def _col_block(offset, width):
    assert offset % width == 0, (offset, width)
    return offset // width


def _silu(g):
    return g * jax.nn.sigmoid(g)


def _rmsnorm_kernel(x_ref, g_ref, o_ref):
    x = x_ref[...]
    ms = jnp.mean(x * x, axis=-1, keepdims=True)
    o_ref[...] = (x * lax.rsqrt(ms + EPS) * g_ref[...]).astype(o_ref.dtype)


def _rmsnorm(x, gain, out_dtype):
    m, d = x.shape
    tm = min(512, m)
    return pl.pallas_call(
        _rmsnorm_kernel,
        grid=(m // tm,),
        in_specs=[pl.BlockSpec((tm, d), lambda i: (i, 0)),
                  pl.BlockSpec((1, d), lambda i: (0, 0))],
        out_specs=pl.BlockSpec((tm, d), lambda i: (i, 0)),
        out_shape=jax.ShapeDtypeStruct((m, d), out_dtype),
        compiler_params=_compiler_params(("parallel",), tm * d * 12),
        name="rmsnorm",
    )(x, gain.reshape(1, d))


def _inproj_kernel(h_ref, w_ref, s_ref, o_ref):
    acc = jnp.dot(h_ref[...], w_ref[...].astype(BF16), preferred_element_type=F32)
    o_ref[...] = (acc * s_ref[...]).astype(o_ref.dtype)


def _inproj_rowscaled_kernel(h_ref, ssq_ref, w_ref, s_ref, o_ref, *, d_model):
    acc = jnp.dot(h_ref[...], w_ref[...].astype(BF16), preferred_element_type=F32)
    inv_rms = lax.rsqrt(ssq_ref[...] * (1.0 / d_model) + EPS)
    o_ref[...] = (acc * inv_rms * s_ref[...]).astype(o_ref.dtype)


def _inproj(hn, w_all, layer, col_scale, row_ssq=None):
    m, d = hn.shape
    n = w_all.shape[2]
    tm = min(1024, m)
    tn = 768 if n % 768 == 0 else 512
    assert m % tm == 0 and n % tn == 0
    blocks = tm * d * 2 + d * tn * (4 + 2) + tm * tn * (2 + 4)
    h_spec = pl.BlockSpec((tm, d), lambda i, j: (i, 0))
    w_spec = pl.BlockSpec((None, d, tn), lambda i, j: (layer, 0, j))
    s_spec = pl.BlockSpec((1, tn), lambda i, j: (0, j))
    if row_ssq is None:
        body, in_specs, operands = _inproj_kernel, [h_spec, w_spec, s_spec], (hn, w_all, col_scale)
    else:
        body = functools.partial(_inproj_rowscaled_kernel, d_model=d)
        in_specs = [h_spec, pl.BlockSpec((tm, 1), lambda i, j: (i, 0)), w_spec, s_spec]
        operands = (hn, row_ssq, w_all, col_scale)
    return pl.pallas_call(
        body,
        grid=(m // tm, n // tn),
        in_specs=in_specs,
        out_specs=pl.BlockSpec((tm, tn), lambda i, j: (i, j)),
        out_shape=jax.ShapeDtypeStruct((m, n), BF16),
        compiler_params=_compiler_params(("parallel", "arbitrary"), blocks),
        name="inproj",
    )(*operands)


def _sb_attn_kernel(q_ref, k_ref, v_ref, g_ref, o_ref, z_scr, w_scr, carry_scr, acc_scr, *, tq, heads):
    i = pl.program_id(2)
    dh = HEAD_DIM
    hds = range(heads)
    row = lax.broadcasted_iota(jnp.int32, (tq, tq), 0)
    col = lax.broadcasted_iota(jnp.int32, (tq, tq), 1)
    strictly_causal = col < row
    later_keys = ((row > col) | (col == tq - 1)).astype(BF16)
    last_lane = lax.broadcasted_iota(jnp.int32, (tq, LANES), 1) == LANES - 1

    def key_rows(kb):
        return pl.ds(pl.multiple_of(kb * tq, tq), tq)

    def head_cols(hd):
        return slice(hd * dh, (hd + 1) * dh)

    def scores(kb, hd):
        return lax.dot_general(q_ref[:, head_cols(hd)], k_ref[key_rows(kb), head_cols(hd)],
                               (((1,), (1,)), ((), ())), preferred_element_type=F32)

    def softplus_terms(z, carry, diagonal):
        p = jnp.maximum(jnp.log2(1.0 + jnp.exp2(jnp.minimum(z, EXP2_CLAMP))), z)
        if diagonal:
            p = jnp.where(strictly_causal, p, 0.0)
        p = jnp.concatenate([p[:, :tq - LANES], p[:, tq - LANES:] + carry], axis=1)
        return p.astype(BF16), z - p

    def finish(u, cs, hd, diagonal):
        cs_tail = cs[:, tq - LANES:]
        carry_scr[hd] = jnp.where(last_lane, cs_tail, 0.0)
        cs = jnp.concatenate([cs[:, :tq - LANES], jnp.where(last_lane, 0.0, cs_tail)], axis=1)
        w = jnp.exp2(u - cs)
        if diagonal:
            w = jnp.where(strictly_causal, w, 0.0)
        w_scr[hd] = w.astype(BF16)

    def apply_weights(kb):
        for hd in hds:
            acc_scr[:, head_cols(hd)] += jnp.dot(w_scr[hd], v_ref[key_rows(kb), head_cols(hd)],
                                                 preferred_element_type=F32)

    acc_scr[...] = jnp.zeros_like(acc_scr)
    zs = [scores(i, hd) for hd in hds]
    zero_carry = jnp.zeros((tq, LANES), F32)
    terms = [softplus_terms(zs[hd], zero_carry, True) for hd in hds]
    nxt = jnp.maximum(i - 1, 0)
    for hd in hds:
        p_bf16, u = terms[hd]
        cs = jnp.dot(p_bf16, later_keys, preferred_element_type=F32)
        finish(u, cs, hd, True)
        z_scr[0, hd] = scores(nxt, hd)

    def step(kb, z_in, z_out):
        apply_weights(kb + 1)
        nxt = jnp.maximum(kb - 1, 0)
        for hd in hds:
            z_scr[z_out, hd] = scores(nxt, hd)
        terms = []
        for hd in hds:
            p_bf16, u = softplus_terms(z_scr[z_in, hd], carry_scr[hd], False)
            z_scr[z_in, hd] = u
            terms.append(p_bf16)
        for hd in hds:
            cs = jnp.dot(terms[hd], later_keys, preferred_element_type=F32)
            finish(z_scr[z_in, hd], cs, hd, False)

    def stick_used_up():
        swept = carry_scr[0]
        for hd in range(1, heads):
            swept = jnp.minimum(swept, carry_scr[hd])
        return jnp.max(jnp.min(swept, axis=0, keepdims=True)) >= F32_EXP2_UNDERFLOW

    n_pairs = i // 2

    def two_steps(state):
        m, _, _ = state
        kb = i - 1 - 2 * m
        step(kb, 0, 1)
        used_up = stick_used_up()

        @pl.when(jnp.logical_not(used_up))
        def _():
            step(kb - 1, 1, 0)

        return m + 1, jnp.where(used_up, kb, kb - 1), jnp.logical_or(used_up, stick_used_up())

    pairs_done, last_block, used_up = lax.while_loop(
        lambda st: jnp.logical_and(st[0] < n_pairs, jnp.logical_not(st[2])),
        two_steps, (jnp.int32(0), i, stick_used_up()))
    odd_step = jnp.logical_and(jnp.logical_and(i % 2 == 1, pairs_done == n_pairs), jnp.logical_not(used_up))

    @pl.when(odd_step)
    def _():
        step(0, 0, 1)

    apply_weights(jnp.where(odd_step, 0, last_block))
    o_ref[...] = (acc_scr[...] * _silu(g_ref[...].astype(F32))).astype(o_ref.dtype)


def _sb_attention(proj, q_off, k_off, v_off, g_off, n_heads):
    b, s, _ = proj.shape
    tq = min(256, s)
    heads = min(SB_HEADS_PER_STEP, n_heads)
    assert s % tq == 0 and tq % LANES == 0 and n_heads % heads == 0
    gw = heads * HEAD_DIM
    qb, kb, vb, gb = (_col_block(o, gw) for o in (q_off, k_off, v_off, g_off))
    blocks = 3 * tq * gw * 2 + s * gw * 2 + heads * 6 * tq * tq * 4
    return pl.pallas_call(
        functools.partial(_sb_attn_kernel, tq=tq, heads=heads),
        grid=(b, n_heads // heads, s // tq),
        in_specs=[pl.BlockSpec((None, tq, gw), lambda bi, h, i: (bi, i, qb + h)),
                  pl.BlockSpec((None, s, gw), lambda bi, h, i: (bi, 0, kb + h), pipeline_mode=pl.Buffered(1)),
                  pl.BlockSpec((None, s, gw), lambda bi, h, i: (bi, 0, vb + h), pipeline_mode=pl.Buffered(1)),
                  pl.BlockSpec((None, tq, gw), lambda bi, h, i: (bi, i, gb + h))],
        out_specs=pl.BlockSpec((None, tq, gw), lambda bi, h, i: (bi, i, h)),
        out_shape=jax.ShapeDtypeStruct((b, s, n_heads * HEAD_DIM), BF16),
        scratch_shapes=[pltpu.VMEM((2, heads, tq, tq), F32),
                        pltpu.VMEM((heads, tq, tq), BF16),
                        pltpu.VMEM((heads, tq, LANES), F32),
                        pltpu.VMEM((tq, gw), F32)],
        compiler_params=_compiler_params(("parallel", "parallel", "arbitrary"), blocks),
        name="sb_attention",
    )(proj, proj, proj, proj)


def _local_mixers_kernel(sink_ref,
                         q_ref, kp_ref, kc_ref, vp_ref, vc_ref, gb0_ref, gb1_ref,
                         x0_ref, x1_ref, xp0_ref, xp1_ref, c0_ref, c1_ref, cp0_ref, cp1_ref,
                         b0_ref, b1_ref, gc0_ref, gc1_ref, cw_ref,
                         u0_ref, u1_ref, v0_ref, v1_ref, gd0_ref, gd1_ref, vg_ref, ws_ref, bias_ref,
                         yb_ref, yc_ref, yd_ref, *, n_kv, group, n_sub):
    n = pl.program_id(1)

    def both(r0, r1):
        return jnp.concatenate([r0[...], r1[...]], axis=1).astype(F32)

    k_all = jnp.concatenate([kp_ref[...], kc_ref[...]], axis=0)
    v_all = jnp.concatenate([vp_ref[...], vc_ref[...]], axis=0)
    r = lax.broadcasted_iota(jnp.int32, (WINDOW, 2 * WINDOW), 0)
    c = lax.broadcasted_iota(jnp.int32, (WINDOW, 2 * WINDOW), 1)
    in_window = (c > r) & (c <= r + WINDOW)
    gate_b = _silu(both(gb0_ref, gb1_ref))
    for sub in range(n_sub):
        rows = slice(sub * WINDOW, (sub + 1) * WINDOW)
        valid = in_window & ((c >= WINDOW) | (n > 0)) if sub == 0 else in_window
        outs = []
        for kv in range(n_kv):
            k = k_all[sub * WINDOW:(sub + 2) * WINDOW, kv * HEAD_DIM:(kv + 1) * HEAD_DIM]
            v = v_all[sub * WINDOW:(sub + 2) * WINDOW, kv * HEAD_DIM:(kv + 1) * HEAD_DIM]
            for g in range(group):
                head = kv * group + g
                qg = q_ref[rows, head * HEAD_DIM:(head + 1) * HEAD_DIM]
                s = lax.dot_general(qg, k, (((1,), (1,)), ((), ())), preferred_element_type=F32)
                s = jnp.where(valid, s, MASKED_SCORE)
                sink = sink_ref[head]
                m = jnp.maximum(jnp.max(s, axis=-1, keepdims=True), sink)
                p = jnp.exp(s - m)
                denom = jnp.sum(p, axis=-1, keepdims=True) + jnp.exp(sink - m)
                outs.append(jnp.dot(p.astype(BF16), v, preferred_element_type=F32) / denom)
        o = jnp.concatenate(outs, axis=1)
        yb_ref[rows, :] = (o * gate_b[rows]).astype(yb_ref.dtype)

    h = both(c0_ref, c1_ref) * both(x0_ref, x1_ref)
    hp = both(cp0_ref, cp1_ref) * both(xp0_ref, xp1_ref)
    hp = jnp.where(n > 0, hp, 0.0)
    ext = jnp.concatenate([hp, h], axis=0)
    h1 = pltpu.roll(ext, 1, 0)[SUBLANES:]
    h2 = pltpu.roll(ext, 2, 0)[SUBLANES:]
    w = cw_ref[...]
    y = w[0:1] * h2 + w[1:2] * h1 + w[2:3] * h
    y = both(b0_ref, b1_ref) * y
    yc_ref[...] = (y * _silu(both(gc0_ref, gc1_ref))).astype(yc_ref.dtype)

    v = both(v0_ref, v1_ref)
    ms = jnp.mean(v * v, axis=-1, keepdims=True)
    vn = (v * lax.rsqrt(ms + EPS) * vg_ref[...]).astype(BF16)
    rr = lax.broadcasted_iota(jnp.int32, (GMLP_CHUNK, GMLP_CHUNK), 0)
    cc = lax.broadcasted_iota(jnp.int32, (GMLP_CHUNK, GMLP_CHUNK), 1)
    causal = cc <= rr
    gd = GMLP_GROUP_DIM
    n_groups = ws_ref.shape[0]
    n_chunks = v.shape[0] // GMLP_CHUNK
    mixed = []
    for g in range(n_groups):
        wsg = jnp.where(causal, ws_ref[g], 0.0).astype(BF16)
        rhs = jnp.concatenate(
            [vn[ch * GMLP_CHUNK:(ch + 1) * GMLP_CHUNK, g * gd:(g + 1) * gd] for ch in range(n_chunks)], axis=1)
        mixed.append(jnp.dot(wsg, rhs, preferred_element_type=F32))
    bias = bias_ref[...]
    chunks = []
    for ch in range(n_chunks):
        chunks.append(jnp.concatenate([mixed[g][:, ch * gd:(ch + 1) * gd] for g in range(n_groups)], axis=1) + bias)
    mix = jnp.concatenate(chunks, axis=0)
    y = both(u0_ref, u1_ref) * mix
    yd_ref[...] = (y * _silu(both(gd0_ref, gd1_ref))).astype(yd_ref.dtype)


def _local_mixers(proj, sinks, conv_w, v_gain, w_s, bias_tw, offs, n_heads, n_kv, width):
    qb, kb, vb, gb, xc, bc, cc, gc, ud, vd, gd = offs
    b, s, _ = proj.shape
    rows = min(LOCAL_ROWS_PER_STEP, s)
    cw = width // 2
    w_kv = n_kv * HEAD_DIM
    assert s % rows == 0 and rows % WINDOW == 0 and rows % GMLP_CHUNK == 0
    assert n_heads * HEAD_DIM == width and w_s.shape[0] * GMLP_GROUP_DIM == width

    def seg(off, w):
        col = _col_block(off, w)
        return pl.BlockSpec((None, rows, w), lambda bi, n: (bi, n, col))

    def seg_before(off, w, n_rows):
        col = _col_block(off, w)
        per_step = rows // n_rows
        return pl.BlockSpec((None, n_rows, w), lambda bi, n: (bi, jnp.maximum(n * per_step - 1, 0), col))

    def halves(off):
        return [seg(off, cw), seg(off + cw, cw)]

    def halves_before(off):
        return [seg_before(off, cw, SUBLANES), seg_before(off + cw, cw, SUBLANES)]

    def whole(shape):
        return pl.BlockSpec(shape, lambda bi, n: (0,) * len(shape))

    in_specs = ([pl.BlockSpec(memory_space=pltpu.SMEM),
                 seg(qb, width), seg_before(kb, w_kv, WINDOW), seg(kb, w_kv),
                 seg_before(vb, w_kv, WINDOW), seg(vb, w_kv)] + halves(gb)
                + halves(xc) + halves_before(xc) + halves(cc) + halves_before(cc) + halves(bc) + halves(gc)
                + [whole((CONV_WIDTH, width))]
                + halves(ud) + halves(vd) + halves(gd)
                + [whole((1, width)), whole(w_s.shape), whole((GMLP_CHUNK, width))])
    operands = [sinks] + [proj] * 7 + [proj] * 12 + [conv_w] + [proj] * 6 + [v_gain.reshape(1, width), w_s, bias_tw]
    assert len(operands) == len(in_specs)
    out_spec = pl.BlockSpec((None, rows, width), lambda bi, n: (bi, n, 0))
    out_shape = jax.ShapeDtypeStruct((b, s, width), BF16)
    blocks = (17 + 3) * rows * cw * 2 + 12 * rows * width * 4
    return pl.pallas_call(
        functools.partial(_local_mixers_kernel, n_kv=n_kv, group=n_heads // n_kv, n_sub=rows // WINDOW),
        grid=(b, s // rows),
        in_specs=in_specs,
        out_specs=[out_spec, out_spec, out_spec],
        out_shape=[out_shape, out_shape, out_shape],
        compiler_params=_compiler_params(("parallel", "arbitrary"), blocks),
        name="local_mixers",
    )(*operands)


def _residual_update(y_refs, w_ref, x_ref, width):
    acc = x_ref[...]
    for idx, y_ref in enumerate(y_refs):
        acc = acc + jnp.dot(y_ref[...], w_ref[idx * width:(idx + 1) * width, :], preferred_element_type=F32)
    return acc


def _outproj_kernel(ya_ref, yb_ref, yc_ref, yd_ref, w_ref, x_ref, o_ref, *, width):
    o_ref[...] = _residual_update((ya_ref, yb_ref, yc_ref, yd_ref), w_ref, x_ref, width)


def _outproj_prenorm_kernel(ya_ref, yb_ref, yc_ref, yd_ref, w_ref, x_ref, g_ref, o_ref, hg_ref, ssq_ref, *, width):
    acc = _residual_update((ya_ref, yb_ref, yc_ref, yd_ref), w_ref, x_ref, width)
    o_ref[...] = acc
    hg_ref[...] = (acc * g_ref[...]).astype(hg_ref.dtype)
    part = jnp.sum(acc * acc, axis=-1, keepdims=True)
    j = pl.program_id(1)

    @pl.when(j == 0)
    def _():
        ssq_ref[...] = part

    @pl.when(j > 0)
    def _():
        ssq_ref[...] += part


def _outproj(ys, w_all, layer, x, next_gain=None):
    m, d = x.shape
    width = ys[0].shape[1]
    tm = min(1024, m)
    tn = min(1024, d)
    assert m % tm == 0 and d % tn == 0
    blocks = 4 * tm * width * 2 + 4 * width * tn * 2 + 3 * tm * tn * 4
    y_spec = pl.BlockSpec((tm, width), lambda i, j: (i, 0))
    tile = pl.BlockSpec((tm, tn), lambda i, j: (i, j))
    in_specs = [y_spec, y_spec, y_spec, y_spec,
                pl.BlockSpec((None, 4 * width, tn), lambda i, j: (layer, 0, j)), tile]
    if next_gain is None:
        return pl.pallas_call(
            functools.partial(_outproj_kernel, width=width),
            grid=(m // tm, d // tn),
            in_specs=in_specs,
            out_specs=tile,
            out_shape=jax.ShapeDtypeStruct((m, d), F32),
            compiler_params=_compiler_params(("parallel", "arbitrary"), blocks),
            name="outproj",
        )(*ys, w_all, x)
    return pl.pallas_call(
        functools.partial(_outproj_prenorm_kernel, width=width),
        grid=(m // tm, d // tn),
        in_specs=in_specs + [pl.BlockSpec((1, tn), lambda i, j: (0, j))],
        out_specs=[tile, tile, pl.BlockSpec((tm, 1), lambda i, j: (i, 0))],
        out_shape=[jax.ShapeDtypeStruct((m, d), F32), jax.ShapeDtypeStruct((m, d), BF16),
                   jax.ShapeDtypeStruct((m, 1), F32)],
        compiler_params=_compiler_params(("parallel", "arbitrary"), blocks + tm * tn * 2),
        name="outproj_prenorm",
    )(*ys, w_all, x, next_gain.reshape(1, d))


def kernel(x, norm_gain, w_in, conv_w, gmlp_v_gain, gmlp_w_s, gmlp_b_s, swa_sinks, w_out, final_gain):
    batch, seq, d_model = x.shape
    depth = w_in.shape[0]
    m = batch * seq
    w_a = w_b = w_c = w_d = d_model // 4
    n_sb_heads = w_a // HEAD_DIM
    n_swa_heads = w_b // HEAD_DIM
    n_swa_kv = max(1, n_swa_heads // 4)
    w_kv = n_swa_kv * HEAD_DIM
    n_groups = gmlp_w_s.shape[1]
    assert n_swa_heads == swa_sinks.shape[1] and w_d == n_groups * GMLP_GROUP_DIM

    widths = (w_a, w_a, w_a, w_a, w_b, w_kv, w_kv, w_b, w_c, w_c, w_c, w_c, w_d, w_d, w_d)
    offs = [0]
    for wd in widths:
        offs.append(offs[-1] + wd)
    (qa, ka, va, ga, qb, kb, vb, gb, xc, bc, cc, gc, ud, vd, gd) = offs[:-1]
    n_proj = offs[-1]
    assert n_proj == w_in.shape[2]

    scale = HEAD_DIM ** -0.5
    col = jnp.arange(n_proj)
    col_scale = jnp.where((col >= qa) & (col < qa + w_a), scale * LOG2_E,
                          jnp.where((col >= qb) & (col < qb + w_b), scale, 1.0))
    col_scale = col_scale.astype(F32).reshape(1, n_proj)

    w_out_bf16 = w_out.astype(BF16)
    h = x.reshape(m, d_model)
    hn, row_ssq = _rmsnorm(h, norm_gain[0], BF16), None
    for l in range(depth):
        proj = _inproj(hn, w_in, l, col_scale, row_ssq).reshape(batch, seq, n_proj)
        y_a = _sb_attention(proj, qa, ka, va, ga, n_sb_heads)
        bias_tw = jnp.repeat(gmlp_b_s[l].T, GMLP_GROUP_DIM, axis=1)
        y_b, y_c, y_d = _local_mixers(proj, swa_sinks[l], conv_w[l], gmlp_v_gain[l], gmlp_w_s[l], bias_tw,
                                      (qb, kb, vb, gb, xc, bc, cc, gc, ud, vd, gd), n_swa_heads, n_swa_kv, w_b)
        ys = [y.reshape(m, -1) for y in (y_a, y_b, y_c, y_d)]
        if l + 1 < depth:
            h, hn, row_ssq = _outproj(ys, w_out_bf16, l, h, norm_gain[l + 1])
        else:
            h = _outproj(ys, w_out_bf16, l, h)
    out = _rmsnorm(h, final_gain, F32)
    return out.reshape(batch, seq, d_model)
```

```python
import functools

import jax
import jax.numpy as jnp
from jax import lax
from jax.experimental import pallas as pl
from jax.experimental.pallas import tpu as pltpu

HEAD_DIM = 128
GMLP_GROUP_DIM = 128
GMLP_CHUNK = 128
WINDOW = 128
CONV_WIDTH = 3
EPS = 1e-6
LOG2_E = 1.4426950408889634
EXP2_CLAMP = 64.0
F32_EXP2_UNDERFLOW = 160.0

LANES = 128
SUBLANES = 8
V7X_VMEM_BYTES = 64 * 1024 * 1024
VMEM_SPILL_ALLOWANCE = 6 * 1024 * 1024
MASKED_SCORE = -1e30
SB_HEADS_PER_STEP = 8
LOCAL_ROWS_PER_STEP = 512

F32 = jnp.float32
BF16 = jnp.bfloat16


def _compiler_params(semantics, block_bytes):
    limit = min(2 * block_bytes + VMEM_SPILL_ALLOWANCE, V7X_VMEM_BYTES - 4 * 1024 * 1024)
    return pltpu.CompilerParams(dimension_semantics=semantics, vmem_limit_bytes=int(limit))


def _col_block(offset, width):
    assert offset % width == 0, (offset, width)
    return offset // width


def _silu(g):
    return g * jax.nn.sigmoid(g)


def _rmsnorm_kernel(x_ref, g_ref, o_ref):
    x = x_ref[...]
    ms = jnp.mean(x * x, axis=-1, keepdims=True)
    o_ref[...] = (x * lax.rsqrt(ms + EPS) * g_ref[...]).astype(o_ref.dtype)


def _rmsnorm(x, gain, out_dtype):
    m, d = x.shape
    tm = min(512, m)
    return pl.pallas_call(
        _rmsnorm_kernel,
        grid=(m // tm,),
        in_specs=[pl.BlockSpec((tm, d), lambda i: (i, 0)),
                  pl.BlockSpec((1, d), lambda i: (0, 0))],
        out_specs=pl.BlockSpec((tm, d), lambda i: (i, 0)),
        out_shape=jax.ShapeDtypeStruct((m, d), out_dtype),
        compiler_params=_compiler_params(("parallel",), tm * d * 12),
        name="rmsnorm",
    )(x, gain.reshape(1, d))


def _inproj_kernel(h_ref, w_ref, s_ref, o_ref):
    acc = jnp.dot(h_ref[...], w_ref[...].astype(BF16), preferred_element_type=F32)
    o_ref[...] = (acc * s_ref[...]).astype(o_ref.dtype)


def _inproj_rowscaled_kernel(h_ref, ssq_ref, w_ref, s_ref, o_ref, *, d_model):
    acc = jnp.dot(h_ref[...], w_ref[...].astype(BF16), preferred_element_type=F32)
    inv_rms = lax.rsqrt(ssq_ref[...] * (1.0 / d_model) + EPS)
    o_ref[...] = (acc * inv_rms * s_ref[...]).astype(o_ref.dtype)


def _inproj(hn, w_all, layer, col_scale, row_ssq=None):
    m, d = hn.shape
    n = w_all.shape[2]
    tm = min(1024, m)
    tn = 768 if n % 768 == 0 else 512
    assert m % tm == 0 and n % tn == 0
    blocks = tm * d * 2 + d * tn * (4 + 2) + tm * tn * (2 + 4)
    h_spec = pl.BlockSpec((tm, d), lambda i, j: (i, 0))
    w_spec = pl.BlockSpec((None, d, tn), lambda i, j: (layer, 0, j))
    s_spec = pl.BlockSpec((1, tn), lambda i, j: (0, j))
    if row_ssq is None:
        body, in_specs, operands = _inproj_kernel, [h_spec, w_spec, s_spec], (hn, w_all, col_scale)
    else:
        body = functools.partial(_inproj_rowscaled_kernel, d_model=d)
        in_specs = [h_spec, pl.BlockSpec((tm, 1), lambda i, j: (i, 0)), w_spec, s_spec]
        operands = (hn, row_ssq, w_all, col_scale)
    return pl.pallas_call(
        body,
        grid=(m // tm, n // tn),
        in_specs=in_specs,
        out_specs=pl.BlockSpec((tm, tn), lambda i, j: (i, j)),
        out_shape=jax.ShapeDtypeStruct((m, n), BF16),
        compiler_params=_compiler_params(("parallel", "arbitrary"), blocks),
        name="inproj",
    )(*operands)


def _sb_attn_kernel(q_ref, k_ref, v_ref, g_ref, o_ref, z_scr, w_scr, carry_scr, acc_scr, *, tq, heads):
    i = pl.program_id(2)
    dh = HEAD_DIM
    hds = range(heads)
    row = lax.broadcasted_iota(jnp.int32, (tq, tq), 0)
    col = lax.broadcasted_iota(jnp.int32, (tq, tq), 1)
    strictly_causal = col < row
    later_keys = ((row > col) | (col == tq - 1)).astype(BF16)
    last_lane = lax.broadcasted_iota(jnp.int32, (tq, LANES), 1) == LANES - 1

    def key_rows(kb):
        return pl.ds(pl.multiple_of(kb * tq, tq), tq)

    def head_cols(hd):
        return slice(hd * dh, (hd + 1) * dh)

    def scores(kb, hd):
        return lax.dot_general(q_ref[:, head_cols(hd)], k_ref[key_rows(kb), head_cols(hd)],
                               (((1,), (1,)), ((), ())), preferred_element_type=F32)

    def softplus_terms(z, carry, diagonal):
        p = jnp.maximum(jnp.log2(1.0 + jnp.exp2(jnp.minimum(z, EXP2_CLAMP))), z)
        if diagonal:
            p = jnp.where(strictly_causal, p, 0.0)
        p = jnp.concatenate([p[:, :tq - LANES], p[:, tq - LANES:] + carry], axis=1)
        return p.astype(BF16), z - p

    def finish(u, cs, hd, diagonal):
        cs_tail = cs[:, tq - LANES:]
        carry_scr[hd] = jnp.where(last_lane, cs_tail, 0.0)
        cs = jnp.concatenate([cs[:, :tq - LANES], jnp.where(last_lane, 0.0, cs_tail)], axis=1)
        w = jnp.exp2(u - cs)
        if diagonal:
            w = jnp.where(strictly_causal, w, 0.0)
        w_scr[hd] = w.astype(BF16)

    def apply_weights(kb):
        for hd in hds:
            acc_scr[:, head_cols(hd)] += jnp.dot(w_scr[hd], v_ref[key_rows(kb), head_cols(hd)],
                                                 preferred_element_type=F32)

    acc_scr[...] = jnp.zeros_like(acc_scr)
    zs = [scores(i, hd) for hd in hds]
    zero_carry = jnp.zeros((tq, LANES), F32)
    terms = [softplus_terms(zs[hd], zero_carry, True) for hd in hds]
    nxt = jnp.maximum(i - 1, 0)
    for hd in hds:
        p_bf16, u = terms[hd]
        cs = jnp.dot(p_bf16, later_keys, preferred_element_type=F32)
        finish(u, cs, hd, True)
        z_scr[0, hd] = scores(nxt, hd)

    def step(kb, z_in, z_out):
        apply_weights(kb + 1)
        nxt = jnp.maximum(kb - 1, 0)
        for hd in hds:
            z_scr[z_out, hd] = scores(nxt, hd)
        terms = []
        for hd in hds:
            p_bf16, u = softplus_terms(z_scr[z_in, hd], carry_scr[hd], False)
            z_scr[z_in, hd] = u
            terms.append(p_bf16)
        for hd in hds:
            cs = jnp.dot(terms[hd], later_keys, preferred_element_type=F32)
            finish(z_scr[z_in, hd], cs, hd, False)

    def stick_used_up():
        swept = carry_scr[0]
        for hd in range(1, heads):
            swept = jnp.minimum(swept, carry_scr[hd])
        return jnp.max(jnp.min(swept, axis=0, keepdims=True)) >= F32_EXP2_UNDERFLOW

    n_pairs = i // 2

    def two_steps(state):
        m, _, _ = state
        kb = i - 1 - 2 * m
        step(kb, 0, 1)
        used_up = stick_used_up()

        @pl.when(jnp.logical_not(used_up))
        def _():
            step(kb - 1, 1, 0)

        return m + 1, jnp.where(used_up, kb, kb - 1), used_up

    pairs_done, last_block, used_up = lax.while_loop(
        lambda st: jnp.logical_and(st[0] < n_pairs, jnp.logical_not(st[2])),
        two_steps, (jnp.int32(0), i, jnp.bool_(False)))
    odd_step = jnp.logical_and(jnp.logical_and(i % 2 == 1, pairs_done == n_pairs), jnp.logical_not(used_up))

    @pl.when(odd_step)
    def _():
        step(0, 0, 1)

    apply_weights(jnp.where(odd_step, 0, last_block))
    o_ref[...] = (acc_scr[...] * _silu(g_ref[...].astype(F32))).astype(o_ref.dtype)


def _sb_attention(proj, q_off, k_off, v_off, g_off, n_heads):
    b, s, _ = proj.shape
    tq = min(256, s)
    heads = min(SB_HEADS_PER_STEP, n_heads)
    assert s % tq == 0 and tq % LANES == 0 and n_heads % heads == 0
    gw = heads * HEAD_DIM
    qb, kb, vb, gb = (_col_block(o, gw) for o in (q_off, k_off, v_off, g_off))
    blocks = 3 * tq * gw * 2 + s * gw * 2 + heads * 6 * tq * tq * 4
    return pl.pallas_call(
        functools.partial(_sb_attn_kernel, tq=tq, heads=heads),
        grid=(b, n_heads // heads, s // tq),
        in_specs=[pl.BlockSpec((None, tq, gw), lambda bi, h, i: (bi, i, qb + h)),
                  pl.BlockSpec((None, s, gw), lambda bi, h, i: (bi, 0, kb + h), pipeline_mode=pl.Buffered(1)),
                  pl.BlockSpec((None, s, gw), lambda bi, h, i: (bi, 0, vb + h), pipeline_mode=pl.Buffered(1)),
                  pl.BlockSpec((None, tq, gw), lambda bi, h, i: (bi, i, gb + h))],
        out_specs=pl.BlockSpec((None, tq, gw), lambda bi, h, i: (bi, i, h)),
        out_shape=jax.ShapeDtypeStruct((b, s, n_heads * HEAD_DIM), BF16),
        scratch_shapes=[pltpu.VMEM((2, heads, tq, tq), F32),
                        pltpu.VMEM((heads, tq, tq), BF16),
                        pltpu.VMEM((heads, tq, LANES), F32),
                        pltpu.VMEM((tq, gw), F32)],
        compiler_params=_compiler_params(("parallel", "parallel", "arbitrary"), blocks),
        name="sb_attention",
    )(proj, proj, proj, proj)


def _local_mixers_kernel(sink_ref,
                         q_ref, kp_ref, kc_ref, vp_ref, vc_ref, gb0_ref, gb1_ref,
                         x0_ref, x1_ref, xp0_ref, xp1_ref, c0_ref, c1_ref, cp0_ref, cp1_ref,
                         b0_ref, b1_ref, gc0_ref, gc1_ref, cw_ref,
                         u0_ref, u1_ref, v0_ref, v1_ref, gd0_ref, gd1_ref, vg_ref, ws_ref, bias_ref,
                         yb_ref, yc_ref, yd_ref, *, n_kv, group, n_sub):
    n = pl.program_id(1)

    def both(r0, r1):
        return jnp.concatenate([r0[...], r1[...]], axis=1).astype(F32)

    k_all = jnp.concatenate([kp_ref[...], kc_ref[...]], axis=0)
    v_all = jnp.concatenate([vp_ref[...], vc_ref[...]], axis=0)
    r = lax.broadcasted_iota(jnp.int32, (WINDOW, 2 * WINDOW), 0)
    c = lax.broadcasted_iota(jnp.int32, (WINDOW, 2 * WINDOW), 1)
    in_window = (c > r) & (c <= r + WINDOW)
    gate_b = _silu(both(gb0_ref, gb1_ref))
    for sub in range(n_sub):
        rows = slice(sub * WINDOW, (sub + 1) * WINDOW)
        valid = in_window & ((c >= WINDOW) | (n > 0)) if sub == 0 else in_window
        outs = []
        for kv in range(n_kv):
            k = k_all[sub * WINDOW:(sub + 2) * WINDOW, kv * HEAD_DIM:(kv + 1) * HEAD_DIM]
            v = v_all[sub * WINDOW:(sub + 2) * WINDOW, kv * HEAD_DIM:(kv + 1) * HEAD_DIM]
            for g in range(group):
                head = kv * group + g
                qg = q_ref[rows, head * HEAD_DIM:(head + 1) * HEAD_DIM]
                s = lax.dot_general(qg, k, (((1,), (1,)), ((), ())), preferred_element_type=F32)
                s = jnp.where(valid, s, MASKED_SCORE)
                sink = sink_ref[head]
                m = jnp.maximum(jnp.max(s, axis=-1, keepdims=True), sink)
                p = jnp.exp(s - m)
                denom = jnp.sum(p, axis=-1, keepdims=True) + jnp.exp(sink - m)
                outs.append(jnp.dot(p.astype(BF16), v, preferred_element_type=F32) / denom)
        o = jnp.concatenate(outs, axis=1)
        yb_ref[rows, :] = (o * gate_b[rows]).astype(yb_ref.dtype)

    h = both(c0_ref, c1_ref) * both(x0_ref, x1_ref)
    hp = both(cp0_ref, cp1_ref) * both(xp0_ref, xp1_ref)
    hp = jnp.where(n > 0, hp, 0.0)
    ext = jnp.concatenate([hp, h], axis=0)
    h1 = pltpu.roll(ext, 1, 0)[SUBLANES:]
    h2 = pltpu.roll(ext, 2, 0)[SUBLANES:]
    w = cw_ref[...]
    y = w[0:1] * h2 + w[1:2] * h1 + w[2:3] * h
    y = both(b0_ref, b1_ref) * y
    yc_ref[...] = (y * _silu(both(gc0_ref, gc1_ref))).astype(yc_ref.dtype)

    v = both(v0_ref, v1_ref)
    ms = jnp.mean(v * v, axis=-1, keepdims=True)
    vn = (v * lax.rsqrt(ms + EPS) * vg_ref[...]).astype(BF16)
    rr = lax.broadcasted_iota(jnp.int32, (GMLP_CHUNK, GMLP_CHUNK), 0)
    cc = lax.broadcasted_iota(jnp.int32, (GMLP_CHUNK, GMLP_CHUNK), 1)
    causal = cc <= rr
    gd = GMLP_GROUP_DIM
    n_groups = ws_ref.shape[0]
    n_chunks = v.shape[0] // GMLP_CHUNK
    mixed = []
    for g in range(n_groups):
        wsg = jnp.where(causal, ws_ref[g], 0.0).astype(BF16)
        rhs = jnp.concatenate(
            [vn[ch * GMLP_CHUNK:(ch + 1) * GMLP_CHUNK, g * gd:(g + 1) * gd] for ch in range(n_chunks)], axis=1)
        mixed.append(jnp.dot(wsg, rhs, preferred_element_type=F32))
    bias = bias_ref[...]
    chunks = []
    for ch in range(n_chunks):
        chunks.append(jnp.concatenate([mixed[g][:, ch * gd:(ch + 1) * gd] for g in range(n_groups)], axis=1) + bias)
    mix = jnp.concatenate(chunks, axis=0)
    y = both(u0_ref, u1_ref) * mix
    yd_ref[...] = (y * _silu(both(gd0_ref, gd1_ref))).astype(yd_ref.dtype)


def _local_mixers(proj, sinks, conv_w, v_gain, w_s, bias_tw, offs, n_heads, n_kv, width):
    qb, kb, vb, gb, xc, bc, cc, gc, ud, vd, gd = offs
    b, s, _ = proj.shape
    rows = min(LOCAL_ROWS_PER_STEP, s)
    cw = width // 2
    w_kv = n_kv * HEAD_DIM
    assert s % rows == 0 and rows % WINDOW == 0 and rows % GMLP_CHUNK == 0
    assert n_heads * HEAD_DIM == width and w_s.shape[0] * GMLP_GROUP_DIM == width

    def seg(off, w):
        col = _col_block(off, w)
        return pl.BlockSpec((None, rows, w), lambda bi, n: (bi, n, col))

    def seg_before(off, w, n_rows):
        col = _col_block(off, w)
        per_step = rows // n_rows
        return pl.BlockSpec((None, n_rows, w), lambda bi, n: (bi, jnp.maximum(n * per_step - 1, 0), col))

    def halves(off):
        return [seg(off, cw), seg(off + cw, cw)]

    def halves_before(off):
        return [seg_before(off, cw, SUBLANES), seg_before(off + cw, cw, SUBLANES)]

    def whole(shape):
        return pl.BlockSpec(shape, lambda bi, n: (0,) * len(shape))

    in_specs = ([pl.BlockSpec(memory_space=pltpu.SMEM),
                 seg(qb, width), seg_before(kb, w_kv, WINDOW), seg(kb, w_kv),
                 seg_before(vb, w_kv, WINDOW), seg(vb, w_kv)] + halves(gb)
                + halves(xc) + halves_before(xc) + halves(cc) + halves_before(cc) + halves(bc) + halves(gc)
                + [whole((CONV_WIDTH, width))]
                + halves(ud) + halves(vd) + halves(gd)
                + [whole((1, width)), whole(w_s.shape), whole((GMLP_CHUNK, width))])
    operands = [sinks] + [proj] * 7 + [proj] * 12 + [conv_w] + [proj] * 6 + [v_gain.reshape(1, width), w_s, bias_tw]
    assert len(operands) == len(in_specs)
    out_spec = pl.BlockSpec((None, rows, width), lambda bi, n: (bi, n, 0))
    out_shape = jax.ShapeDtypeStruct((b, s, width), BF16)
    blocks = (17 + 3) * rows * cw * 2 + 12 * rows * width * 4
    return pl.pallas_call(
        functools.partial(_local_mixers_kernel, n_kv=n_kv, group=n_heads // n_kv, n_sub=rows // WINDOW),
        grid=(b, s // rows),
        in_specs=in_specs,
        out_specs=[out_spec, out_spec, out_spec],
        out_shape=[out_shape, out_shape, out_shape],
        compiler_params=_compiler_params(("parallel", "arbitrary"), blocks),
        name="local_mixers",
    )(*operands)


def _residual_update(y_refs, w_ref, x_ref, width):
    acc = x_ref[...]
    for idx, y_ref in enumerate(y_refs):
        acc = acc + jnp.dot(y_ref[...], w_ref[idx * width:(idx + 1) * width, :], preferred_element_type=F32)
    return acc


def _outproj_kernel(ya_ref, yb_ref, yc_ref, yd_ref, w_ref, x_ref, o_ref, *, width):
    o_ref[...] = _residual_update((ya_ref, yb_ref, yc_ref, yd_ref), w_ref, x_ref, width)


def _outproj_prenorm_kernel(ya_ref, yb_ref, yc_ref, yd_ref, w_ref, x_ref, g_ref, o_ref, hg_ref, ssq_ref, *, width):
    acc = _residual_update((ya_ref, yb_ref, yc_ref, yd_ref), w_ref, x_ref, width)
    o_ref[...] = acc
    hg_ref[...] = (acc * g_ref[...]).astype(hg_ref.dtype)
    part = jnp.sum(acc * acc, axis=-1, keepdims=True)
    j = pl.program_id(1)

    @pl.when(j == 0)
    def _():
        ssq_ref[...] = part

    @pl.when(j > 0)
    def _():
        ssq_ref[...] += part


def _outproj(ys, w_all, layer, x, next_gain=None):
    m, d = x.shape
    width = ys[0].shape[1]
    tm = min(1024, m)
    tn = min(1024, d)
    assert m % tm == 0 and d % tn == 0
    blocks = 4 * tm * width * 2 + 4 * width * tn * 2 + 3 * tm * tn * 4
    y_spec = pl.BlockSpec((tm, width), lambda i, j: (i, 0))
    tile = pl.BlockSpec((tm, tn), lambda i, j: (i, j))
    in_specs = [y_spec, y_spec, y_spec, y_spec,
                pl.BlockSpec((None, 4 * width, tn), lambda i, j: (layer, 0, j)), tile]
    if next_gain is None:
        return pl.pallas_call(
            functools.partial(_outproj_kernel, width=width),
            grid=(m // tm, d // tn),
            in_specs=in_specs,
            out_specs=tile,
            out_shape=jax.ShapeDtypeStruct((m, d), F32),
            compiler_params=_compiler_params(("parallel", "arbitrary"), blocks),
            name="outproj",
        )(*ys, w_all, x)
    return pl.pallas_call(
        functools.partial(_outproj_prenorm_kernel, width=width),
        grid=(m // tm, d // tn),
        in_specs=in_specs + [pl.BlockSpec((1, tn), lambda i, j: (0, j))],
        out_specs=[tile, tile, pl.BlockSpec((tm, 1), lambda i, j: (i, 0))],
        out_shape=[jax.ShapeDtypeStruct((m, d), F32), jax.ShapeDtypeStruct((m, d), BF16),
                   jax.ShapeDtypeStruct((m, 1), F32)],
        compiler_params=_compiler_params(("parallel", "arbitrary"), blocks + tm * tn * 2),
        name="outproj_prenorm",
    )(*ys, w_all, x, next_gain.reshape(1, d))


def kernel(x, norm_gain, w_in, conv_w, gmlp_v_gain, gmlp_w_s, gmlp_b_s, swa_sinks, w_out, final_gain):
    batch, seq, d_model = x.shape
    depth = w_in.shape[0]
    m = batch * seq
    w_a = w_b = w_c = w_d = d_model // 4
    n_sb_heads = w_a // HEAD_DIM
    n_swa_heads = w_b // HEAD_DIM
    n_swa_kv = max(1, n_swa_heads // 4)
    w_kv = n_swa_kv * HEAD_DIM
    n_groups = gmlp_w_s.shape[1]
    assert n_swa_heads == swa_sinks.shape[1] and w_d == n_groups * GMLP_GROUP_DIM

    widths = (w_a, w_a, w_a, w_a, w_b, w_kv, w_kv, w_b, w_c, w_c, w_c, w_c, w_d, w_d, w_d)
    offs = [0]
    for wd in widths:
        offs.append(offs[-1] + wd)
    (qa, ka, va, ga, qb, kb, vb, gb, xc, bc, cc, gc, ud, vd, gd) = offs[:-1]
    n_proj = offs[-1]
    assert n_proj == w_in.shape[2]

    scale = HEAD_DIM ** -0.5
    col = jnp.arange(n_proj)
    col_scale = jnp.where((col >= qa) & (col < qa + w_a), scale * LOG2_E,
                          jnp.where((col >= qb) & (col < qb + w_b), scale, 1.0))
    col_scale = col_scale.astype(F32).reshape(1, n_proj)

    w_out_bf16 = w_out.astype(BF16)
    h = x.reshape(m, d_model)
    hn, row_ssq = _rmsnorm(h, norm_gain[0], BF16), None
    for l in range(depth):
        proj = _inproj(hn, w_in, l, col_scale, row_ssq).reshape(batch, seq, n_proj)
        y_a = _sb_attention(proj, qa, ka, va, ga, n_sb_heads)
        bias_tw = jnp.repeat(gmlp_b_s[l].T, GMLP_GROUP_DIM, axis=1)
        y_b, y_c, y_d = _local_mixers(proj, swa_sinks[l], conv_w[l], gmlp_v_gain[l], gmlp_w_s[l], bias_tw,
                                      (qb, kb, vb, gb, xc, bc, cc, gc, ud, vd, gd), n_swa_heads, n_swa_kv, w_b)
        ys = [y.reshape(m, -1) for y in (y_a, y_b, y_c, y_d)]
        if l + 1 < depth:
            h, hn, row_ssq = _outproj(ys, w_out_bf16, l, h, norm_gain[l + 1])
        else:
            h = _outproj(ys, w_out_bf16, l, h)
    out = _rmsnorm(h, final_gain, F32)
    return out.reshape(batch, seq, d_model)
```

```python
import functools

import jax
import jax.numpy as jnp
from jax import lax
from jax.experimental import pallas as pl
from jax.experimental.pallas import tpu as pltpu

HEAD_DIM = 128
GMLP_GROUP_DIM = 128
GMLP_CHUNK = 128
WINDOW = 128
CONV_WIDTH = 3
EPS = 1e-6
LOG2_E = 1.4426950408889634
EXP2_CLAMP = 64.0
F32_EXP2_UNDERFLOW = 160.0

LANES = 128
SUBLANES = 8
V7X_VMEM_BYTES = 64 * 1024 * 1024
V7X_VMEM_RESERVED_BYTES = 4 * 1024 * 1024
VMEM_SPILL_ALLOWANCE = 6 * 1024 * 1024
MASKED_SCORE = -1e30
SB_HEADS_PER_STEP = 8
LOCAL_ROWS_PER_STEP = 512

F32 = jnp.float32
BF16 = jnp.bfloat16


def _compiler_params(semantics, block_bytes):
    limit = min(2 * block_bytes + VMEM_SPILL_ALLOWANCE, V7X_VMEM_BYTES - V7X_VMEM_RESERVED_BYTES)
    return pltpu.CompilerParams(dimension_semantics=semantics, vmem_limit_bytes=int(limit))


def _col_block(offset, width):
    assert offset % width == 0, (offset, width)
    return offset // width


def _silu(g):
    return g * jax.nn.sigmoid(g)


def _rmsnorm_kernel(x_ref, g_ref, o_ref):
    x = x_ref[...]
    ms = jnp.mean(x * x, axis=-1, keepdims=True)
    o_ref[...] = (x * lax.rsqrt(ms + EPS) * g_ref[...]).astype(o_ref.dtype)


def _rmsnorm(x, gain, out_dtype):
    m, d = x.shape
    tm = min(512, m)
    return pl.pallas_call(
        _rmsnorm_kernel,
        grid=(m // tm,),
        in_specs=[pl.BlockSpec((tm, d), lambda i: (i, 0)),
                  pl.BlockSpec((1, d), lambda i: (0, 0))],
        out_specs=pl.BlockSpec((tm, d), lambda i: (i, 0)),
        out_shape=jax.ShapeDtypeStruct((m, d), out_dtype),
        compiler_params=_compiler_params(("parallel",), tm * d * (4 + 4 + 4)),
        name="rmsnorm",
    )(x, gain.reshape(1, d))


def _inproj_kernel(h_ref, w_ref, s_ref, o_ref):
    acc = jnp.dot(h_ref[...], w_ref[...].astype(BF16), preferred_element_type=F32)
    o_ref[...] = (acc * s_ref[...]).astype(o_ref.dtype)


def _inproj_rowscaled_kernel(h_ref, ssq_ref, w_ref, s_ref, o_ref, *, d_model):
    acc = jnp.dot(h_ref[...], w_ref[...].astype(BF16), preferred_element_type=F32)
    inv_rms = lax.rsqrt(ssq_ref[...] * (1.0 / d_model) + EPS)
    o_ref[...] = (acc * inv_rms * s_ref[...]).astype(o_ref.dtype)


def _inproj(hn, w_all, layer, col_scale, row_ssq=None):
    m, d = hn.shape
    n = w_all.shape[2]
    tm = min(1024, m)
    tn = 768 if n % 768 == 0 else 512
    assert m % tm == 0 and n % tn == 0
    blocks = tm * d * 2 + d * tn * (4 + 2) + tm * tn * (2 + 4)
    h_spec = pl.BlockSpec((tm, d), lambda i, j: (i, 0))
    w_spec = pl.BlockSpec((None, d, tn), lambda i, j: (layer, 0, j))
    s_spec = pl.BlockSpec((1, tn), lambda i, j: (0, j))
    if row_ssq is None:
        body, in_specs, operands = _inproj_kernel, [h_spec, w_spec, s_spec], (hn, w_all, col_scale)
    else:
        body = functools.partial(_inproj_rowscaled_kernel, d_model=d)
        in_specs = [h_spec, pl.BlockSpec((tm, 1), lambda i, j: (i, 0)), w_spec, s_spec]
        operands = (hn, row_ssq, w_all, col_scale)
    return pl.pallas_call(
        body,
        grid=(m // tm, n // tn),
        in_specs=in_specs,
        out_specs=pl.BlockSpec((tm, tn), lambda i, j: (i, j)),
        out_shape=jax.ShapeDtypeStruct((m, n), BF16),
        compiler_params=_compiler_params(("parallel", "arbitrary"), blocks),
        name="inproj",
    )(*operands)


def _sb_attn_kernel(q_ref, k_ref, v_ref, g_ref, o_ref, z_scr, w_scr, carry_scr, acc_scr, *, tq, heads):
    i = pl.program_id(2)
    dh = HEAD_DIM
    hds = range(heads)
    row = lax.broadcasted_iota(jnp.int32, (tq, tq), 0)
    col = lax.broadcasted_iota(jnp.int32, (tq, tq), 1)
    strictly_causal = col < row
    later_keys = ((row > col) | (col == tq - 1)).astype(BF16)
    last_lane = lax.broadcasted_iota(jnp.int32, (tq, LANES), 1) == LANES - 1

    def key_rows(kb):
        return pl.ds(pl.multiple_of(kb * tq, tq), tq)

    def head_cols(hd):
        return slice(hd * dh, (hd + 1) * dh)

    def scores(kb, hd):
        return lax.dot_general(q_ref[:, head_cols(hd)], k_ref[key_rows(kb), head_cols(hd)],
                               (((1,), (1,)), ((), ())), preferred_element_type=F32)

    def softplus_terms(z, carry, diagonal):
        p = jnp.maximum(jnp.log2(1.0 + jnp.exp2(jnp.minimum(z, EXP2_CLAMP))), z)
        if diagonal:
            p = jnp.where(strictly_causal, p, 0.0)
        p = jnp.concatenate([p[:, :tq - LANES], p[:, tq - LANES:] + carry], axis=1)
        return p.astype(BF16), z - p

    def finish(u, cs, hd, diagonal):
        cs_tail = cs[:, tq - LANES:]
        carry_scr[hd] = jnp.where(last_lane, cs_tail, 0.0)
        cs = jnp.concatenate([cs[:, :tq - LANES], jnp.where(last_lane, 0.0, cs_tail)], axis=1)
        w = jnp.exp2(u - cs)
        if diagonal:
            w = jnp.where(strictly_causal, w, 0.0)
        w_scr[hd] = w.astype(BF16)

    def apply_weights(kb):
        for hd in hds:
            acc_scr[:, head_cols(hd)] += jnp.dot(w_scr[hd], v_ref[key_rows(kb), head_cols(hd)],
                                                 preferred_element_type=F32)

    acc_scr[...] = jnp.zeros_like(acc_scr)
    zs = [scores(i, hd) for hd in hds]
    zero_carry = jnp.zeros((tq, LANES), F32)
    terms = [softplus_terms(zs[hd], zero_carry, True) for hd in hds]
    nxt = jnp.maximum(i - 1, 0)
    for hd in hds:
        p_bf16, u = terms[hd]
        cs = jnp.dot(p_bf16, later_keys, preferred_element_type=F32)
        finish(u, cs, hd, True)
        z_scr[0, hd] = scores(nxt, hd)

    def step(kb, z_in, z_out):
        apply_weights(kb + 1)
        nxt = jnp.maximum(kb - 1, 0)
        for hd in hds:
            z_scr[z_out, hd] = scores(nxt, hd)
        terms = []
        for hd in hds:
            p_bf16, u = softplus_terms(z_scr[z_in, hd], carry_scr[hd], False)
            z_scr[z_in, hd] = u
            terms.append(p_bf16)
        for hd in hds:
            cs = jnp.dot(terms[hd], later_keys, preferred_element_type=F32)
            finish(z_scr[z_in, hd], cs, hd, False)

    def stick_used_up():
        swept = carry_scr[0]
        for hd in range(1, heads):
            swept = jnp.minimum(swept, carry_scr[hd])
        return jnp.max(jnp.min(swept, axis=0, keepdims=True)) >= F32_EXP2_UNDERFLOW

    n_pairs = i // 2

    def two_steps(state):
        m, _, _ = state
        kb = i - 1 - 2 * m
        step(kb, 0, 1)
        used_up = stick_used_up()

        @pl.when(jnp.logical_not(used_up))
        def _():
            step(kb - 1, 1, 0)

        return m + 1, jnp.where(used_up, kb, kb - 1), used_up

    pairs_done, last_block, used_up = lax.while_loop(
        lambda st: jnp.logical_and(st[0] < n_pairs, jnp.logical_not(st[2])),
        two_steps, (jnp.int32(0), i, jnp.bool_(False)))
    odd_step = jnp.logical_and(jnp.logical_and(i % 2 == 1, pairs_done == n_pairs), jnp.logical_not(used_up))

    @pl.when(odd_step)
    def _():
        step(0, 0, 1)

    apply_weights(jnp.where(odd_step, 0, last_block))
    o_ref[...] = (acc_scr[...] * _silu(g_ref[...].astype(F32))).astype(o_ref.dtype)


def _sb_attention(proj, q_off, k_off, v_off, g_off, n_heads):
    b, s, _ = proj.shape
    tq = min(256, s)
    heads = min(SB_HEADS_PER_STEP, n_heads)
    assert s % tq == 0 and tq % LANES == 0 and n_heads % heads == 0
    gw = heads * HEAD_DIM
    qb, kb, vb, gb = (_col_block(o, gw) for o in (q_off, k_off, v_off, g_off))
    blocks = 3 * tq * gw * 2 + s * gw * 2 + heads * 6 * tq * tq * 4
    return pl.pallas_call(
        functools.partial(_sb_attn_kernel, tq=tq, heads=heads),
        grid=(b, n_heads // heads, s // tq),
        in_specs=[pl.BlockSpec((None, tq, gw), lambda bi, h, i: (bi, i, qb + h)),
                  pl.BlockSpec((None, s, gw), lambda bi, h, i: (bi, 0, kb + h), pipeline_mode=pl.Buffered(1)),
                  pl.BlockSpec((None, s, gw), lambda bi, h, i: (bi, 0, vb + h), pipeline_mode=pl.Buffered(1)),
                  pl.BlockSpec((None, tq, gw), lambda bi, h, i: (bi, i, gb + h))],
        out_specs=pl.BlockSpec((None, tq, gw), lambda bi, h, i: (bi, i, h)),
        out_shape=jax.ShapeDtypeStruct((b, s, n_heads * HEAD_DIM), BF16),
        scratch_shapes=[pltpu.VMEM((2, heads, tq, tq), F32),
                        pltpu.VMEM((heads, tq, tq), BF16),
                        pltpu.VMEM((heads, tq, LANES), F32),
                        pltpu.VMEM((tq, gw), F32)],
        compiler_params=_compiler_params(("parallel", "parallel", "arbitrary"), blocks),
        name="sb_attention",
    )(proj, proj, proj, proj)


def _local_mixers_kernel(sink_ref,
                         q_ref, kp_ref, kc_ref, vp_ref, vc_ref, gb0_ref, gb1_ref,
                         x0_ref, x1_ref, xp0_ref, xp1_ref, c0_ref, c1_ref, cp0_ref, cp1_ref,
                         b0_ref, b1_ref, gc0_ref, gc1_ref, cw_ref,
                         u0_ref, u1_ref, v0_ref, v1_ref, gd0_ref, gd1_ref, vg_ref, ws_ref, bias_ref,
                         yb_ref, yc_ref, yd_ref, *, n_kv, group, n_sub):
    n = pl.program_id(1)

    def both(r0, r1):
        return jnp.concatenate([r0[...], r1[...]], axis=1).astype(F32)

    k_all = jnp.concatenate([kp_ref[...], kc_ref[...]], axis=0)
    v_all = jnp.concatenate([vp_ref[...], vc_ref[...]], axis=0)
    r = lax.broadcasted_iota(jnp.int32, (WINDOW, 2 * WINDOW), 0)
    c = lax.broadcasted_iota(jnp.int32, (WINDOW, 2 * WINDOW), 1)
    in_window = (c > r) & (c <= r + WINDOW)
    gate_b = _silu(both(gb0_ref, gb1_ref))
    for sub in range(n_sub):
        rows = slice(sub * WINDOW, (sub + 1) * WINDOW)
        valid = in_window & ((c >= WINDOW) | (n > 0)) if sub == 0 else in_window
        outs = []
        for kv in range(n_kv):
            k = k_all[sub * WINDOW:(sub + 2) * WINDOW, kv * HEAD_DIM:(kv + 1) * HEAD_DIM]
            v = v_all[sub * WINDOW:(sub + 2) * WINDOW, kv * HEAD_DIM:(kv + 1) * HEAD_DIM]
            for g in range(group):
                head = kv * group + g
                qg = q_ref[rows, head * HEAD_DIM:(head + 1) * HEAD_DIM]
                s = lax.dot_general(qg, k, (((1,), (1,)), ((), ())), preferred_element_type=F32)
                s = jnp.where(valid, s, MASKED_SCORE)
                sink = sink_ref[head]
                m = jnp.maximum(jnp.max(s, axis=-1, keepdims=True), sink)
                p = jnp.exp(s - m)
                denom = jnp.sum(p, axis=-1, keepdims=True) + jnp.exp(sink - m)
                outs.append(jnp.dot(p.astype(BF16), v, preferred_element_type=F32) / denom)
        o = jnp.concatenate(outs, axis=1)
        yb_ref[rows, :] = (o * gate_b[rows]).astype(yb_ref.dtype)

    h = both(c0_ref, c1_ref) * both(x0_ref, x1_ref)
    hp = both(cp0_ref, cp1_ref) * both(xp0_ref, xp1_ref)
    hp = jnp.where(n > 0, hp, 0.0)
    ext = jnp.concatenate([hp, h], axis=0)
    h1 = pltpu.roll(ext, 1, 0)[SUBLANES:]
    h2 = pltpu.roll(ext, 2, 0)[SUBLANES:]
    w = cw_ref[...]
    y = w[0:1] * h2 + w[1:2] * h1 + w[2:3] * h
    y = both(b0_ref, b1_ref) * y
    yc_ref[...] = (y * _silu(both(gc0_ref, gc1_ref))).astype(yc_ref.dtype)

    v = both(v0_ref, v1_ref)
    ms = jnp.mean(v * v, axis=-1, keepdims=True)
    vn = (v * lax.rsqrt(ms + EPS) * vg_ref[...]).astype(BF16)
    rr = lax.broadcasted_iota(jnp.int32, (GMLP_CHUNK, GMLP_CHUNK), 0)
    cc = lax.broadcasted_iota(jnp.int32, (GMLP_CHUNK, GMLP_CHUNK), 1)
    causal = cc <= rr
    gd = GMLP_GROUP_DIM
    n_groups = ws_ref.shape[0]
    n_chunks = v.shape[0] // GMLP_CHUNK
    mixed = []
    for g in range(n_groups):
        wsg = jnp.where(causal, ws_ref[g], 0.0).astype(BF16)
        rhs = jnp.concatenate(
            [vn[ch * GMLP_CHUNK:(ch + 1) * GMLP_CHUNK, g * gd:(g + 1) * gd] for ch in range(n_chunks)], axis=1)
        mixed.append(jnp.dot(wsg, rhs, preferred_element_type=F32))
    bias = bias_ref[...]
    chunks = []
    for ch in range(n_chunks):
        chunks.append(jnp.concatenate([mixed[g][:, ch * gd:(ch + 1) * gd] for g in range(n_groups)], axis=1) + bias)
    mix = jnp.concatenate(chunks, axis=0)
    y = both(u0_ref, u1_ref) * mix
    yd_ref[...] = (y * _silu(both(gd0_ref, gd1_ref))).astype(yd_ref.dtype)


def _local_mixers(proj, sinks, conv_w, v_gain, w_s, bias_tw, offs, n_heads, n_kv, width):
    qb, kb, vb, gb, xc, bc, cc, gc, ud, vd, gd = offs
    b, s, _ = proj.shape
    rows = min(LOCAL_ROWS_PER_STEP, s)
    cw = width // 2
    w_kv = n_kv * HEAD_DIM
    assert s % rows == 0 and rows % WINDOW == 0 and rows % GMLP_CHUNK == 0
    assert n_heads * HEAD_DIM == width and w_s.shape[0] * GMLP_GROUP_DIM == width

    def seg(off, w):
        col = _col_block(off, w)
        return pl.BlockSpec((None, rows, w), lambda bi, n: (bi, n, col))

    def seg_before(off, w, n_rows):
        col = _col_block(off, w)
        per_step = rows // n_rows
        return pl.BlockSpec((None, n_rows, w), lambda bi, n: (bi, jnp.maximum(n * per_step - 1, 0), col))

    def halves(off):
        return [seg(off, cw), seg(off + cw, cw)]

    def halves_before(off):
        return [seg_before(off, cw, SUBLANES), seg_before(off + cw, cw, SUBLANES)]

    def whole(shape):
        return pl.BlockSpec(shape, lambda bi, n: (0,) * len(shape))

    in_specs = ([pl.BlockSpec(memory_space=pltpu.SMEM),
                 seg(qb, width), seg_before(kb, w_kv, WINDOW), seg(kb, w_kv),
                 seg_before(vb, w_kv, WINDOW), seg(vb, w_kv)] + halves(gb)
                + halves(xc) + halves_before(xc) + halves(cc) + halves_before(cc) + halves(bc) + halves(gc)
                + [whole((CONV_WIDTH, width))]
                + halves(ud) + halves(vd) + halves(gd)
                + [whole((1, width)), whole(w_s.shape), whole((GMLP_CHUNK, width))])
    operands = [sinks] + [proj] * 7 + [proj] * 12 + [conv_w] + [proj] * 6 + [v_gain.reshape(1, width), w_s, bias_tw]
    assert len(operands) == len(in_specs)
    out_spec = pl.BlockSpec((None, rows, width), lambda bi, n: (bi, n, 0))
    out_shape = jax.ShapeDtypeStruct((b, s, width), BF16)
    blocks = (17 + 3) * rows * cw * 2 + 12 * rows * width * 4
    return pl.pallas_call(
        functools.partial(_local_mixers_kernel, n_kv=n_kv, group=n_heads // n_kv, n_sub=rows // WINDOW),
        grid=(b, s // rows),
        in_specs=in_specs,
        out_specs=[out_spec, out_spec, out_spec],
        out_shape=[out_shape, out_shape, out_shape],
        compiler_params=_compiler_params(("parallel", "arbitrary"), blocks),
        name="local_mixers",
    )(*operands)


def _residual_update(y_refs, w_ref, x_ref, width):
    acc = x_ref[...]
    for idx, y_ref in enumerate(y_refs):
        acc = acc + jnp.dot(y_ref[...], w_ref[idx * width:(idx + 1) * width, :], preferred_element_type=F32)
    return acc


def _outproj_kernel(ya_ref, yb_ref, yc_ref, yd_ref, w_ref, x_ref, o_ref, *, width):
    o_ref[...] = _residual_update((ya_ref, yb_ref, yc_ref, yd_ref), w_ref, x_ref, width)


def _outproj_prenorm_kernel(ya_ref, yb_ref, yc_ref, yd_ref, w_ref, x_ref, g_ref, o_ref, hg_ref, ssq_ref, *, width):
    acc = _residual_update((ya_ref, yb_ref, yc_ref, yd_ref), w_ref, x_ref, width)
    o_ref[...] = acc
    hg_ref[...] = (acc * g_ref[...]).astype(hg_ref.dtype)
    part = jnp.sum(acc * acc, axis=-1, keepdims=True)
    j = pl.program_id(1)

    @pl.when(j == 0)
    def _():
        ssq_ref[...] = part

    @pl.when(j > 0)
    def _():
        ssq_ref[...] += part


def _outproj(ys, w_all, layer, x, next_gain=None):
    m, d = x.shape
    width = ys[0].shape[1]
    tm = min(1024, m)
    tn = min(1024, d)
    assert m % tm == 0 and d % tn == 0
    blocks = 4 * tm * width * 2 + 4 * width * tn * 2 + 3 * tm * tn * 4
    y_spec = pl.BlockSpec((tm, width), lambda i, j: (i, 0))
    tile = pl.BlockSpec((tm, tn), lambda i, j: (i, j))
    in_specs = [y_spec, y_spec, y_spec, y_spec,
                pl.BlockSpec((None, 4 * width, tn), lambda i, j: (layer, 0, j)), tile]
    if next_gain is None:
        return pl.pallas_call(
            functools.partial(_outproj_kernel, width=width),
            grid=(m // tm, d // tn),
            in_specs=in_specs,
            out_specs=tile,
            out_shape=jax.ShapeDtypeStruct((m, d), F32),
            compiler_params=_compiler_params(("parallel", "arbitrary"), blocks),
            name="outproj",
        )(*ys, w_all, x)
    return pl.pallas_call(
        functools.partial(_outproj_prenorm_kernel, width=width),
        grid=(m // tm, d // tn),
        in_specs=in_specs + [pl.BlockSpec((1, tn), lambda i, j: (0, j))],
        out_specs=[tile, tile, pl.BlockSpec((tm, 1), lambda i, j: (i, 0))],
        out_shape=[jax.ShapeDtypeStruct((m, d), F32), jax.ShapeDtypeStruct((m, d), BF16),
                   jax.ShapeDtypeStruct((m, 1), F32)],
        compiler_params=_compiler_params(("parallel", "arbitrary"), blocks + tm * tn * 2),
        name="outproj_prenorm",
    )(*ys, w_all, x, next_gain.reshape(1, d))


def kernel(x, norm_gain, w_in, conv_w, gmlp_v_gain, gmlp_w_s, gmlp_b_s, swa_sinks, w_out, final_gain):
    batch, seq, d_model = x.shape
    depth = w_in.shape[0]
    m = batch * seq
    w_a = w_b = w_c = w_d = d_model // 4
    n_sb_heads = w_a // HEAD_DIM
    n_swa_heads = w_b // HEAD_DIM
    n_swa_kv = max(1, n_swa_heads // 4)
    w_kv = n_swa_kv * HEAD_DIM
    n_groups = gmlp_w_s.shape[1]
    assert n_swa_heads == swa_sinks.shape[1] and w_d == n_groups * GMLP_GROUP_DIM

    widths = (w_a, w_a, w_a, w_a, w_b, w_kv, w_kv, w_b, w_c, w_c, w_c, w_c, w_d, w_d, w_d)
    offs = [0]
    for wd in widths:
        offs.append(offs[-1] + wd)
    (qa, ka, va, ga, qb, kb, vb, gb, xc, bc, cc, gc, ud, vd, gd) = offs[:-1]
    n_proj = offs[-1]
    assert n_proj == w_in.shape[2]

    scale = HEAD_DIM ** -0.5
    col = jnp.arange(n_proj)
    col_scale = jnp.where((col >= qa) & (col < qa + w_a), scale * LOG2_E,
                          jnp.where((col >= qb) & (col < qb + w_b), scale, 1.0))
    col_scale = col_scale.astype(F32).reshape(1, n_proj)

    w_out_bf16 = w_out.astype(BF16)
    h = x.reshape(m, d_model)
    hn, row_ssq = _rmsnorm(h, norm_gain[0], BF16), None
    for l in range(depth):
        proj = _inproj(hn, w_in, l, col_scale, row_ssq).reshape(batch, seq, n_proj)
        y_a = _sb_attention(proj, qa, ka, va, ga, n_sb_heads)
        bias_tw = jnp.repeat(gmlp_b_s[l].T, GMLP_GROUP_DIM, axis=1)
        y_b, y_c, y_d = _local_mixers(proj, swa_sinks[l], conv_w[l], gmlp_v_gain[l], gmlp_w_s[l], bias_tw,
                                      (qb, kb, vb, gb, xc, bc, cc, gc, ud, vd, gd), n_swa_heads, n_swa_kv, w_b)
        ys = [y.reshape(m, -1) for y in (y_a, y_b, y_c, y_d)]
        if l + 1 < depth:
            h, hn, row_ssq = _outproj(ys, w_out_bf16, l, h, norm_gain[l + 1])
        else:
            h = _outproj(ys, w_out_bf16, l, h)
    out = _rmsnorm(h, final_gain, F32)
    return out.reshape(batch, seq, d_model)
```

```python
import functools

import jax
import jax.numpy as jnp
from jax import lax
from jax.experimental import pallas as pl
from jax.experimental.pallas import tpu as pltpu

HEAD_DIM = 128
GMLP_GROUP_DIM = 128
GMLP_CHUNK = 128
WINDOW = 128
CONV_WIDTH = 3
EPS = 1e-6
LOG2_E = 1.4426950408889634
EXP2_CLAMP = 64.0
F32_EXP2_UNDERFLOW = 160.0

LANES = 128
SUBLANES = 8
V7X_VMEM_BYTES = 64 * 1024 * 1024
V7X_VMEM_RESERVED_BYTES = 4 * 1024 * 1024
VMEM_SPILL_ALLOWANCE = 6 * 1024 * 1024
MASKED_SCORE = -1e30
SB_HEADS_PER_STEP = 8
LOCAL_ROWS_PER_STEP = 512

F32 = jnp.float32
BF16 = jnp.bfloat16


def _compiler_params(semantics, block_bytes):
    limit = min(2 * block_bytes + VMEM_SPILL_ALLOWANCE, V7X_VMEM_BYTES - V7X_VMEM_RESERVED_BYTES)
    return pltpu.CompilerParams(dimension_semantics=semantics, vmem_limit_bytes=int(limit))


def _col_block(offset, width):
    assert offset % width == 0, (offset, width)
    return offset // width


def _silu(g):
    return g * jax.nn.sigmoid(g)


def _rmsnorm_kernel(x_ref, g_ref, o_ref):
    x = x_ref[...]
    ms = jnp.mean(x * x, axis=-1, keepdims=True)
    o_ref[...] = (x * lax.rsqrt(ms + EPS) * g_ref[...]).astype(o_ref.dtype)


def _rmsnorm(x, gain, out_dtype):
    m, d = x.shape
    tm = min(512, m)
    return pl.pallas_call(
        _rmsnorm_kernel,
        grid=(m // tm,),
        in_specs=[pl.BlockSpec((tm, d), lambda i: (i, 0)),
                  pl.BlockSpec((1, d), lambda i: (0, 0))],
        out_specs=pl.BlockSpec((tm, d), lambda i: (i, 0)),
        out_shape=jax.ShapeDtypeStruct((m, d), out_dtype),
        compiler_params=_compiler_params(("parallel",), tm * d * (4 + 4 + 4)),
        name="rmsnorm",
    )(x, gain.reshape(1, d))


def _inproj_kernel(h_ref, w_ref, s_ref, o_ref):
    acc = jnp.dot(h_ref[...], w_ref[...].astype(BF16), preferred_element_type=F32)
    o_ref[...] = (acc * s_ref[...]).astype(o_ref.dtype)


def _inproj_rowscaled_kernel(h_ref, ssq_ref, w_ref, s_ref, o_ref, *, d_model):
    acc = jnp.dot(h_ref[...], w_ref[...].astype(BF16), preferred_element_type=F32)
    inv_rms = lax.rsqrt(ssq_ref[...] * (1.0 / d_model) + EPS)
    o_ref[...] = (acc * inv_rms * s_ref[...]).astype(o_ref.dtype)


def _inproj(hn, w_all, layer, col_scale, row_ssq=None):
    m, d = hn.shape
    n = w_all.shape[2]
    tm = min(1024, m)
    tn = 768 if n % 768 == 0 else 512
    assert m % tm == 0 and n % tn == 0
    blocks = tm * d * 2 + d * tn * (4 + 2) + tm * tn * (2 + 4)
    h_spec = pl.BlockSpec((tm, d), lambda j, i: (i, 0))
    w_spec = pl.BlockSpec((None, d, tn), lambda j, i: (layer, 0, j))
    s_spec = pl.BlockSpec((1, tn), lambda j, i: (0, j))
    if row_ssq is None:
        body, in_specs, operands = _inproj_kernel, [h_spec, w_spec, s_spec], (hn, w_all, col_scale)
    else:
        body = functools.partial(_inproj_rowscaled_kernel, d_model=d)
        in_specs = [h_spec, pl.BlockSpec((tm, 1), lambda j, i: (i, 0)), w_spec, s_spec]
        operands = (hn, row_ssq, w_all, col_scale)
    return pl.pallas_call(
        body,
        grid=(n // tn, m // tm),
        in_specs=in_specs,
        out_specs=pl.BlockSpec((tm, tn), lambda j, i: (i, j)),
        out_shape=jax.ShapeDtypeStruct((m, n), BF16),
        compiler_params=_compiler_params(("parallel", "arbitrary"), blocks),
        name="inproj",
    )(*operands)


def _sb_attn_kernel(q_ref, k_ref, v_ref, g_ref, o_ref, z_scr, w_scr, carry_scr, acc_scr, *, tq, heads):
    i = pl.program_id(2)
    dh = HEAD_DIM
    hds = range(heads)
    row = lax.broadcasted_iota(jnp.int32, (tq, tq), 0)
    col = lax.broadcasted_iota(jnp.int32, (tq, tq), 1)
    strictly_causal = col < row
    later_keys = ((row > col) | (col == tq - 1)).astype(BF16)
    last_lane = lax.broadcasted_iota(jnp.int32, (tq, LANES), 1) == LANES - 1

    def key_rows(kb):
        return pl.ds(pl.multiple_of(kb * tq, tq), tq)

    def head_cols(hd):
        return slice(hd * dh, (hd + 1) * dh)

    def scores(kb, hd):
        return lax.dot_general(q_ref[:, head_cols(hd)], k_ref[key_rows(kb), head_cols(hd)],
                               (((1,), (1,)), ((), ())), preferred_element_type=F32)

    def softplus_terms(z, carry, diagonal):
        p = jnp.maximum(jnp.log2(1.0 + jnp.exp2(jnp.minimum(z, EXP2_CLAMP))), z)
        if diagonal:
            p = jnp.where(strictly_causal, p, 0.0)
        p = jnp.concatenate([p[:, :tq - LANES], p[:, tq - LANES:] + carry], axis=1)
        return p.astype(BF16), z - p

    def finish(u, cs, hd, diagonal):
        cs_tail = cs[:, tq - LANES:]
        carry_scr[hd] = jnp.where(last_lane, cs_tail, 0.0)
        cs = jnp.concatenate([cs[:, :tq - LANES], jnp.where(last_lane, 0.0, cs_tail)], axis=1)
        w = jnp.exp2(u - cs)
        if diagonal:
            w = jnp.where(strictly_causal, w, 0.0)
        w_scr[hd] = w.astype(BF16)

    def apply_weights(kb):
        for hd in hds:
            acc_scr[:, head_cols(hd)] += jnp.dot(w_scr[hd], v_ref[key_rows(kb), head_cols(hd)],
                                                 preferred_element_type=F32)

    acc_scr[...] = jnp.zeros_like(acc_scr)
    zs = [scores(i, hd) for hd in hds]
    zero_carry = jnp.zeros((tq, LANES), F32)
    terms = [softplus_terms(zs[hd], zero_carry, True) for hd in hds]
    nxt = jnp.maximum(i - 1, 0)
    for hd in hds:
        p_bf16, u = terms[hd]
        cs = jnp.dot(p_bf16, later_keys, preferred_element_type=F32)
        finish(u, cs, hd, True)
        z_scr[0, hd] = scores(nxt, hd)

    def step(kb, z_in, z_out):
        apply_weights(kb + 1)
        nxt = jnp.maximum(kb - 1, 0)
        for hd in hds:
            z_scr[z_out, hd] = scores(nxt, hd)
        terms = []
        for hd in hds:
            p_bf16, u = softplus_terms(z_scr[z_in, hd], carry_scr[hd], False)
            z_scr[z_in, hd] = u
            terms.append(p_bf16)
        for hd in hds:
            cs = jnp.dot(terms[hd], later_keys, preferred_element_type=F32)
            finish(z_scr[z_in, hd], cs, hd, False)

    def stick_used_up():
        swept = carry_scr[0]
        for hd in range(1, heads):
            swept = jnp.minimum(swept, carry_scr[hd])
        return jnp.max(jnp.min(swept, axis=0, keepdims=True)) >= F32_EXP2_UNDERFLOW

    n_pairs = i // 2

    def two_steps(state):
        m, _, _ = state
        kb = i - 1 - 2 * m
        step(kb, 0, 1)
        used_up = stick_used_up()

        @pl.when(jnp.logical_not(used_up))
        def _():
            step(kb - 1, 1, 0)

        return m + 1, jnp.where(used_up, kb, kb - 1), used_up

    pairs_done, last_block, used_up = lax.while_loop(
        lambda st: jnp.logical_and(st[0] < n_pairs, jnp.logical_not(st[2])),
        two_steps, (jnp.int32(0), i, jnp.bool_(False)))
    odd_step = jnp.logical_and(jnp.logical_and(i % 2 == 1, pairs_done == n_pairs), jnp.logical_not(used_up))

    @pl.when(odd_step)
    def _():
        step(0, 0, 1)

    apply_weights(jnp.where(odd_step, 0, last_block))
    o_ref[...] = (acc_scr[...] * _silu(g_ref[...].astype(F32))).astype(o_ref.dtype)


def _sb_attention(proj, q_off, k_off, v_off, g_off, n_heads):
    b, s, _ = proj.shape
    tq = min(256, s)
    heads = min(SB_HEADS_PER_STEP, n_heads)
    assert s % tq == 0 and tq % LANES == 0 and n_heads % heads == 0
    gw = heads * HEAD_DIM
    qb, kb, vb, gb = (_col_block(o, gw) for o in (q_off, k_off, v_off, g_off))
    blocks = 3 * tq * gw * 2 + s * gw * 2 + heads * 6 * tq * tq * 4
    return pl.pallas_call(
        functools.partial(_sb_attn_kernel, tq=tq, heads=heads),
        grid=(b, n_heads // heads, s // tq),
        in_specs=[pl.BlockSpec((None, tq, gw), lambda bi, h, i: (bi, i, qb + h)),
                  pl.BlockSpec((None, s, gw), lambda bi, h, i: (bi, 0, kb + h), pipeline_mode=pl.Buffered(1)),
                  pl.BlockSpec((None, s, gw), lambda bi, h, i: (bi, 0, vb + h), pipeline_mode=pl.Buffered(1)),
                  pl.BlockSpec((None, tq, gw), lambda bi, h, i: (bi, i, gb + h))],
        out_specs=pl.BlockSpec((None, tq, gw), lambda bi, h, i: (bi, i, h)),
        out_shape=jax.ShapeDtypeStruct((b, s, n_heads * HEAD_DIM), BF16),
        scratch_shapes=[pltpu.VMEM((2, heads, tq, tq), F32),
                        pltpu.VMEM((heads, tq, tq), BF16),
                        pltpu.VMEM((heads, tq, LANES), F32),
                        pltpu.VMEM((tq, gw), F32)],
        compiler_params=_compiler_params(("parallel", "parallel", "arbitrary"), blocks),
        name="sb_attention",
    )(proj, proj, proj, proj)


def _local_mixers_kernel(sink_ref,
                         q_ref, kp_ref, kc_ref, vp_ref, vc_ref, gb0_ref, gb1_ref,
                         x0_ref, x1_ref, xp0_ref, xp1_ref, c0_ref, c1_ref, cp0_ref, cp1_ref,
                         b0_ref, b1_ref, gc0_ref, gc1_ref, cw_ref,
                         u0_ref, u1_ref, v0_ref, v1_ref, gd0_ref, gd1_ref, vg_ref, ws_ref, bias_ref,
                         yb_ref, yc_ref, yd_ref, *, n_kv, group, n_sub):
    n = pl.program_id(1)

    def both(r0, r1):
        return jnp.concatenate([r0[...], r1[...]], axis=1).astype(F32)

    k_all = jnp.concatenate([kp_ref[...], kc_ref[...]], axis=0)
    v_all = jnp.concatenate([vp_ref[...], vc_ref[...]], axis=0)
    r = lax.broadcasted_iota(jnp.int32, (WINDOW, 2 * WINDOW), 0)
    c = lax.broadcasted_iota(jnp.int32, (WINDOW, 2 * WINDOW), 1)
    in_window = (c > r) & (c <= r + WINDOW)
    gate_b = _silu(both(gb0_ref, gb1_ref))
    for sub in range(n_sub):
        rows = slice(sub * WINDOW, (sub + 1) * WINDOW)
        valid = in_window & ((c >= WINDOW) | (n > 0)) if sub == 0 else in_window
        outs = []
        for kv in range(n_kv):
            k = k_all[sub * WINDOW:(sub + 2) * WINDOW, kv * HEAD_DIM:(kv + 1) * HEAD_DIM]
            v = v_all[sub * WINDOW:(sub + 2) * WINDOW, kv * HEAD_DIM:(kv + 1) * HEAD_DIM]
            for g in range(group):
                head = kv * group + g
                qg = q_ref[rows, head * HEAD_DIM:(head + 1) * HEAD_DIM]
                s = lax.dot_general(qg, k, (((1,), (1,)), ((), ())), preferred_element_type=F32)
                s = jnp.where(valid, s, MASKED_SCORE)
                sink = sink_ref[head]
                m = jnp.maximum(jnp.max(s, axis=-1, keepdims=True), sink)
                p = jnp.exp(s - m)
                denom = jnp.sum(p, axis=-1, keepdims=True) + jnp.exp(sink - m)
                outs.append(jnp.dot(p.astype(BF16), v, preferred_element_type=F32) / denom)
        o = jnp.concatenate(outs, axis=1)
        yb_ref[rows, :] = (o * gate_b[rows]).astype(yb_ref.dtype)

    h = both(c0_ref, c1_ref) * both(x0_ref, x1_ref)
    hp = both(cp0_ref, cp1_ref) * both(xp0_ref, xp1_ref)
    hp = jnp.where(n > 0, hp, 0.0)
    ext = jnp.concatenate([hp, h], axis=0)
    h1 = pltpu.roll(ext, 1, 0)[SUBLANES:]
    h2 = pltpu.roll(ext, 2, 0)[SUBLANES:]
    w = cw_ref[...]
    y = w[0:1] * h2 + w[1:2] * h1 + w[2:3] * h
    y = both(b0_ref, b1_ref) * y
    yc_ref[...] = (y * _silu(both(gc0_ref, gc1_ref))).astype(yc_ref.dtype)

    v = both(v0_ref, v1_ref)
    ms = jnp.mean(v * v, axis=-1, keepdims=True)
    vn = (v * lax.rsqrt(ms + EPS) * vg_ref[...]).astype(BF16)
    rr = lax.broadcasted_iota(jnp.int32, (GMLP_CHUNK, GMLP_CHUNK), 0)
    cc = lax.broadcasted_iota(jnp.int32, (GMLP_CHUNK, GMLP_CHUNK), 1)
    causal = cc <= rr
    gd = GMLP_GROUP_DIM
    n_groups = ws_ref.shape[0]
    n_chunks = v.shape[0] // GMLP_CHUNK
    mixed = []
    for g in range(n_groups):
        wsg = jnp.where(causal, ws_ref[g], 0.0).astype(BF16)
        rhs = jnp.concatenate(
            [vn[ch * GMLP_CHUNK:(ch + 1) * GMLP_CHUNK, g * gd:(g + 1) * gd] for ch in range(n_chunks)], axis=1)
        mixed.append(jnp.dot(wsg, rhs, preferred_element_type=F32))
    bias = bias_ref[...]
    chunks = []
    for ch in range(n_chunks):
        chunks.append(jnp.concatenate([mixed[g][:, ch * gd:(ch + 1) * gd] for g in range(n_groups)], axis=1) + bias)
    mix = jnp.concatenate(chunks, axis=0)
    y = both(u0_ref, u1_ref) * mix
    yd_ref[...] = (y * _silu(both(gd0_ref, gd1_ref))).astype(yd_ref.dtype)


def _local_mixers(proj, sinks, conv_w, v_gain, w_s, bias_tw, offs, n_heads, n_kv, width):
    qb, kb, vb, gb, xc, bc, cc, gc, ud, vd, gd = offs
    b, s, _ = proj.shape
    rows = min(LOCAL_ROWS_PER_STEP, s)
    cw = width // 2
    w_kv = n_kv * HEAD_DIM
    assert s % rows == 0 and rows % WINDOW == 0 and rows % GMLP_CHUNK == 0
    assert n_heads * HEAD_DIM == width and w_s.shape[0] * GMLP_GROUP_DIM == width

    def seg(off, w):
        col = _col_block(off, w)
        return pl.BlockSpec((None, rows, w), lambda bi, n: (bi, n, col))

    def seg_before(off, w, n_rows):
        col = _col_block(off, w)
        per_step = rows // n_rows
        return pl.BlockSpec((None, n_rows, w), lambda bi, n: (bi, jnp.maximum(n * per_step - 1, 0), col))

    def halves(off):
        return [seg(off, cw), seg(off + cw, cw)]

    def halves_before(off):
        return [seg_before(off, cw, SUBLANES), seg_before(off + cw, cw, SUBLANES)]

    def whole(shape):
        return pl.BlockSpec(shape, lambda bi, n: (0,) * len(shape))

    in_specs = ([pl.BlockSpec(memory_space=pltpu.SMEM),
                 seg(qb, width), seg_before(kb, w_kv, WINDOW), seg(kb, w_kv),
                 seg_before(vb, w_kv, WINDOW), seg(vb, w_kv)] + halves(gb)
                + halves(xc) + halves_before(xc) + halves(cc) + halves_before(cc) + halves(bc) + halves(gc)
                + [whole((CONV_WIDTH, width))]
                + halves(ud) + halves(vd) + halves(gd)
                + [whole((1, width)), whole(w_s.shape), whole((GMLP_CHUNK, width))])
    operands = [sinks] + [proj] * 7 + [proj] * 12 + [conv_w] + [proj] * 6 + [v_gain.reshape(1, width), w_s, bias_tw]
    assert len(operands) == len(in_specs)
    out_spec = pl.BlockSpec((None, rows, width), lambda bi, n: (bi, n, 0))
    out_shape = jax.ShapeDtypeStruct((b, s, width), BF16)
    blocks = (17 + 3) * rows * cw * 2 + 12 * rows * width * 4
    return pl.pallas_call(
        functools.partial(_local_mixers_kernel, n_kv=n_kv, group=n_heads // n_kv, n_sub=rows // WINDOW),
        grid=(b, s // rows),
        in_specs=in_specs,
        out_specs=[out_spec, out_spec, out_spec],
        out_shape=[out_shape, out_shape, out_shape],
        compiler_params=_compiler_params(("parallel", "arbitrary"), blocks),
        name="local_mixers",
    )(*operands)


def _residual_update(y_refs, w_ref, x_ref, width):
    acc = x_ref[...]
    for idx, y_ref in enumerate(y_refs):
        acc = acc + jnp.dot(y_ref[...], w_ref[idx * width:(idx + 1) * width, :], preferred_element_type=F32)
    return acc


def _outproj_kernel(ya_ref, yb_ref, yc_ref, yd_ref, w_ref, x_ref, o_ref, *, width):
    o_ref[...] = _residual_update((ya_ref, yb_ref, yc_ref, yd_ref), w_ref, x_ref, width)


def _outproj_prenorm_kernel(ya_ref, yb_ref, yc_ref, yd_ref, w_ref, x_ref, g_ref, o_ref, hg_ref, ssq_ref, *, width):
    acc = _residual_update((ya_ref, yb_ref, yc_ref, yd_ref), w_ref, x_ref, width)
    o_ref[...] = acc
    hg_ref[...] = (acc * g_ref[...]).astype(hg_ref.dtype)
    part = jnp.sum(acc * acc, axis=-1, keepdims=True)
    j = pl.program_id(1)

    @pl.when(j == 0)
    def _():
        ssq_ref[...] = part

    @pl.when(j > 0)
    def _():
        ssq_ref[...] += part


def _outproj(ys, w_all, layer, x, next_gain=None):
    m, d = x.shape
    width = ys[0].shape[1]
    tm = min(1024, m)
    tn = min(1024, d)
    assert m % tm == 0 and d % tn == 0
    blocks = 4 * tm * width * 2 + 4 * width * tn * 2 + 3 * tm * tn * 4
    y_spec = pl.BlockSpec((tm, width), lambda i, j: (i, 0))
    tile = pl.BlockSpec((tm, tn), lambda i, j: (i, j))
    in_specs = [y_spec, y_spec, y_spec, y_spec,
                pl.BlockSpec((None, 4 * width, tn), lambda i, j: (layer, 0, j)), tile]
    if next_gain is None:
        return pl.pallas_call(
            functools.partial(_outproj_kernel, width=width),
            grid=(m // tm, d // tn),
            in_specs=in_specs,
            out_specs=tile,
            out_shape=jax.ShapeDtypeStruct((m, d), F32),
            compiler_params=_compiler_params(("parallel", "arbitrary"), blocks),
            name="outproj",
        )(*ys, w_all, x)
    return pl.pallas_call(
        functools.partial(_outproj_prenorm_kernel, width=width),
        grid=(m // tm, d // tn),
        in_specs=in_specs + [pl.BlockSpec((1, tn), lambda i, j: (0, j))],
        out_specs=[tile, tile, pl.BlockSpec((tm, 1), lambda i, j: (i, 0))],
        out_shape=[jax.ShapeDtypeStruct((m, d), F32), jax.ShapeDtypeStruct((m, d), BF16),
                   jax.ShapeDtypeStruct((m, 1), F32)],
        compiler_params=_compiler_params(("parallel", "arbitrary"), blocks + tm * tn * 2),
        name="outproj_prenorm",
    )(*ys, w_all, x, next_gain.reshape(1, d))


def kernel(x, norm_gain, w_in, conv_w, gmlp_v_gain, gmlp_w_s, gmlp_b_s, swa_sinks, w_out, final_gain):
    batch, seq, d_model = x.shape
    depth = w_in.shape[0]
    m = batch * seq
    w_a = w_b = w_c = w_d = d_model // 4
    n_sb_heads = w_a // HEAD_DIM
    n_swa_heads = w_b // HEAD_DIM
    n_swa_kv = max(1, n_swa_heads // 4)
    w_kv = n_swa_kv * HEAD_DIM
    n_groups = gmlp_w_s.shape[1]
    assert n_swa_heads == swa_sinks.shape[1] and w_d == n_groups * GMLP_GROUP_DIM

    widths = (w_a, w_a, w_a, w_a, w_b, w_kv, w_kv, w_b, w_c, w_c, w_c, w_c, w_d, w_d, w_d)
    offs = [0]
    for wd in widths:
        offs.append(offs[-1] + wd)
    (qa, ka, va, ga, qb, kb, vb, gb, xc, bc, cc, gc, ud, vd, gd) = offs[:-1]
    n_proj = offs[-1]
    assert n_proj == w_in.shape[2]

    scale = HEAD_DIM ** -0.5
    col = jnp.arange(n_proj)
    col_scale = jnp.where((col >= qa) & (col < qa + w_a), scale * LOG2_E,
                          jnp.where((col >= qb) & (col < qb + w_b), scale, 1.0))
    col_scale = col_scale.astype(F32).reshape(1, n_proj)

    w_out_bf16 = w_out.astype(BF16)
    h = x.reshape(m, d_model)
    hn, row_ssq = _rmsnorm(h, norm_gain[0], BF16), None
    for l in range(depth):
        proj = _inproj(hn, w_in, l, col_scale, row_ssq).reshape(batch, seq, n_proj)
        y_a = _sb_attention(proj, qa, ka, va, ga, n_sb_heads)
        bias_tw = jnp.repeat(gmlp_b_s[l].T, GMLP_GROUP_DIM, axis=1)
        y_b, y_c, y_d = _local_mixers(proj, swa_sinks[l], conv_w[l], gmlp_v_gain[l], gmlp_w_s[l], bias_tw,
                                      (qb, kb, vb, gb, xc, bc, cc, gc, ud, vd, gd), n_swa_heads, n_swa_kv, w_b)
        ys = [y.reshape(m, -1) for y in (y_a, y_b, y_c, y_d)]
        if l + 1 < depth:
            h, hn, row_ssq = _outproj(ys, w_out_bf16, l, h, norm_gain[l + 1])
        else:
            h = _outproj(ys, w_out_bf16, l, h)
    out = _rmsnorm(h, final_gain, F32)
    return out.reshape(batch, seq, d_model)
```

```python
import functools

import jax
import jax.numpy as jnp
from jax import lax
from jax.experimental import pallas as pl
from jax.experimental.pallas import tpu as pltpu

HEAD_DIM = 128
GMLP_GROUP_DIM = 128
GMLP_CHUNK = 128
WINDOW = 128
CONV_WIDTH = 3
EPS = 1e-6
LOG2_E = 1.4426950408889634
EXP2_CLAMP = 64.0
F32_EXP2_UNDERFLOW = 160.0

LANES = 128
SUBLANES = 8
V7X_VMEM_BYTES = 64 * 1024 * 1024
V7X_VMEM_RESERVED_BYTES = 4 * 1024 * 1024
VMEM_SPILL_ALLOWANCE = 6 * 1024 * 1024
MASKED_SCORE = -1e30
SB_HEADS_PER_STEP = 8
LOCAL_ROWS_PER_STEP = 512

F32 = jnp.float32
BF16 = jnp.bfloat16


def _compiler_params(semantics, block_bytes):
    limit = min(2 * block_bytes + VMEM_SPILL_ALLOWANCE, V7X_VMEM_BYTES - V7X_VMEM_RESERVED_BYTES)
    return pltpu.CompilerParams(dimension_semantics=semantics, vmem_limit_bytes=int(limit))


def _col_block(offset, width):
    assert offset % width == 0, (offset, width)
    return offset // width


def _silu(g):
    return g * jax.nn.sigmoid(g)


def _rmsnorm_kernel(x_ref, g_ref, o_ref):
    x = x_ref[...]
    ms = jnp.mean(x * x, axis=-1, keepdims=True)
    o_ref[...] = (x * lax.rsqrt(ms + EPS) * g_ref[...]).astype(o_ref.dtype)


def _rmsnorm(x, gain, out_dtype):
    m, d = x.shape
    tm = min(512, m)
    return pl.pallas_call(
        _rmsnorm_kernel,
        grid=(m // tm,),
        in_specs=[pl.BlockSpec((tm, d), lambda i: (i, 0)),
                  pl.BlockSpec((1, d), lambda i: (0, 0))],
        out_specs=pl.BlockSpec((tm, d), lambda i: (i, 0)),
        out_shape=jax.ShapeDtypeStruct((m, d), out_dtype),
        compiler_params=_compiler_params(("parallel",), tm * d * (4 + 4 + 4)),
        name="rmsnorm",
    )(x, gain.reshape(1, d))


def _cast_weights_once(w_ref, wb_scr):
    @pl.when(pl.program_id(1) == 0)
    def _():
        wb_scr[...] = w_ref[...].astype(BF16)


def _inproj_kernel(h_ref, w_ref, s_ref, o_ref, wb_scr):
    _cast_weights_once(w_ref, wb_scr)
    acc = jnp.dot(h_ref[...], wb_scr[...], preferred_element_type=F32)
    o_ref[...] = (acc * s_ref[...]).astype(o_ref.dtype)


def _inproj_rowscaled_kernel(h_ref, ssq_ref, w_ref, s_ref, o_ref, wb_scr, *, d_model):
    _cast_weights_once(w_ref, wb_scr)
    acc = jnp.dot(h_ref[...], wb_scr[...], preferred_element_type=F32)
    inv_rms = lax.rsqrt(ssq_ref[...] * (1.0 / d_model) + EPS)
    o_ref[...] = (acc * inv_rms * s_ref[...]).astype(o_ref.dtype)


def _inproj(hn, w_all, layer, col_scale, row_ssq=None):
    m, d = hn.shape
    n = w_all.shape[2]
    tm = min(1024, m)
    tn = 768 if n % 768 == 0 else 512
    assert m % tm == 0 and n % tn == 0
    blocks = tm * d * 2 + d * tn * (4 + 2) + tm * tn * (2 + 4)
    h_spec = pl.BlockSpec((tm, d), lambda j, i: (i, 0))
    w_spec = pl.BlockSpec((None, d, tn), lambda j, i: (layer, 0, j))
    s_spec = pl.BlockSpec((1, tn), lambda j, i: (0, j))
    if row_ssq is None:
        body, in_specs, operands = _inproj_kernel, [h_spec, w_spec, s_spec], (hn, w_all, col_scale)
    else:
        body = functools.partial(_inproj_rowscaled_kernel, d_model=d)
        in_specs = [h_spec, pl.BlockSpec((tm, 1), lambda j, i: (i, 0)), w_spec, s_spec]
        operands = (hn, row_ssq, w_all, col_scale)
    return pl.pallas_call(
        body,
        grid=(n // tn, m // tm),
        in_specs=in_specs,
        out_specs=pl.BlockSpec((tm, tn), lambda j, i: (i, j)),
        out_shape=jax.ShapeDtypeStruct((m, n), BF16),
        scratch_shapes=[pltpu.VMEM((d, tn), BF16)],
        compiler_params=_compiler_params(("arbitrary", "arbitrary"), blocks),
        name="inproj",
    )(*operands)


def _sb_attn_kernel(q_ref, k_ref, v_ref, g_ref, o_ref, z_scr, w_scr, carry_scr, acc_scr, *, tq, heads):
    i = pl.program_id(2)
    dh = HEAD_DIM
    hds = range(heads)
    row = lax.broadcasted_iota(jnp.int32, (tq, tq), 0)
    col = lax.broadcasted_iota(jnp.int32, (tq, tq), 1)
    strictly_causal = col < row
    later_keys = ((row > col) | (col == tq - 1)).astype(BF16)
    last_lane = lax.broadcasted_iota(jnp.int32, (tq, LANES), 1) == LANES - 1

    def key_rows(kb):
        return pl.ds(pl.multiple_of(kb * tq, tq), tq)

    def head_cols(hd):
        return slice(hd * dh, (hd + 1) * dh)

    def scores(kb, hd):
        return lax.dot_general(q_ref[:, head_cols(hd)], k_ref[key_rows(kb), head_cols(hd)],
                               (((1,), (1,)), ((), ())), preferred_element_type=F32)

    def softplus_terms(z, carry, diagonal):
        p = jnp.maximum(jnp.log2(1.0 + jnp.exp2(jnp.minimum(z, EXP2_CLAMP))), z)
        if diagonal:
            p = jnp.where(strictly_causal, p, 0.0)
        p = jnp.concatenate([p[:, :tq - LANES], p[:, tq - LANES:] + carry], axis=1)
        return p.astype(BF16), z - p

    def finish(u, cs, hd, diagonal):
        cs_tail = cs[:, tq - LANES:]
        carry_scr[hd] = jnp.where(last_lane, cs_tail, 0.0)
        cs = jnp.concatenate([cs[:, :tq - LANES], jnp.where(last_lane, 0.0, cs_tail)], axis=1)
        w = jnp.exp2(u - cs)
        if diagonal:
            w = jnp.where(strictly_causal, w, 0.0)
        w_scr[hd] = w.astype(BF16)

    def apply_weights(kb):
        for hd in hds:
            acc_scr[:, head_cols(hd)] += jnp.dot(w_scr[hd], v_ref[key_rows(kb), head_cols(hd)],
                                                 preferred_element_type=F32)

    acc_scr[...] = jnp.zeros_like(acc_scr)
    zs = [scores(i, hd) for hd in hds]
    zero_carry = jnp.zeros((tq, LANES), F32)
    terms = [softplus_terms(zs[hd], zero_carry, True) for hd in hds]
    nxt = jnp.maximum(i - 1, 0)
    for hd in hds:
        p_bf16, u = terms[hd]
        cs = jnp.dot(p_bf16, later_keys, preferred_element_type=F32)
        finish(u, cs, hd, True)
        z_scr[0, hd] = scores(nxt, hd)

    def step(kb, z_in, z_out):
        apply_weights(kb + 1)
        nxt = jnp.maximum(kb - 1, 0)
        for hd in hds:
            z_scr[z_out, hd] = scores(nxt, hd)
        terms = []
        for hd in hds:
            p_bf16, u = softplus_terms(z_scr[z_in, hd], carry_scr[hd], False)
            z_scr[z_in, hd] = u
            terms.append(p_bf16)
        for hd in hds:
            cs = jnp.dot(terms[hd], later_keys, preferred_element_type=F32)
            finish(z_scr[z_in, hd], cs, hd, False)

    def stick_used_up():
        swept = carry_scr[0]
        for hd in range(1, heads):
            swept = jnp.minimum(swept, carry_scr[hd])
        return jnp.max(jnp.min(swept, axis=0, keepdims=True)) >= F32_EXP2_UNDERFLOW

    n_pairs = i // 2

    def two_steps(state):
        m, _, _ = state
        kb = i - 1 - 2 * m
        step(kb, 0, 1)
        used_up = stick_used_up()

        @pl.when(jnp.logical_not(used_up))
        def _():
            step(kb - 1, 1, 0)

        return m + 1, jnp.where(used_up, kb, kb - 1), used_up

    pairs_done, last_block, used_up = lax.while_loop(
        lambda st: jnp.logical_and(st[0] < n_pairs, jnp.logical_not(st[2])),
        two_steps, (jnp.int32(0), i, jnp.bool_(False)))
    odd_step = jnp.logical_and(jnp.logical_and(i % 2 == 1, pairs_done == n_pairs), jnp.logical_not(used_up))

    @pl.when(odd_step)
    def _():
        step(0, 0, 1)

    apply_weights(jnp.where(odd_step, 0, last_block))
    o_ref[...] = (acc_scr[...] * _silu(g_ref[...].astype(F32))).astype(o_ref.dtype)


def _sb_attention(proj, q_off, k_off, v_off, g_off, n_heads):
    b, s, _ = proj.shape
    tq = min(256, s)
    heads = min(SB_HEADS_PER_STEP, n_heads)
    assert s % tq == 0 and tq % LANES == 0 and n_heads % heads == 0
    gw = heads * HEAD_DIM
    qb, kb, vb, gb = (_col_block(o, gw) for o in (q_off, k_off, v_off, g_off))
    blocks = 3 * tq * gw * 2 + s * gw * 2 + heads * 6 * tq * tq * 4
    return pl.pallas_call(
        functools.partial(_sb_attn_kernel, tq=tq, heads=heads),
        grid=(b, n_heads // heads, s // tq),
        in_specs=[pl.BlockSpec((None, tq, gw), lambda bi, h, i: (bi, i, qb + h)),
                  pl.BlockSpec((None, s, gw), lambda bi, h, i: (bi, 0, kb + h), pipeline_mode=pl.Buffered(1)),
                  pl.BlockSpec((None, s, gw), lambda bi, h, i: (bi, 0, vb + h), pipeline_mode=pl.Buffered(1)),
                  pl.BlockSpec((None, tq, gw), lambda bi, h, i: (bi, i, gb + h))],
        out_specs=pl.BlockSpec((None, tq, gw), lambda bi, h, i: (bi, i, h)),
        out_shape=jax.ShapeDtypeStruct((b, s, n_heads * HEAD_DIM), BF16),
        scratch_shapes=[pltpu.VMEM((2, heads, tq, tq), F32),
                        pltpu.VMEM((heads, tq, tq), BF16),
                        pltpu.VMEM((heads, tq, LANES), F32),
                        pltpu.VMEM((tq, gw), F32)],
        compiler_params=_compiler_params(("parallel", "parallel", "arbitrary"), blocks),
        name="sb_attention",
    )(proj, proj, proj, proj)


def _local_mixers_kernel(sink_ref,
                         q_ref, kp_ref, kc_ref, vp_ref, vc_ref, gb0_ref, gb1_ref,
                         x0_ref, x1_ref, xp0_ref, xp1_ref, c0_ref, c1_ref, cp0_ref, cp1_ref,
                         b0_ref, b1_ref, gc0_ref, gc1_ref, cw_ref,
                         u0_ref, u1_ref, v0_ref, v1_ref, gd0_ref, gd1_ref, vg_ref, ws_ref, bias_ref,
                         yb_ref, yc_ref, yd_ref, *, n_kv, group, n_sub):
    n = pl.program_id(1)

    def both(r0, r1):
        return jnp.concatenate([r0[...], r1[...]], axis=1).astype(F32)

    k_all = jnp.concatenate([kp_ref[...], kc_ref[...]], axis=0)
    v_all = jnp.concatenate([vp_ref[...], vc_ref[...]], axis=0)
    r = lax.broadcasted_iota(jnp.int32, (WINDOW, 2 * WINDOW), 0)
    c = lax.broadcasted_iota(jnp.int32, (WINDOW, 2 * WINDOW), 1)
    in_window = (c > r) & (c <= r + WINDOW)
    gate_b = _silu(both(gb0_ref, gb1_ref))
    for sub in range(n_sub):
        rows = slice(sub * WINDOW, (sub + 1) * WINDOW)
        valid = in_window & ((c >= WINDOW) | (n > 0)) if sub == 0 else in_window
        outs = []
        for kv in range(n_kv):
            k = k_all[sub * WINDOW:(sub + 2) * WINDOW, kv * HEAD_DIM:(kv + 1) * HEAD_DIM]
            v = v_all[sub * WINDOW:(sub + 2) * WINDOW, kv * HEAD_DIM:(kv + 1) * HEAD_DIM]
            for g in range(group):
                head = kv * group + g
                qg = q_ref[rows, head * HEAD_DIM:(head + 1) * HEAD_DIM]
                s = lax.dot_general(qg, k, (((1,), (1,)), ((), ())), preferred_element_type=F32)
                s = jnp.where(valid, s, MASKED_SCORE)
                sink = sink_ref[head]
                m = jnp.maximum(jnp.max(s, axis=-1, keepdims=True), sink)
                p = jnp.exp(s - m)
                denom = jnp.sum(p, axis=-1, keepdims=True) + jnp.exp(sink - m)
                outs.append(jnp.dot(p.astype(BF16), v, preferred_element_type=F32) / denom)
        o = jnp.concatenate(outs, axis=1)
        yb_ref[rows, :] = (o * gate_b[rows]).astype(yb_ref.dtype)

    h = both(c0_ref, c1_ref) * both(x0_ref, x1_ref)
    hp = both(cp0_ref, cp1_ref) * both(xp0_ref, xp1_ref)
    hp = jnp.where(n > 0, hp, 0.0)
    ext = jnp.concatenate([hp, h], axis=0)
    h1 = pltpu.roll(ext, 1, 0)[SUBLANES:]
    h2 = pltpu.roll(ext, 2, 0)[SUBLANES:]
    w = cw_ref[...]
    y = w[0:1] * h2 + w[1:2] * h1 + w[2:3] * h
    y = both(b0_ref, b1_ref) * y
    yc_ref[...] = (y * _silu(both(gc0_ref, gc1_ref))).astype(yc_ref.dtype)

    v = both(v0_ref, v1_ref)
    ms = jnp.mean(v * v, axis=-1, keepdims=True)
    vn = (v * lax.rsqrt(ms + EPS) * vg_ref[...]).astype(BF16)
    rr = lax.broadcasted_iota(jnp.int32, (GMLP_CHUNK, GMLP_CHUNK), 0)
    cc = lax.broadcasted_iota(jnp.int32, (GMLP_CHUNK, GMLP_CHUNK), 1)
    causal = cc <= rr
    gd = GMLP_GROUP_DIM
    n_groups = ws_ref.shape[0]
    n_chunks = v.shape[0] // GMLP_CHUNK
    mixed = []
    for g in range(n_groups):
        wsg = jnp.where(causal, ws_ref[g], 0.0).astype(BF16)
        rhs = jnp.concatenate(
            [vn[ch * GMLP_CHUNK:(ch + 1) * GMLP_CHUNK, g * gd:(g + 1) * gd] for ch in range(n_chunks)], axis=1)
        mixed.append(jnp.dot(wsg, rhs, preferred_element_type=F32))
    bias = bias_ref[...]
    chunks = []
    for ch in range(n_chunks):
        chunks.append(jnp.concatenate([mixed[g][:, ch * gd:(ch + 1) * gd] for g in range(n_groups)], axis=1) + bias)
    mix = jnp.concatenate(chunks, axis=0)
    y = both(u0_ref, u1_ref) * mix
    yd_ref[...] = (y * _silu(both(gd0_ref, gd1_ref))).astype(yd_ref.dtype)


def _local_mixers(proj, sinks, conv_w, v_gain, w_s, bias_tw, offs, n_heads, n_kv, width):
    qb, kb, vb, gb, xc, bc, cc, gc, ud, vd, gd = offs
    b, s, _ = proj.shape
    rows = min(LOCAL_ROWS_PER_STEP, s)
    cw = width // 2
    w_kv = n_kv * HEAD_DIM
    assert s % rows == 0 and rows % WINDOW == 0 and rows % GMLP_CHUNK == 0
    assert n_heads * HEAD_DIM == width and w_s.shape[0] * GMLP_GROUP_DIM == width

    def seg(off, w):
        col = _col_block(off, w)
        return pl.BlockSpec((None, rows, w), lambda bi, n: (bi, n, col))

    def seg_before(off, w, n_rows):
        col = _col_block(off, w)
        per_step = rows // n_rows
        return pl.BlockSpec((None, n_rows, w), lambda bi, n: (bi, jnp.maximum(n * per_step - 1, 0), col))

    def halves(off):
        return [seg(off, cw), seg(off + cw, cw)]

    def halves_before(off):
        return [seg_before(off, cw, SUBLANES), seg_before(off + cw, cw, SUBLANES)]

    def whole(shape):
        return pl.BlockSpec(shape, lambda bi, n: (0,) * len(shape))

    in_specs = ([pl.BlockSpec(memory_space=pltpu.SMEM),
                 seg(qb, width), seg_before(kb, w_kv, WINDOW), seg(kb, w_kv),
                 seg_before(vb, w_kv, WINDOW), seg(vb, w_kv)] + halves(gb)
                + halves(xc) + halves_before(xc) + halves(cc) + halves_before(cc) + halves(bc) + halves(gc)
                + [whole((CONV_WIDTH, width))]
                + halves(ud) + halves(vd) + halves(gd)
                + [whole((1, width)), whole(w_s.shape), whole((GMLP_CHUNK, width))])
    operands = [sinks] + [proj] * 7 + [proj] * 12 + [conv_w] + [proj] * 6 + [v_gain.reshape(1, width), w_s, bias_tw]
    assert len(operands) == len(in_specs)
    out_spec = pl.BlockSpec((None, rows, width), lambda bi, n: (bi, n, 0))
    out_shape = jax.ShapeDtypeStruct((b, s, width), BF16)
    blocks = (17 + 3) * rows * cw * 2 + 12 * rows * width * 4
    return pl.pallas_call(
        functools.partial(_local_mixers_kernel, n_kv=n_kv, group=n_heads // n_kv, n_sub=rows // WINDOW),
        grid=(b, s // rows),
        in_specs=in_specs,
        out_specs=[out_spec, out_spec, out_spec],
        out_shape=[out_shape, out_shape, out_shape],
        compiler_params=_compiler_params(("parallel", "arbitrary"), blocks),
        name="local_mixers",
    )(*operands)


def _residual_update(y_refs, w_ref, x_ref, width):
    acc = x_ref[...]
    for idx, y_ref in enumerate(y_refs):
        acc = acc + jnp.dot(y_ref[...], w_ref[idx * width:(idx + 1) * width, :], preferred_element_type=F32)
    return acc


def _outproj_kernel(ya_ref, yb_ref, yc_ref, yd_ref, w_ref, x_ref, o_ref, *, width):
    o_ref[...] = _residual_update((ya_ref, yb_ref, yc_ref, yd_ref), w_ref, x_ref, width)


def _outproj_prenorm_kernel(ya_ref, yb_ref, yc_ref, yd_ref, w_ref, x_ref, g_ref, o_ref, hg_ref, ssq_ref, *, width):
    acc = _residual_update((ya_ref, yb_ref, yc_ref, yd_ref), w_ref, x_ref, width)
    o_ref[...] = acc
    hg_ref[...] = (acc * g_ref[...]).astype(hg_ref.dtype)
    part = jnp.sum(acc * acc, axis=-1, keepdims=True)
    j = pl.program_id(1)

    @pl.when(j == 0)
    def _():
        ssq_ref[...] = part

    @pl.when(j > 0)
    def _():
        ssq_ref[...] += part


def _outproj(ys, w_all, layer, x, next_gain=None):
    m, d = x.shape
    width = ys[0].shape[1]
    tm = min(1024, m)
    tn = min(1024, d)
    assert m % tm == 0 and d % tn == 0
    blocks = 4 * tm * width * 2 + 4 * width * tn * 2 + 3 * tm * tn * 4
    y_spec = pl.BlockSpec((tm, width), lambda i, j: (i, 0))
    tile = pl.BlockSpec((tm, tn), lambda i, j: (i, j))
    in_specs = [y_spec, y_spec, y_spec, y_spec,
                pl.BlockSpec((None, 4 * width, tn), lambda i, j: (layer, 0, j)), tile]
    if next_gain is None:
        return pl.pallas_call(
            functools.partial(_outproj_kernel, width=width),
            grid=(m // tm, d // tn),
            in_specs=in_specs,
            out_specs=tile,
            out_shape=jax.ShapeDtypeStruct((m, d), F32),
            compiler_params=_compiler_params(("parallel", "arbitrary"), blocks),
            name="outproj",
        )(*ys, w_all, x)
    return pl.pallas_call(
        functools.partial(_outproj_prenorm_kernel, width=width),
        grid=(m // tm, d // tn),
        in_specs=in_specs + [pl.BlockSpec((1, tn), lambda i, j: (0, j))],
        out_specs=[tile, tile, pl.BlockSpec((tm, 1), lambda i, j: (i, 0))],
        out_shape=[jax.ShapeDtypeStruct((m, d), F32), jax.ShapeDtypeStruct((m, d), BF16),
                   jax.ShapeDtypeStruct((m, 1), F32)],
        compiler_params=_compiler_params(("parallel", "arbitrary"), blocks + tm * tn * 2),
        name="outproj_prenorm",
    )(*ys, w_all, x, next_gain.reshape(1, d))


def kernel(x, norm_gain, w_in, conv_w, gmlp_v_gain, gmlp_w_s, gmlp_b_s, swa_sinks, w_out, final_gain):
    batch, seq, d_model = x.shape
    depth = w_in.shape[0]
    m = batch * seq
    w_a = w_b = w_c = w_d = d_model // 4
    n_sb_heads = w_a // HEAD_DIM
    n_swa_heads = w_b // HEAD_DIM
    n_swa_kv = max(1, n_swa_heads // 4)
    w_kv = n_swa_kv * HEAD_DIM
    n_groups = gmlp_w_s.shape[1]
    assert n_swa_heads == swa_sinks.shape[1] and w_d == n_groups * GMLP_GROUP_DIM

    widths = (w_a, w_a, w_a, w_a, w_b, w_kv, w_kv, w_b, w_c, w_c, w_c, w_c, w_d, w_d, w_d)
    offs = [0]
    for wd in widths:
        offs.append(offs[-1] + wd)
    (qa, ka, va, ga, qb, kb, vb, gb, xc, bc, cc, gc, ud, vd, gd) = offs[:-1]
    n_proj = offs[-1]
    assert n_proj == w_in.shape[2]

    scale = HEAD_DIM ** -0.5
    col = jnp.arange(n_proj)
    col_scale = jnp.where((col >= qa) & (col < qa + w_a), scale * LOG2_E,
                          jnp.where((col >= qb) & (col < qb + w_b), scale, 1.0))
    col_scale = col_scale.astype(F32).reshape(1, n_proj)

    w_out_bf16 = w_out.astype(BF16)
    h = x.reshape(m, d_model)
    hn, row_ssq = _rmsnorm(h, norm_gain[0], BF16), None
    for l in range(depth):
        proj = _inproj(hn, w_in, l, col_scale, row_ssq).reshape(batch, seq, n_proj)
        y_a = _sb_attention(proj, qa, ka, va, ga, n_sb_heads)
        bias_tw = jnp.repeat(gmlp_b_s[l].T, GMLP_GROUP_DIM, axis=1)
        y_b, y_c, y_d = _local_mixers(proj, swa_sinks[l], conv_w[l], gmlp_v_gain[l], gmlp_w_s[l], bias_tw,
                                      (qb, kb, vb, gb, xc, bc, cc, gc, ud, vd, gd), n_swa_heads, n_swa_kv, w_b)
        ys = [y.reshape(m, -1) for y in (y_a, y_b, y_c, y_d)]
        if l + 1 < depth:
            h, hn, row_ssq = _outproj(ys, w_out_bf16, l, h, norm_gain[l + 1])
        else:
            h = _outproj(ys, w_out_bf16, l, h)
    out = _rmsnorm(h, final_gain, F32)
    return out.reshape(batch, seq, d_model)
```
